```python
import math
import jax
import jax.numpy as jnp
from jax import lax
import numpy as np

D_MODEL = 2048
BATCH = 2
SEQ = 4096
DEPTH = 2
DEC_BATCH = 32
DEC_SEQ = 8
PAST_LEN = 8192
PAGE_SIZE = 128

HEAD_DIM = 64
D_CONV = D_MODEL // 4
D_SSM = D_MODEL // 4
D_ATTN = D_MODEL - D_CONV - D_SSM
CONV_A_WIDTH = 3
SSM_HEADS = D_SSM // HEAD_DIM
SSM_GROUPS = 2
SSM_STATE = 128
SSM_CONV_WIDTH = 4
SSM_CHUNK = 128
SSM_CONV_DIM = D_SSM + 2 * SSM_GROUPS * SSM_STATE
ATTN_HEADS = D_ATTN // HEAD_DIM
KV_HEADS = 4
Q_PER_KV = ATTN_HEADS // KV_HEADS
CMP_BLOCK = 32
CMP_STRIDE = 16
CMP_SPAN = CMP_BLOCK // CMP_STRIDE
CMP_HIDDEN = 128
SEL_BLOCK = 64
CMP_PER_SEL = SEL_BLOCK // CMP_STRIDE
TOP_BLOCKS = 16
WINDOW = 512
Q_BLOCK = 128
D_FF = ((8 * D_MODEL // 3 + 255) // 256) * 256
COLS_A = 3 * D_CONV
COLS_B = D_SSM + SSM_CONV_DIM + SSM_HEADS
KV_COLS = 3 * 2 * KV_HEADS * HEAD_DIM
COLS_C = D_ATTN + KV_COLS + 3 * ATTN_HEADS
D_IN_PROJ = COLS_A + COLS_B + COLS_C
RMS_EPS = 1e-6
NEG_INF = -1e30
TINY = 1e-30
FORCE_BONUS = 1e3
ATTN_SCALE = HEAD_DIM ** -0.5

kernel_name = 'hymba_shortconv_ssd_nsa_step'


def rms_norm(x, gain):
    xf = x.astype(jnp.float32)
    y = xf * lax.rsqrt(jnp.mean(xf * xf, axis=-1, keepdims=True) + RMS_EPS)
    return (y * gain.astype(jnp.float32)).astype(x.dtype)


def masked_softmax(s, mask):
    s = jnp.where(mask, s, NEG_INF)
    p = jnp.exp(s - jnp.max(s, axis=-1, keepdims=True)) * mask
    return p / jnp.maximum(jnp.sum(p, axis=-1, keepdims=True), TINY)


def causal_depthwise_conv(u, prefix, w):
    width = w.shape[0]
    t = u.shape[1]
    ext = jnp.concatenate([prefix.astype(u.dtype), u], axis=1)
    y = ext[:, 0:t] * w[0]
    for k in range(1, width):
        y = y + ext[:, k:k + t] * w[k]
    return y, ext[:, t:]


def short_conv_mixer(u_a, conv_prefix, conv_w):
    b_gate, c_gate, x_in = jnp.split(u_a, 3, axis=-1)
    y, new_prefix = causal_depthwise_conv(c_gate * x_in, conv_prefix, conv_w)
    return b_gate * y, new_prefix


def ssd_chunked_scan(x, dt, a_log, bm, cm, h0):
    bsz, t, h, p = x.shape
    n = bm.shape[-1]
    lc = math.gcd(t, SSM_CHUNK)
    nc = t // lc
    f32 = jnp.float32
    log_a = dt * (-jnp.exp(a_log.astype(f32)))
    xdt = (x.astype(f32) * dt[..., None]).reshape(bsz, nc, lc, h, p)
    bc = bm.astype(f32).reshape(bsz, nc, lc, h, n)
    cc = cm.astype(f32).reshape(bsz, nc, lc, h, n)
    cum = jnp.cumsum(log_a.reshape(bsz, nc, lc, h), axis=2)
    causal = jnp.tril(jnp.ones((lc, lc), dtype=bool))[None, None, :, :, None]
    seg = cum[:, :, :, None, :] - cum[:, :, None, :, :]
    decay = jnp.exp(jnp.where(causal, seg, NEG_INF))
    scores = jnp.einsum('bclhn,bcshn->bclsh', cc, bc) * decay
    y_intra = jnp.einsum('bclsh,bcshp->bclhp', scores, xdt)
    to_end = jnp.exp(cum[:, :, -1:, :] - cum)
    chunk_states = jnp.einsum('bclhn,bclh,bclhp->bchpn', bc, to_end, xdt)
    chunk_decay = jnp.exp(cum[:, :, -1, :])

    def step(state, inp):
        st, dec = inp
        return state * dec[:, :, None, None] + st, state

    h_final, h_enter = lax.scan(step, h0.astype(f32),
                                (jnp.moveaxis(chunk_states, 1, 0), jnp.moveaxis(chunk_decay, 1, 0)))
    h_enter = jnp.moveaxis(h_enter, 0, 1)
    y_inter = jnp.einsum('bclhn,bchpn,bclh->bclhp', cc, h_enter, jnp.exp(cum))
    return (y_intra + y_inter).reshape(bsz, t, h, p), h_final


def ssd_mixer(u_b, conv_prefix, h0, conv_w, conv_bias, dt_bias, a_log, d_skip):
    bsz, t, _ = u_b.shape
    z, xbc, dt_raw = jnp.split(u_b, [D_SSM, D_SSM + SSM_CONV_DIM], axis=-1)
    xbc_c, new_prefix = causal_depthwise_conv(xbc, conv_prefix, conv_w)
    xbc_c = jax.nn.silu(xbc_c + conv_bias)
    xs, bm, cm = jnp.split(xbc_c, [D_SSM, D_SSM + SSM_GROUPS * SSM_STATE], axis=-1)
    xs = xs.reshape(bsz, t, SSM_HEADS, HEAD_DIM)
    rep = SSM_HEADS // SSM_GROUPS
    bm = jnp.repeat(bm.reshape(bsz, t, SSM_GROUPS, SSM_STATE), rep, axis=2)
    cm = jnp.repeat(cm.reshape(bsz, t, SSM_GROUPS, SSM_STATE), rep, axis=2)
    dt = jax.nn.softplus(dt_raw.astype(jnp.float32) + dt_bias.astype(jnp.float32))
    y, h_final = ssd_chunked_scan(xs, dt, a_log, bm, cm, h0)
    y = y + d_skip.astype(jnp.float32)[:, None] * xs.astype(jnp.float32)
    y = y.reshape(bsz, t, D_SSM).astype(u_b.dtype) * jax.nn.silu(z)
    return y, new_prefix, h_final.astype(h0.dtype)


def compress_blocks(rows, pos_emb, w1, w2):
    bsz, lp, g, dh = rows.shape
    n_chunks = lp // CMP_STRIDE
    n_blocks = n_chunks - CMP_SPAN + 1
    chunks = rows.reshape(bsz, n_chunks, CMP_STRIDE, g, dh)
    blocks = jnp.concatenate([chunks[:, k:k + n_blocks] for k in range(CMP_SPAN)], axis=2)
    blocks = blocks + pos_emb[:, None, :]
    flat = jnp.transpose(blocks, (0, 1, 3, 2, 4)).reshape(bsz, n_blocks, g, CMP_BLOCK * dh)
    return jax.nn.gelu(flat @ w1) @ w2


def nsa_mixer(u_c, q0, k_past, v_past, win_k_prefix, win_v_prefix, n_win_keep,
              q_gain, k_gain, cmp_pos, cmp_w1, cmp_w2):
    bsz, t, _ = u_c.shape
    q_raw, kv, g_raw = jnp.split(u_c, [D_ATTN, D_ATTN + KV_COLS], axis=-1)
    q = rms_norm(q_raw.reshape(bsz, t, ATTN_HEADS, HEAD_DIM), q_gain)
    q = q.reshape(bsz, t, KV_HEADS, Q_PER_KV, HEAD_DIM)
    kv = kv.reshape(bsz, t, 3, 2, KV_HEADS, HEAD_DIM)
    gates = jax.nn.sigmoid(g_raw.reshape(bsz, t, KV_HEADS, Q_PER_KV, 3))
    k_sel_new = rms_norm(kv[:, :, 1, 0], k_gain[1])
    k_win_new = rms_norm(kv[:, :, 2, 0], k_gain[2])
    new_k_rows = jnp.stack([kv[:, :, 0, 0], k_sel_new], axis=2)
    new_v_rows = jnp.stack([kv[:, :, 0, 1], kv[:, :, 1, 1]], axis=2)

    k_full = jnp.concatenate([k_past.astype(u_c.dtype), new_k_rows], axis=1)
    v_full = jnp.concatenate([v_past.astype(u_c.dtype), new_v_rows], axis=1)
    length = k_full.shape[1]
    pad = (-length) % SEL_BLOCK
    k_full = jnp.pad(k_full, ((0, 0), (0, pad), (0, 0), (0, 0), (0, 0)))
    v_full = jnp.pad(v_full, ((0, 0), (0, pad), (0, 0), (0, 0), (0, 0)))
    lp = length + pad
    n_sel = lp // SEL_BLOCK
    topk = min(TOP_BLOCKS, n_sel)

    k_cmp = rms_norm(compress_blocks(k_full[:, :, 0], cmp_pos[0], cmp_w1[0], cmp_w2[0]), k_gain[0])
    v_cmp = compress_blocks(v_full[:, :, 0], cmp_pos[1], cmp_w1[1], cmp_w2[1])
    n_cmp = k_cmp.shape[1]
    cmp_end = jnp.arange(n_cmp, dtype=jnp.int32) * CMP_STRIDE + CMP_BLOCK - 1

    def to_blocks(r):
        r = r.reshape(bsz, n_sel, SEL_BLOCK, KV_HEADS, HEAD_DIM)
        return jnp.transpose(r, (0, 3, 1, 2, 4)).reshape(bsz, KV_HEADS, n_sel, SEL_BLOCK * HEAD_DIM)

    k_sel_blocks = to_blocks(k_full[:, :, 1])
    v_sel_blocks = to_blocks(v_full[:, :, 1])

    ext_k = jnp.concatenate([win_k_prefix.astype(u_c.dtype), k_win_new], axis=1)
    ext_v = jnp.concatenate([win_v_prefix.astype(u_c.dtype), kv[:, :, 2, 1]], axis=1)

    slopes = jnp.exp2(-8.0 * jnp.arange(1, ATTN_HEADS + 1, dtype=jnp.float32) / ATTN_HEADS)
    slopes = slopes.reshape(KV_HEADS, Q_PER_KV)
    qb_size = math.gcd(t, Q_BLOCK)
    n_qb = t // qb_size
    b_ix = jnp.arange(bsz)[:, None, None, None]
    g_ix = jnp.arange(KV_HEADS)[None, :, None, None]
    blk = jnp.arange(n_sel, dtype=jnp.int32)
    f32 = jnp.float32

    def block_fn(i):
        qs = i * qb_size
        qb = lax.dynamic_slice_in_dim(q, qs, qb_size, axis=1)
        gb = lax.dynamic_slice_in_dim(gates, qs, qb_size, axis=1)
        tpos = q0 + qs + jnp.arange(qb_size, dtype=jnp.int32)
        dist_c = tpos[:, None] - cmp_end[None, :]
        s_c = (jnp.einsum('bqgrd,bngd->bgrqn', qb, k_cmp).astype(f32) * ATTN_SCALE
               - slopes[:, :, None, None] * dist_c.astype(f32))
        p_c = masked_softmax(s_c, dist_c >= 0)
        o_c = jnp.einsum('bgrqn,bngd->bqgrd', p_c.astype(v_cmp.dtype), v_cmp)
        imp = jnp.pad(jnp.sum(p_c, axis=2), ((0, 0), (0, 0), (0, 0), (0, 1)))
        imp = jnp.sum(imp.reshape(bsz, KV_HEADS, qb_size, n_sel, CMP_PER_SEL), axis=-1)
        cur = tpos // SEL_BLOCK
        forced = (blk[None, :] == 0) | (blk[None, :] == cur[:, None]) | (blk[None, :] == cur[:, None] - 1)
        valid = blk[None, :] <= cur[:, None]
        imp = jnp.where(valid, imp + FORCE_BONUS * forced, NEG_INF)
        _, idx = lax.top_k(imp, topk)
        kg = k_sel_blocks[b_ix, g_ix, idx].reshape(bsz, KV_HEADS, qb_size, topk * SEL_BLOCK, HEAD_DIM)
        vg = v_sel_blocks[b_ix, g_ix, idx].reshape(bsz, KV_HEADS, qb_size, topk * SEL_BLOCK, HEAD_DIM)
        key_pos = (idx[..., None] * SEL_BLOCK + jnp.arange(SEL_BLOCK, dtype=jnp.int32))
        key_pos = key_pos.reshape(bsz, KV_HEADS, qb_size, topk * SEL_BLOCK)
        dist_s = tpos[None, None, :, None] - key_pos
        s_s = (jnp.einsum('bqgrd,bgqkd->bgrqk', qb, kg).astype(f32) * ATTN_SCALE
               - slopes[None, :, :, None, None] * dist_s[:, :, None].astype(f32))
        p_s = masked_softmax(s_s, (dist_s >= 0)[:, :, None])
        o_s = jnp.einsum('bgrqk,bgqkd->bqgrd', p_s.astype(vg.dtype), vg)
        kw = lax.dynamic_slice_in_dim(ext_k, qs, WINDOW + qb_size, axis=1)
        vw = lax.dynamic_slice_in_dim(ext_v, qs, WINDOW + qb_size, axis=1)
        key_pos_w = q0 - WINDOW + qs + jnp.arange(WINDOW + qb_size, dtype=jnp.int32)
        dist_w = tpos[:, None] - key_pos_w[None, :]
        mask_w = (dist_w >= 0) & (dist_w <= WINDOW) & (key_pos_w[None, :] >= 0)
        s_w = (jnp.einsum('bqgrd,bkgd->bgrqk', qb, kw).astype(f32) * ATTN_SCALE
               - slopes[:, :, None, None] * dist_w.astype(f32))
        p_w = masked_softmax(s_w, mask_w)
        o_w = jnp.einsum('bgrqk,bkgd->bqgrd', p_w.astype(vw.dtype), vw)
        return gb[..., 0:1] * o_c + gb[..., 1:2] * o_s + gb[..., 2:3] * o_w

    o = lax.map(block_fn, jnp.arange(n_qb, dtype=jnp.int32))
    o = jnp.moveaxis(o, 0, 1).reshape(bsz, t, D_ATTN)
    return o, new_k_rows, new_v_rows, ext_k[:, -n_win_keep:], ext_v[:, -n_win_keep:]


def decoder_layer(x, q0, k_past, v_past, win_k_prefix, win_v_prefix, n_win_keep,
                  conv_a_prefix, conv_b_prefix, ssm_h0, params):
    (g_mix, w_in_l, conv_a_w_l, conv_b_w_l, conv_b_bias_l, dt_bias_l, a_log_l, d_skip_l,
     q_gain, k_gain, cmp_pos_l, cmp_w1_l, cmp_w2_l, g_out, w_out_l, g_ffn,
     w_gate_l, w_up_l, w_down_l) = params
    u = rms_norm(x, g_mix) @ w_in_l
    u_a, u_b, u_c = jnp.split(u, [COLS_A, COLS_A + COLS_B], axis=-1)
    y_a, new_conv_a = short_conv_mixer(u_a, conv_a_prefix, conv_a_w_l)
    y_b, new_conv_b, new_ssm = ssd_mixer(u_b, conv_b_prefix, ssm_h0, conv_b_w_l, conv_b_bias_l,
                                         dt_bias_l, a_log_l, d_skip_l)
    y_c, new_k, new_v, new_win_k, new_win_v = nsa_mixer(
        u_c, q0, k_past, v_past, win_k_prefix, win_v_prefix, n_win_keep,
        q_gain, k_gain, cmp_pos_l, cmp_w1_l, cmp_w2_l)
    merged = jnp.concatenate([
        rms_norm(y_a, g_out[:D_CONV]),
        rms_norm(y_b, g_out[D_CONV:D_CONV + D_SSM]),
        rms_norm(y_c, g_out[D_CONV + D_SSM:])], axis=-1)
    x = x + merged @ w_out_l
    h = rms_norm(x, g_ffn)
    x = x + (jax.nn.silu(h @ w_gate_l) * (h @ w_up_l)) @ w_down_l
    return x, (new_k, new_v, new_win_k, new_win_v, new_conv_a, new_conv_b, new_ssm)


def setup_inputs(seed: int = 0) -> dict:
    key = jax.random.key(seed)
    ks = jax.random.split(key, 32)
    f32 = jnp.float32
    n_pages = PAST_LEN // PAGE_SIZE
    n_used = DEC_BATCH * n_pages
    n_pool = n_used + (n_used + 3) // 4
    win_buf = min(WINDOW, PAST_LEN)

    def normal(k, shape, scale):
        return jax.random.normal(k, shape, f32) * scale

    def gain(k, shape):
        return 1.0 + normal(k, shape, 0.02)

    dt_init = jnp.exp(jax.random.uniform(ks[12], (DEPTH, SSM_HEADS), f32, math.log(1e-3), math.log(1e-1)))
    return {
        'x_prompt': normal(ks[0], (BATCH, SEQ, D_MODEL), 1.0),
        'x_sample': normal(ks[1], (DEC_BATCH, DEC_SEQ, D_MODEL), 1.0),
        'cache_k': normal(ks[2], (DEPTH, n_pool, PAGE_SIZE, 2, KV_HEADS, HEAD_DIM), 1.0),
        'cache_v': normal(ks[3], (DEPTH, n_pool, PAGE_SIZE, 2, KV_HEADS, HEAD_DIM), 1.0),
        'cache_win_k': normal(ks[4], (DEPTH, DEC_BATCH, win_buf, KV_HEADS, HEAD_DIM), 1.0),
        'cache_win_v': normal(ks[5], (DEPTH, DEC_BATCH, win_buf, KV_HEADS, HEAD_DIM), 1.0),
        'state_conv_a': normal(ks[6], (DEPTH, DEC_BATCH, CONV_A_WIDTH - 1, D_CONV), 1.0),
        'state_conv_b': normal(ks[7], (DEPTH, DEC_BATCH, SSM_CONV_WIDTH - 1, SSM_CONV_DIM), 1.0),
        'state_ssm': normal(ks[8], (DEPTH, DEC_BATCH, SSM_HEADS, HEAD_DIM, SSM_STATE), 0.5),
        'page_table': jax.random.permutation(ks[9], n_pool)[:n_used].reshape(DEC_BATCH, n_pages).astype(jnp.int32),
        'norm_mix': gain(ks[10], (DEPTH, D_MODEL)),
        'w_in': normal(ks[11], (DEPTH, D_MODEL, D_IN_PROJ), D_MODEL ** -0.5),
        'conv_a_w': normal(ks[13], (DEPTH, CONV_A_WIDTH, D_CONV), CONV_A_WIDTH ** -0.5),
        'conv_b_w': normal(ks[14], (DEPTH, SSM_CONV_WIDTH, SSM_CONV_DIM), SSM_CONV_WIDTH ** -0.5),
        'conv_b_bias': normal(ks[15], (DEPTH, SSM_CONV_DIM), 0.02),
        'dt_bias': dt_init + jnp.log(-jnp.expm1(-dt_init)),
        'a_log': jnp.log(jax.random.uniform(ks[16], (DEPTH, SSM_HEADS), f32, 1.0, 16.0)),
        'd_skip': 1.0 + normal(ks[17], (DEPTH, SSM_HEADS), 0.1),
        'q_norm': gain(ks[18], (DEPTH, HEAD_DIM)),
        'k_norm': gain(ks[19], (DEPTH, 3, HEAD_DIM)),
        'cmp_pos': normal(ks[20], (DEPTH, 2, CMP_BLOCK, HEAD_DIM), 0.1),
        'cmp_w1': normal(ks[21], (DEPTH, 2, CMP_BLOCK * HEAD_DIM, CMP_HIDDEN), (CMP_BLOCK * HEAD_DIM) ** -0.5),
        'cmp_w2': normal(ks[22], (DEPTH, 2, CMP_HIDDEN, HEAD_DIM), CMP_HIDDEN ** -0.5),
        'norm_out': gain(ks[23], (DEPTH, D_MODEL)),
        'w_out': normal(ks[24], (DEPTH, D_MODEL, D_MODEL), D_MODEL ** -0.5),
        'norm_ffn': gain(ks[25], (DEPTH, D_MODEL)),
        'w_gate': normal(ks[26], (DEPTH, D_MODEL, D_FF), D_MODEL ** -0.5),
        'w_up': normal(ks[27], (DEPTH, D_MODEL, D_FF), D_MODEL ** -0.5),
        'w_down': normal(ks[28], (DEPTH, D_FF, D_MODEL), D_FF ** -0.5),
    }


def reference(x_prompt, x_sample, cache_k, cache_v, cache_win_k, cache_win_v,
              state_conv_a, state_conv_b, state_ssm, page_table,
              norm_mix, w_in, conv_a_w, conv_b_w, conv_b_bias, dt_bias, a_log, d_skip,
              q_norm, k_norm, cmp_pos, cmp_w1, cmp_w2, norm_out, w_out, norm_ffn,
              w_gate, w_up, w_down):
    bsz, t_prompt, _ = x_prompt.shape
    dec_b = x_sample.shape[0]
    dtype = x_prompt.dtype
    n_pages = page_table.shape[1]
    past_len = n_pages * PAGE_SIZE
    win_buf = cache_win_k.shape[2]
    win_pad = WINDOW - win_buf
    hp, hs = x_prompt, x_sample
    prompt_states = []
    sample_states = []
    for l in range(DEPTH):
        params = (norm_mix[l], w_in[l], conv_a_w[l], conv_b_w[l], conv_b_bias[l], dt_bias[l],
                  a_log[l], d_skip[l], q_norm[l], k_norm[l], cmp_pos[l], cmp_w1[l], cmp_w2[l],
                  norm_out[l], w_out[l], norm_ffn[l], w_gate[l], w_up[l], w_down[l])
        hp, st_p = decoder_layer(
            hp, 0,
            jnp.zeros((bsz, 0, 2, KV_HEADS, HEAD_DIM), dtype),
            jnp.zeros((bsz, 0, 2, KV_HEADS, HEAD_DIM), dtype),
            jnp.zeros((bsz, WINDOW, KV_HEADS, HEAD_DIM), dtype),
            jnp.zeros((bsz, WINDOW, KV_HEADS, HEAD_DIM), dtype),
            min(WINDOW, t_prompt),
            jnp.zeros((bsz, CONV_A_WIDTH - 1, D_CONV), dtype),
            jnp.zeros((bsz, SSM_CONV_WIDTH - 1, SSM_CONV_DIM), dtype),
            jnp.zeros((bsz, SSM_HEADS, HEAD_DIM, SSM_STATE), dtype),
            params)
        k_past = cache_k[l][page_table].reshape(dec_b, past_len, 2, KV_HEADS, HEAD_DIM)
        v_past = cache_v[l][page_table].reshape(dec_b, past_len, 2, KV_HEADS, HEAD_DIM)
        wk = jnp.pad(cache_win_k[l], ((0, 0), (win_pad, 0), (0, 0), (0, 0)))
        wv = jnp.pad(cache_win_v[l], ((0, 0), (win_pad, 0), (0, 0), (0, 0)))
        hs, st_s = decoder_layer(hs, past_len, k_past, v_past, wk, wv, win_buf,
                                 state_conv_a[l], state_conv_b[l], state_ssm[l], params)
        prompt_states.append(st_p)
        sample_states.append(st_s)

    def stack_state(states, j):
        return jnp.stack([s[j] for s in states], axis=0)

    new_k_prompt = stack_state(prompt_states, 0)
    new_v_prompt = stack_state(prompt_states, 1)
    new_win_k_prompt = stack_state(prompt_states, 2)
    new_win_v_prompt = stack_state(prompt_states, 3)
    new_conv_a_prompt = stack_state(prompt_states, 4)
    new_conv_b_prompt = stack_state(prompt_states, 5)
    new_ssm_prompt = stack_state(prompt_states, 6)
    new_k_sample = stack_state(sample_states, 0)
    new_v_sample = stack_state(sample_states, 1)
    new_win_k_sample = stack_state(sample_states, 2)
    new_win_v_sample = stack_state(sample_states, 3)
    new_conv_a_sample = stack_state(sample_states, 4)
    new_conv_b_sample = stack_state(sample_states, 5)
    new_ssm_sample = stack_state(sample_states, 6)
    return (hp, hs, new_k_prompt, new_v_prompt, new_win_k_prompt, new_win_v_prompt,
            new_conv_a_prompt, new_conv_b_prompt, new_ssm_prompt,
            new_k_sample, new_v_sample, new_win_k_sample, new_win_v_sample,
            new_conv_a_sample, new_conv_b_sample, new_ssm_sample)
```

```python
import functools

import jax
import jax.numpy as jnp
from jax import lax
from jax.experimental import pallas as pl
from jax.experimental.pallas import tpu as pltpu

F32 = jnp.float32
BF16 = jnp.bfloat16

D_MODEL = 2048
HEAD_DIM = 64
D_CONV = 512
D_SSM = 512
D_ATTN = 1024
CONV_A_WIDTH = 3
SSM_HEADS = 8
SSM_STATE = 128
SSM_CONV_WIDTH = 4
SSM_CHUNK = 128
SSM_CONV_DIM = 1024
ATTN_HEADS = 16
KV_HEADS = 4
Q_PER_KV = 4
KV_COLS = 1536
CMP_STRIDE = 16
CMP_HIDDEN = 128
SEL_BLOCK = 64
TOP_BLOCKS = 16
WINDOW = 512
Q_BLOCK = 128
PAGE_SIZE = 128
PAGE_COLS = 512
D_FF = 5632
RMS_EPS = 1e-6
NEG_INF = -1e30
TINY = 1e-30
FORCE_BONUS = 1e3
ATTN_SCALE = HEAD_DIM ** -0.5

U_KV = 0
U_A = 1536
U_XBC = 3072
U_Q = 4096
U_Z = 5120
U_MISC = 5632
U_COLS = 5760
GATE_LANE0 = 8

LANES = 128
KEY_TILE = 128
SEL_CHUNK = 256
VMEM_LIMIT_BYTES = 56 * 2 ** 20

NT_DIMS = (((1,), (1,)), ((), ()))


def _cparams(*sem):
    return pltpu.CompilerParams(dimension_semantics=sem, vmem_limit_bytes=VMEM_LIMIT_BYTES)


def _tile(n, pref):
    t = min(n, pref)
    while n % t:
        t //= 2
    return t


def _dot(a, b):
    return jnp.dot(a, b, preferred_element_type=F32)


def _split3(x):
    hi = x.astype(BF16)
    r1 = x - hi.astype(F32)
    mid = r1.astype(BF16)
    lo = (r1 - mid.astype(F32)).astype(BF16)
    return hi, mid, lo


def _lo_half(rows):
    return lax.broadcasted_iota(jnp.int32, (rows, LANES), 1) < HEAD_DIM


def _swap_halves(x):
    return pltpu.roll(x, HEAD_DIM, 1)


def _store_kv_t(dst, k_t, v_t):
    for g in range(KV_HEADS):
        view = dst(g)
        view[0:HEAD_DIM, :] = k_t[g * HEAD_DIM:(g + 1) * HEAD_DIM].astype(BF16)
        view[HEAD_DIM:2 * HEAD_DIM, :] = v_t[g * HEAD_DIM:(g + 1) * HEAD_DIM].astype(BF16)


def _in_proj_kernel(x_ref, g_ref, w_ref, o_ref, xn_ref):
    @pl.when(pl.program_id(1) == 0)
    def _():
        x = x_ref[...]
        ms = jnp.mean(x * x, axis=-1, keepdims=True)
        xn_ref[...] = (x * lax.rsqrt(ms + RMS_EPS) * g_ref[...]).astype(BF16)

    o_ref[...] = _dot(xn_ref[...], w_ref[...])


def _in_proj(x, gain, w):
    n = x.shape[0]
    tm = _tile(n, 512)
    tn = 1152
    return pl.pallas_call(
        _in_proj_kernel,
        grid=(n // tm, U_COLS // tn),
        in_specs=[pl.BlockSpec((tm, D_MODEL), lambda i, j: (i, 0)),
                  pl.BlockSpec((1, D_MODEL), lambda i, j: (0, 0)),
                  pl.BlockSpec((D_MODEL, tn), lambda i, j: (0, j))],
        out_specs=pl.BlockSpec((tm, tn), lambda i, j: (i, j)),
        out_shape=jax.ShapeDtypeStruct((n, U_COLS), F32),
        scratch_shapes=[pltpu.VMEM((tm, D_MODEL), BF16)],
        compiler_params=_cparams("parallel", "arbitrary"),
        name="in_proj",
    )(x, gain, w)


def _out_proj_kernel(x_ref, ya_ref, yb_ref, yc_ref, g_ref, w_ref, o_ref, mg_ref):
    @pl.when(pl.program_id(1) == 0)
    def _():
        def nrm(y, g):
            ms = jnp.mean(y * y, axis=-1, keepdims=True)
            return (y * lax.rsqrt(ms + RMS_EPS) * g).astype(BF16)

        mg_ref[:, 0:512] = nrm(ya_ref[...], g_ref[:, 0:512])
        mg_ref[:, 512:1024] = nrm(yb_ref[...], g_ref[:, 512:1024])
        mg_ref[:, 1024:2048] = nrm(yc_ref[...], g_ref[:, 1024:2048])

    o_ref[...] = x_ref[...] + _dot(mg_ref[...], w_ref[...])


def _out_proj(x, ya, yb, yc, gain, w):
    n = x.shape[0]
    tm = _tile(n, 512)
    tn = 1024
    return pl.pallas_call(
        _out_proj_kernel,
        grid=(n // tm, D_MODEL // tn),
        in_specs=[pl.BlockSpec((tm, tn), lambda i, j: (i, j)),
                  pl.BlockSpec((tm, D_CONV), lambda i, j: (i, 0)),
                  pl.BlockSpec((tm, D_SSM), lambda i, j: (i, 0)),
                  pl.BlockSpec((tm, D_ATTN), lambda i, j: (i, 0)),
                  pl.BlockSpec((1, D_MODEL), lambda i, j: (0, 0)),
                  pl.BlockSpec((D_MODEL, tn), lambda i, j: (0, j))],
        out_specs=pl.BlockSpec((tm, tn), lambda i, j: (i, j)),
        out_shape=jax.ShapeDtypeStruct((n, D_MODEL), F32),
        scratch_shapes=[pltpu.VMEM((tm, D_MODEL), BF16)],
        compiler_params=_cparams("parallel", "arbitrary"),
        name="out_proj",
    )(x, ya, yb, yc, gain, w)


def _ffn_kernel(x_ref, g_ref, wg_ref, wu_ref, wd_ref, o_ref, h_ref, acc_ref):
    f = pl.program_id(1)

    @pl.when(f == 0)
    def _():
        x = x_ref[...]
        ms = jnp.mean(x * x, axis=-1, keepdims=True)
        h_ref[...] = (x * lax.rsqrt(ms + RMS_EPS) * g_ref[...]).astype(BF16)
        acc_ref[...] = jnp.zeros_like(acc_ref)

    h = h_ref[...]
    a = _dot(h, wg_ref[...])
    b = _dot(h, wu_ref[...])
    acc_ref[...] += _dot((a * jax.nn.sigmoid(a) * b).astype(BF16), wd_ref[...])

    @pl.when(f == pl.num_programs(1) - 1)
    def _():
        o_ref[...] = x_ref[...] + acc_ref[...]


def _ffn(x, gain, wg, wu, wd):
    n = x.shape[0]
    tm = _tile(n, 512)
    tf = 512
    return pl.pallas_call(
        _ffn_kernel,
        grid=(n // tm, D_FF // tf),
        in_specs=[pl.BlockSpec((tm, D_MODEL), lambda i, f: (i, 0)),
                  pl.BlockSpec((1, D_MODEL), lambda i, f: (0, 0)),
                  pl.BlockSpec((D_MODEL, tf), lambda i, f: (0, f)),
                  pl.BlockSpec((D_MODEL, tf), lambda i, f: (0, f)),
                  pl.BlockSpec((tf, D_MODEL), lambda i, f: (f, 0))],
        out_specs=pl.BlockSpec((tm, D_MODEL), lambda i, f: (i, 0)),
        out_shape=jax.ShapeDtypeStruct((n, D_MODEL), F32),
        scratch_shapes=[pltpu.VMEM((tm, D_MODEL), BF16), pltpu.VMEM((tm, D_MODEL), F32)],
        compiler_params=_cparams("parallel", "arbitrary"),
        name="ffn",
    )(x, gain, wg, wu, wd)


def _mixer_a_kernel(u_ref, pre_ref, w_ref, y_ref, newpre_ref, ext_ref, *, rows):
    @pl.when(pl.program_id(1) == 0)
    def _():
        ext_ref[6:8, :] = pre_ref[0]

    u = u_ref[...]
    v = u[:, 512:1024] * u[:, 1024:1536]
    ext_ref[8:8 + rows, :] = v
    w = w_ref[...]
    y = ext_ref[6:6 + rows, :] * w[0:1] + ext_ref[7:7 + rows, :] * w[1:2] + v * w[2:3]
    y_ref[...] = u[:, 0:512] * y
    last = ext_ref[8 + rows - 2:8 + rows, :]
    ext_ref[6:8, :] = last
    newpre_ref[0] = last


def _mixer_a(u, prefix, w, nseq, t):
    rows = _tile(t, 512)
    nblk = t // rows
    return pl.pallas_call(
        functools.partial(_mixer_a_kernel, rows=rows),
        grid=(nseq, nblk),
        in_specs=[pl.BlockSpec((rows, 3 * D_CONV), lambda s, j: (s * nblk + j, U_A // (3 * D_CONV))),
                  pl.BlockSpec((1, CONV_A_WIDTH - 1, D_CONV), lambda s, j: (s, 0, 0)),
                  pl.BlockSpec((CONV_A_WIDTH, D_CONV), lambda s, j: (0, 0))],
        out_specs=[pl.BlockSpec((rows, D_CONV), lambda s, j: (s * nblk + j, 0)),
                   pl.BlockSpec((1, CONV_A_WIDTH - 1, D_CONV), lambda s, j: (s, 0, 0))],
        out_shape=[jax.ShapeDtypeStruct((nseq * t, D_CONV), F32),
                   jax.ShapeDtypeStruct((nseq, CONV_A_WIDTH - 1, D_CONV), F32)],
        scratch_shapes=[pltpu.VMEM((8 + rows, D_CONV), F32)],
        compiler_params=_cparams("parallel", "arbitrary"),
        name="mixer_a",
    )(u, prefix, w)


def _ssd_kernel(z_ref, xbc_ref, misc_ref, pre_ref, h0_ref, cw_ref, cb_ref, dtb_ref, alog_ref, dsk_ref,
                y_ref, newpre_ref, hout_ref, ext_ref, st_ref, pad_ref, *, lin):
    L = SSM_CHUNK
    c = pl.program_id(1)

    @pl.when(c == 0)
    def _():
        if lin < L:
            ext_ref[...] = jnp.zeros_like(ext_ref)
            pad_ref[...] = jnp.zeros_like(pad_ref)
        ext_ref[5:8, :] = pre_ref[0]
        st_ref[...] = h0_ref[0].reshape(SSM_HEADS * HEAD_DIM, SSM_STATE)

    ext_ref[8:8 + lin, :] = xbc_ref[...]
    cw = cw_ref[...]
    acc = (ext_ref[5:5 + L, :] * cw[0:1] + ext_ref[6:6 + L, :] * cw[1:2]
           + ext_ref[7:7 + L, :] * cw[2:3] + ext_ref[8:8 + L, :] * cw[3:4])
    acc = acc + cb_ref[...]
    xbc = acc * jax.nn.sigmoid(acc)
    newp = ext_ref[8 + lin - 3:8 + lin, :]
    ext_ref[5:8, :] = newp
    newpre_ref[0] = newp

    if lin < L:
        pad_ref[0:lin, :] = misc_ref[...]
        misc = pad_ref[...]
    else:
        misc = misc_ref[...]
    rows = lax.broadcasted_iota(jnp.int32, (L, L), 0)
    cols = lax.broadcasted_iota(jnp.int32, (L, L), 1)
    x = misc + dtb_ref[...]
    dt = jnp.maximum(x, 0.0) + jnp.log1p(jnp.exp(-jnp.abs(x)))
    if lin < L:
        dt = jnp.where(rows < lin, dt, 0.0)
    la = dt * (-jnp.exp(alog_ref[...]))

    causal = rows >= cols
    tril = jnp.where(causal, 1.0, 0.0).astype(BF16)
    hi, mid, lo = _split3(la)
    cum = _dot(tril, hi) + _dot(tril, mid) + _dot(tril, lo)
    cum_t = cum.T
    last = cum[L - 1:L, :]
    lo_half = cols < HEAD_DIM
    top_rows = rows < HEAD_DIM

    z = z_ref[...]
    dsk = dsk_ref[...]
    for g in range(2):
        bg = xbc[:, 512 + g * 128:512 + (g + 1) * 128].astype(BF16)
        cg = xbc[:, 768 + g * 128:768 + (g + 1) * 128].astype(BF16)
        gram = lax.dot_general(cg, bg, NT_DIMS, preferred_element_type=F32)
        for i in (2 * g, 2 * g + 1):
            a, b = 2 * i, 2 * i + 1
            sl = slice(i * LANES, (i + 1) * LANES)
            xs_p = xbc[:, sl]
            col_a = cum[:, a:a + 1]
            col_b = cum[:, b:b + 1]
            xdt = xs_p * jnp.where(lo_half, dt[:, a:a + 1], dt[:, b:b + 1])
            dec_a = jnp.exp(jnp.where(causal, col_a - cum_t[a:a + 1, :], NEG_INF))
            dec_b = jnp.exp(jnp.where(causal, col_b - cum_t[b:b + 1, :], NEG_INF))
            xa = jnp.where(lo_half, xdt, 0.0).astype(BF16)
            xb = jnp.where(lo_half, 0.0, xdt).astype(BF16)
            y_intra = _dot((gram * dec_a).astype(BF16), xa) + _dot((gram * dec_b).astype(BF16), xb)
            st = st_ref[sl, :]
            y_inter = (lax.dot_general(cg, st.astype(BF16), NT_DIMS, preferred_element_type=F32)
                       * jnp.where(lo_half, jnp.exp(col_a), jnp.exp(col_b)))
            to_end = jnp.where(lo_half, jnp.exp(last[:, a:a + 1] - col_a), jnp.exp(last[:, b:b + 1] - col_b))
            xw_t = (xdt * to_end).T.astype(BF16)
            decay = jnp.where(top_rows, jnp.exp(last[:, a:a + 1]), jnp.exp(last[:, b:b + 1]))
            st_ref[sl, :] = st * decay + _dot(xw_t, bg)
            y = y_intra + y_inter + dsk[:, sl] * xs_p
            zp = z[:, sl]
            y_ref[:, sl] = y[0:lin] * (zp * jax.nn.sigmoid(zp))

    @pl.when(c == pl.num_programs(1) - 1)
    def _():
        hout_ref[0] = st_ref[...].reshape(SSM_HEADS, HEAD_DIM, SSM_STATE)


def _ssd(u, prefix, h0, cw, cb, dtb, alog, dsk, nseq, t):
    lin = min(t, SSM_CHUNK)
    nc = t // lin
    full = lambda shape: pl.BlockSpec(shape, lambda s, c: (0,) * len(shape))
    return pl.pallas_call(
        functools.partial(_ssd_kernel, lin=lin),
        grid=(nseq, nc),
        in_specs=[pl.BlockSpec((lin, D_SSM), lambda s, c: (s * nc + c, U_Z // D_SSM)),
                  pl.BlockSpec((lin, SSM_CONV_DIM), lambda s, c: (s * nc + c, U_XBC // SSM_CONV_DIM)),
                  pl.BlockSpec((lin, LANES), lambda s, c: (s * nc + c, U_MISC // LANES)),
                  pl.BlockSpec((1, SSM_CONV_WIDTH - 1, SSM_CONV_DIM), lambda s, c: (s, 0, 0)),
                  pl.BlockSpec((1, SSM_HEADS, HEAD_DIM, SSM_STATE), lambda s, c: (s, 0, 0, 0)),
                  full((SSM_CONV_WIDTH, SSM_CONV_DIM)), full((1, SSM_CONV_DIM)),
                  full((1, LANES)), full((1, LANES)), full((1, D_SSM))],
        out_specs=[pl.BlockSpec((lin, D_SSM), lambda s, c: (s * nc + c, 0)),
                   pl.BlockSpec((1, SSM_CONV_WIDTH - 1, SSM_CONV_DIM), lambda s, c: (s, 0, 0)),
                   pl.BlockSpec((1, SSM_HEADS, HEAD_DIM, SSM_STATE), lambda s, c: (s, 0, 0, 0))],
        out_shape=[jax.ShapeDtypeStruct((nseq * t, D_SSM), F32),
                   jax.ShapeDtypeStruct((nseq, SSM_CONV_WIDTH - 1, SSM_CONV_DIM), F32),
                   jax.ShapeDtypeStruct((nseq, SSM_HEADS, HEAD_DIM, SSM_STATE), F32)],
        scratch_shapes=[pltpu.VMEM((8 + SSM_CHUNK, SSM_CONV_DIM), F32),
                        pltpu.VMEM((SSM_HEADS * HEAD_DIM, SSM_STATE), F32),
                        pltpu.VMEM((SSM_CHUNK, LANES), F32)],
        compiler_params=_cparams("parallel", "arbitrary"),
        name="ssd",
    )(u, u, u, prefix, h0, cw, cb, dtb, alog, dsk)


def _headnorm(x, mavg, gain):
    x2 = x * x
    hi = x2.astype(BF16)
    lo = (x2 - hi.astype(F32)).astype(BF16)
    ms = _dot(hi, mavg) + _dot(lo, mavg)
    return x * lax.rsqrt(ms + RMS_EPS) * gain


def _prep_kernel(q_ref, kv_ref, mavg_ref, qg_ref, kg1_ref, kg2_ref,
                 qhm_ref, nk_ref, nv_ref, kwin_ref, *maybe_kvwin_ref):
    mavg = mavg_ref[...]
    lo = _lo_half(q_ref.shape[0])
    for c in range(4):
        qn = _headnorm(q_ref[:, c * 256:(c + 1) * 256], mavg, qg_ref[...]) * ATTN_SCALE
        for cc in range(2):
            col = qn[:, cc * LANES:(cc + 1) * LANES]
            h = c * 4 + cc * 2
            qhm_ref[h] = jnp.where(lo, col, 0.0).astype(qhm_ref.dtype)
            qhm_ref[h + 1] = jnp.where(lo, _swap_halves(col), 0.0).astype(qhm_ref.dtype)
    ksel = _headnorm(kv_ref[:, 512:768], mavg, kg1_ref[...])
    kwin = _headnorm(kv_ref[:, 1024:1280], mavg, kg2_ref[...])
    nk_ref[:, 0:256] = kv_ref[:, 0:256]
    nk_ref[:, 256:512] = ksel
    nv_ref[:, 0:256] = kv_ref[:, 256:512]
    nv_ref[:, 256:512] = kv_ref[:, 768:1024]
    kwin_ref[...] = kwin
    if maybe_kvwin_ref:
        kvwin_ref, = maybe_kvwin_ref
        for c in range(q_ref.shape[0] // KEY_TILE):
            rs = slice(c * KEY_TILE, (c + 1) * KEY_TILE)
            _store_kv_t(lambda g: kvwin_ref.at[g, c], kwin[rs].T, kv_ref[rs, 1280:1536].T)


def _prep(u, mavg, qg, kg1, kg2, q_dtype, emit_window_tiles):
    n = u.shape[0]
    tm = _tile(n, 256)
    full = lambda shape: pl.BlockSpec(shape, lambda i: (0,) * len(shape))
    out_specs = [pl.BlockSpec((ATTN_HEADS, tm, LANES), lambda i: (0, i, 0)),
                 pl.BlockSpec((tm, PAGE_COLS), lambda i: (i, 0)),
                 pl.BlockSpec((tm, PAGE_COLS), lambda i: (i, 0)),
                 pl.BlockSpec((tm, 256), lambda i: (i, 0))]
    out_shape = [jax.ShapeDtypeStruct((ATTN_HEADS, n, LANES), q_dtype),
                 jax.ShapeDtypeStruct((n, PAGE_COLS), F32),
                 jax.ShapeDtypeStruct((n, PAGE_COLS), F32),
                 jax.ShapeDtypeStruct((n, 256), F32)]
    if emit_window_tiles:
        out_specs.append(pl.BlockSpec((KV_HEADS, tm // KEY_TILE, 2 * HEAD_DIM, KEY_TILE), lambda i: (0, i, 0, 0)))
        out_shape.append(jax.ShapeDtypeStruct((KV_HEADS, n // KEY_TILE, 2 * HEAD_DIM, KEY_TILE), BF16))
    return pl.pallas_call(
        _prep_kernel,
        grid=(n // tm,),
        in_specs=[pl.BlockSpec((tm, D_ATTN), lambda i: (i, U_Q // D_ATTN)),
                  pl.BlockSpec((tm, KV_COLS), lambda i: (i, 0)),
                  full((256, 256)), full((1, 256)), full((1, 256)), full((1, 256))],
        out_specs=out_specs,
        out_shape=out_shape,
        compiler_params=_cparams("parallel"),
        name="nsa_prep",
    )(u, u, mavg, qg, kg1, kg2)


PAGES_PER_STEP = 8


def _gelu_tanh(x):
    return 0.5 * x * (1.0 + jnp.tanh(0.7978845608028654 * (x + 0.044715 * (x * x * x))))


def _pagepass_kernel(pt_ref, *refs, nchunk, feature_major):
    pp = PAGES_PER_STEP

    def cmp_rows(page):
        return page[0, 0:256, :].T if feature_major else page[0, :, 0:256]

    def sel_t(page):
        return page[0, 256:512, :] if feature_major else page[0, :, 256:512].T

    kpages = refs[0:pp]
    vpages = refs[pp:2 * pp]
    pos_ref, w1k_ref, w1v_ref, w2k_ref, w2v_ref, kg_ref = refs[2 * pp:2 * pp + 6]
    kvcmp_ref, kvsel_ref = refs[2 * pp + 6:2 * pp + 8]
    xk_ref, xv_ref, hs_ref, stage_ref = refs[2 * pp + 8:]
    j = pl.program_id(1)
    lo16 = _lo_half(16)

    for i2 in range(pp // 2):
        r0 = pl.multiple_of((j * pp + 2 * i2) * 8, 16)
        for pages, xs_ref in ((kpages, xk_ref), (vpages, xv_ref)):
            for half in range(2):
                rows = cmp_rows(pages[2 * i2 + half])
                for c in range(2):
                    stage_ref[half * 2 + c] = rows[:, c * LANES:(c + 1) * LANES]
            for a in range(8):
                for c in range(2):
                    ec = jnp.concatenate([stage_ref[c, pl.ds(2 * a, 8, stride=16), :],
                                          stage_ref[2 + c, pl.ds(2 * a, 8, stride=16), :]], axis=0)
                    oc = jnp.concatenate([stage_ref[c, pl.ds(2 * a + 1, 8, stride=16), :],
                                          stage_ref[2 + c, pl.ds(2 * a + 1, 8, stride=16), :]], axis=0)
                    xs_ref[2 * c, pl.ds(r0, 16), a * LANES:(a + 1) * LANES] = (
                        jnp.where(lo16, ec, _swap_halves(oc)).astype(BF16))
                    xs_ref[2 * c + 1, pl.ds(r0, 16), a * LANES:(a + 1) * LANES] = (
                        jnp.where(lo16, _swap_halves(ec), oc).astype(BF16))

    per_chunk = SEL_CHUNK // PAGE_SIZE
    for i in range(pp):
        ls = slice((i % per_chunk) * PAGE_SIZE, (i % per_chunk + 1) * PAGE_SIZE)
        _store_kv_t(lambda g: kvsel_ref.at[0, g, i // per_chunk, :, ls], sel_t(kpages[i]), sel_t(vpages[i]))

    @pl.when(j == pl.num_programs(1) - 1)
    def _():
        hs_ref[nchunk:nchunk + 8, :] = jnp.zeros((8, 256), F32)
        pos = pos_ref[...]
        phi = pos.astype(BF16)
        plo = (pos - phi.astype(F32)).astype(BF16)
        bias_k = _dot(phi, w1k_ref[...]) + _dot(plo, w1k_ref[...])
        bias_v = _dot(phi, w1v_ref[...]) + _dot(plo, w1v_ref[...])
        bias_k = bias_k[0:1, 0:128] + bias_k[1:2, 128:256]
        bias_v = bias_v[2:3, 0:128] + bias_v[3:4, 128:256]

        def summarise(xs, w1_ref, bias, w2_ref):
            hs_ref[0:nchunk, :] = _dot(xs, w1_ref[...])
            pre = hs_ref[0:nchunk, 0:128] + hs_ref[1:nchunk + 1, 128:256] + bias
            return _dot(_gelu_tanh(pre).astype(BF16), w2_ref[...])

        for g in range(KV_HEADS):
            ko = summarise(xk_ref[g], w1k_ref, bias_k, w2k_ref)
            vo = summarise(xv_ref[g], w1v_ref, bias_v, w2v_ref)
            ms = jnp.sum(ko * ko, axis=-1, keepdims=True) * (1.0 / HEAD_DIM)
            kvcmp_ref[0, g] = (ko * lax.rsqrt(ms + RMS_EPS) * kg_ref[...] + vo).astype(BF16)


def _pagepass(ptab, kpages, vpages, pos, w1k, w1v, w2k, w2v, kg, nseq, npages):
    pp = PAGES_PER_STEP
    nsteps = npages // pp
    nchunk = npages * PAGE_SIZE // CMP_STRIDE
    feature_major = kpages.shape[1] == PAGE_COLS
    chunks_per_step = pp * PAGE_SIZE // SEL_CHUNK

    def page_spec(i):
        return pl.BlockSpec((1,) + kpages.shape[1:],
                            lambda s, j, pt: (pt[s * npages + j * pp + i], 0, 0))

    full = lambda shape: pl.BlockSpec(shape, lambda s, j, pt: (0,) * len(shape))
    grid_spec = pltpu.PrefetchScalarGridSpec(
        num_scalar_prefetch=1,
        grid=(nseq, nsteps),
        in_specs=([page_spec(i) for i in range(pp)] + [page_spec(i) for i in range(pp)]
                  + [full((8, 1024)), full((1024, 256)), full((1024, 256)),
                     full((128, 128)), full((128, 128)), full((1, 128))]),
        out_specs=[pl.BlockSpec((1, KV_HEADS, nchunk, LANES), lambda s, j, pt: (s, 0, 0, 0)),
                   pl.BlockSpec((1, KV_HEADS, chunks_per_step, 2 * HEAD_DIM, SEL_CHUNK),
                                lambda s, j, pt: (s, 0, j, 0, 0))],
        scratch_shapes=[pltpu.VMEM((KV_HEADS, nchunk, 1024), BF16),
                        pltpu.VMEM((KV_HEADS, nchunk, 1024), BF16),
                        pltpu.VMEM((nchunk + 8, 256), F32),
                        pltpu.VMEM((4, PAGE_SIZE, LANES), F32)],
    )
    return pl.pallas_call(
        functools.partial(_pagepass_kernel, nchunk=nchunk, feature_major=feature_major),
        grid_spec=grid_spec,
        out_shape=[jax.ShapeDtypeStruct((nseq, KV_HEADS, nchunk, LANES), BF16),
                   jax.ShapeDtypeStruct((nseq, KV_HEADS, npages * PAGE_SIZE // SEL_CHUNK, 2 * HEAD_DIM, SEL_CHUNK),
                                        BF16)],
        compiler_params=_cparams("parallel", "arbitrary"),
        name="pagepass",
    )(ptab, *([kpages] * pp), *([vpages] * pp), pos, w1k, w1v, w2k, w2v, kg)


WIN_TILES = (WINDOW + Q_BLOCK) // KEY_TILE


def _winpack_kernel(wk_ref, wv_ref, kwn_ref, vwn_ref, ksn_ref, vsn_ref, kvw_ref, tail_ref):
    t = kwn_ref.shape[0]
    past_tiles = WINDOW // KEY_TILE
    for c in range(past_tiles):
        ls = slice(c * KEY_TILE, (c + 1) * KEY_TILE)
        _store_kv_t(lambda g: kvw_ref.at[0, g, c], wk_ref[0, :, ls], wv_ref[0, :, ls])
    zeros = jnp.zeros((KEY_TILE - t, 256), F32)
    pad_t = lambda ref: jnp.concatenate([ref[...], zeros], axis=0).T
    _store_kv_t(lambda g: kvw_ref.at[0, g, past_tiles], pad_t(kwn_ref), pad_t(vwn_ref))
    _store_kv_t(lambda g: tail_ref.at[0, g], pad_t(ksn_ref), pad_t(vsn_ref))


def _winpack(wk_t, wv_t, kwin, u, new_k, new_v, nseq, t):
    tile = (1, KV_HEADS, 2 * HEAD_DIM, KEY_TILE)
    return pl.pallas_call(
        _winpack_kernel,
        grid=(nseq,),
        in_specs=[pl.BlockSpec((1, 256, WINDOW), lambda s: (s, 0, 0)),
                  pl.BlockSpec((1, 256, WINDOW), lambda s: (s, 0, 0)),
                  pl.BlockSpec((t, 256), lambda s: (s, 0)),
                  pl.BlockSpec((t, 256), lambda s: (s, 5)),
                  pl.BlockSpec((t, 256), lambda s: (s, 1)),
                  pl.BlockSpec((t, 256), lambda s: (s, 1))],
        out_specs=[pl.BlockSpec((1, KV_HEADS, WIN_TILES, 2 * HEAD_DIM, KEY_TILE), lambda s: (s, 0, 0, 0, 0)),
                   pl.BlockSpec(tile, lambda s: (s, 0, 0, 0))],
        out_shape=[jax.ShapeDtypeStruct((nseq, KV_HEADS, WIN_TILES, 2 * HEAD_DIM, KEY_TILE), BF16),
                   jax.ShapeDtypeStruct((nseq,) + tile[1:], BF16)],
        compiler_params=_cparams("parallel"),
        name="winpack",
    )(wk_t, wv_t, kwin, u, new_k, new_v)


def _attn_kernel(slopes_ref, q_ref, misc_ref, kvc_ref, kvs_ref, kvw_ref, *rest,
                 tq, q0, nchunk, nselp, lmain, wpos0, has_tail):
    if has_tail:
        tail_ref, o_ref, m_ref, l_ref, acc_ref = rest
    else:
        o_ref, m_ref, l_ref, acc_ref = rest
    g = pl.program_id(1)
    qs = pl.program_id(2) * tq
    m_rows = Q_PER_KV * tq
    q = q_ref[...].reshape(m_rows, LANES).astype(BF16)
    rowi = lax.broadcasted_iota(jnp.int32, (m_rows, 1), 0)
    tpos = q0 + qs + (rowi & (tq - 1))
    slope = jnp.zeros((m_rows, 1), F32)
    for r in range(Q_PER_KV):
        slope = jnp.where((rowi >= r * tq) & (rowi < (r + 1) * tq), slopes_ref[g * Q_PER_KV + r], slope)

    def masked_softmax(s, mask):
        s = jnp.where(mask, s, NEG_INF)
        p = jnp.exp(s - jnp.max(s, axis=-1, keepdims=True)) * jnp.where(mask, 1.0, 0.0)
        return p / jnp.maximum(jnp.sum(p, axis=-1, keepdims=True), TINY)

    kvc = kvc_ref[0, 0]
    s = lax.dot_general(q, kvc, NT_DIMS, preferred_element_type=F32)
    n_idx = lax.broadcasted_iota(jnp.int32, (m_rows, nchunk), 1)
    dist = tpos - (n_idx * CMP_STRIDE + (2 * CMP_STRIDE - 1))
    p = masked_softmax(s - slope * dist.astype(F32), dist >= 0)
    o_cmp = _dot(p.astype(BF16), kvc)

    psum = p[0:tq] + p[tq:2 * tq] + p[2 * tq:3 * tq] + p[3 * tq:4 * tq]
    pool = jnp.where((lax.broadcasted_iota(jnp.int32, (nchunk, nselp), 0) >> 2)
                     == lax.broadcasted_iota(jnp.int32, (nchunk, nselp), 1), 1.0, 0.0).astype(BF16)
    hi, mid, lo = _split3(psum)
    imp = _dot(hi, pool) + _dot(mid, pool) + _dot(lo, pool)
    blk = lax.broadcasted_iota(jnp.int32, (tq, nselp), 1)
    blk_f = blk.astype(F32)
    cur = (q0 + qs + lax.broadcasted_iota(jnp.int32, (tq, 1), 0)) >> 6
    forced = (blk == 0) | (blk == cur) | (blk == cur - 1)
    val = jnp.where(blk <= cur, imp + jnp.where(forced, FORCE_BONUS, 0.0), NEG_INF)

    def pick_top(_, carry):
        val, sel = carry
        best = jnp.max(val, axis=-1, keepdims=True)
        idx = jnp.min(jnp.where(val == best, blk_f, 1e9), axis=-1, keepdims=True)
        pick = blk_f == idx
        return jnp.where(pick, -3e38, val), jnp.where(pick, 1.0, sel)

    _, sel = lax.fori_loop(0, TOP_BLOCKS, pick_top, (val, jnp.zeros((tq, nselp), F32)))
    sel16 = sel.astype(BF16)

    m_ref[...] = jnp.full_like(m_ref, NEG_INF)
    l_ref[...] = jnp.zeros_like(l_ref)
    acc_ref[...] = jnp.zeros_like(acc_ref)

    def sel_chunk(kv_t, k0):
        n = kv_t.shape[1]
        s = _dot(q, kv_t)
        dist = tpos - (k0 + lax.broadcasted_iota(jnp.int32, (m_rows, n), 1))
        s = s - slope * dist.astype(F32)
        expand = jnp.where(lax.broadcasted_iota(jnp.int32, (nselp, n), 0) - (k0 >> 6)
                           == (lax.broadcasted_iota(jnp.int32, (nselp, n), 1) >> 6), 1.0, 0.0).astype(BF16)
        selk = _dot(sel16, expand)
        selk = jnp.concatenate([selk] * Q_PER_KV, axis=0)
        mask = (selk > 0.5) & (dist >= 0)
        s = jnp.where(mask, s, NEG_INF)
        m_old = m_ref[...]
        m_new = jnp.maximum(m_old, jnp.max(s, axis=-1, keepdims=True))
        alpha = jnp.exp(m_old - m_new)
        p = jnp.exp(s - m_new) * jnp.where(mask, 1.0, 0.0)
        l_ref[...] = alpha * l_ref[...] + jnp.sum(p, axis=-1, keepdims=True)
        acc_ref[...] = alpha * acc_ref[...] + lax.dot_general(p.astype(BF16), kv_t, NT_DIMS,
                                                              preferred_element_type=F32)
        m_ref[...] = m_new

    def body(kc, carry):
        sel_chunk(kvs_ref[0, 0, kc], kc * SEL_CHUNK)
        return carry

    n_vis = jnp.minimum((q0 + qs + tq + SEL_CHUNK - 1) // SEL_CHUNK, lmain // SEL_CHUNK)
    lax.fori_loop(0, n_vis, body, 0)
    if has_tail:
        sel_chunk(tail_ref[0, 0], lmain)
    o_sel = acc_ref[...] / jnp.maximum(l_ref[...], TINY)

    tile0 = jnp.maximum(q0 + qs - WINDOW - wpos0, 0) // KEY_TILE
    tiles = [kvw_ref[0, 0, tile0 + j] for j in range(WIN_TILES)]
    s = jnp.concatenate([_dot(q, kv_t) for kv_t in tiles], axis=1)
    dist = tpos - (wpos0 + tile0 * KEY_TILE
                   + lax.broadcasted_iota(jnp.int32, (m_rows, WIN_TILES * KEY_TILE), 1))
    p = masked_softmax(s - slope * dist.astype(F32), (dist >= 0) & (dist <= WINDOW)).astype(BF16)
    o_win = lax.dot_general(p[:, 0:KEY_TILE], tiles[0], NT_DIMS, preferred_element_type=F32)
    for j in range(1, WIN_TILES):
        o_win = o_win + lax.dot_general(p[:, j * KEY_TILE:(j + 1) * KEY_TILE], tiles[j], NT_DIMS,
                                        preferred_element_type=F32)

    sig = jax.nn.sigmoid(misc_ref[...])
    lane = lax.broadcasted_iota(jnp.int32, (tq, LANES), 1)
    comb = []
    for r in range(Q_PER_KV):
        rs = slice(r * tq, (r + 1) * tq)
        gate = [jnp.sum(jnp.where(lane == GATE_LANE0 + g * 12 + r * 3 + c, sig, 0.0), axis=-1, keepdims=True)
                for c in range(3)]
        comb.append(gate[0] * o_cmp[rs] + gate[1] * o_sel[rs] + gate[2] * o_win[rs])
    lo = lane < HEAD_DIM
    o_ref[:, 0:LANES] = jnp.where(lo, _swap_halves(comb[0]), comb[1])
    o_ref[:, LANES:2 * LANES] = jnp.where(lo, _swap_halves(comb[2]), comb[3])


def _attn(slopes, qhm, u, kvcmp, kvsel, kvwin, tail, nseq, t, q0, wpos0, win_head_major):
    tq = min(t, Q_BLOCK)
    nqb = t // tq
    nchunk = kvcmp.shape[2]
    sel_chunks = kvsel.shape[2]
    lmain = sel_chunks * SEL_CHUNK
    n_sel = (q0 + t + SEL_BLOCK - 1) // SEL_BLOCK
    nselp = -(-n_sel // LANES) * LANES
    win_tiles = kvwin.shape[2]
    m_rows = Q_PER_KV * tq
    if win_head_major:
        win_map = lambda s, g, i: (g, s, 0, 0, 0)
    else:
        win_map = lambda s, g, i: (s, g, 0, 0, 0)
    in_specs = [pl.BlockSpec(memory_space=pltpu.SMEM),
                pl.BlockSpec((Q_PER_KV, tq, LANES), lambda s, g, i: (g, s * nqb + i, 0)),
                pl.BlockSpec((tq, LANES), lambda s, g, i: (s * nqb + i, U_MISC // LANES)),
                pl.BlockSpec((1, 1, nchunk, LANES), lambda s, g, i: (s, g, 0, 0)),
                pl.BlockSpec((1, 1, sel_chunks, 2 * HEAD_DIM, SEL_CHUNK), lambda s, g, i: (s, g, 0, 0, 0)),
                pl.BlockSpec((1, 1, win_tiles, 2 * HEAD_DIM, KEY_TILE), win_map)]
    args = [slopes, qhm, u, kvcmp, kvsel, kvwin]
    if tail is not None:
        in_specs.append(pl.BlockSpec((1, 1, 2 * HEAD_DIM, KEY_TILE), lambda s, g, i: (s, g, 0, 0)))
        args.append(tail)
    return pl.pallas_call(
        functools.partial(_attn_kernel, tq=tq, q0=q0, nchunk=nchunk, nselp=nselp, lmain=lmain,
                          wpos0=wpos0, has_tail=tail is not None),
        grid=(nseq, KV_HEADS, nqb),
        in_specs=in_specs,
        out_specs=pl.BlockSpec((tq, 2 * LANES), lambda s, g, i: (s * nqb + i, g)),
        out_shape=jax.ShapeDtypeStruct((nseq * t, D_ATTN), F32),
        scratch_shapes=[pltpu.VMEM((m_rows, 1), F32), pltpu.VMEM((m_rows, 1), F32),
                        pltpu.VMEM((m_rows, LANES), F32)],
        compiler_params=_cparams("parallel", "parallel", "arbitrary"),
        name="nsa_attn",
    )(*args)


def _layer_params(l, norm_mix, w_in, conv_a_w, conv_b_w, conv_b_bias, dt_bias, a_log, d_skip,
                  q_norm, k_norm, cmp_pos, cmp_w1, cmp_w2, norm_out, w_out, norm_ffn, w_gate, w_up, w_down):
    w = w_in[l]
    w_perm = jnp.concatenate(
        [w[:, 4104:5640], w[:, 0:1536], w[:, 2048:3072], w[:, 3080:4104], w[:, 1536:2048],
         w[:, 3072:3080], w[:, 5640:5688], jnp.zeros((D_MODEL, U_COLS - 5688), F32)], axis=1).astype(BF16)
    pad8 = lambda v: jnp.pad(v, (0, LANES - v.shape[0])).reshape(1, LANES)
    head_id = jnp.arange(256) // HEAD_DIM
    pos = cmp_pos[l].reshape(4, 1024)
    w1 = cmp_w1[l]
    w2 = cmp_w2[l]
    zeros_w2 = jnp.zeros((CMP_HIDDEN, HEAD_DIM), F32)
    return dict(
        g_mix=norm_mix[l].reshape(1, D_MODEL), w_in=w_perm,
        conv_a_w=conv_a_w[l], conv_b_w=conv_b_w[l], conv_b_bias=conv_b_bias[l].reshape(1, SSM_CONV_DIM),
        dt_bias=pad8(dt_bias[l]), a_log=pad8(a_log[l]),
        d_skip=jnp.repeat(d_skip[l], HEAD_DIM).reshape(1, D_SSM),
        mavg=jnp.where(head_id[:, None] == head_id[None, :], 1.0 / HEAD_DIM, 0.0).astype(BF16),
        q_gain=jnp.tile(q_norm[l], 4).reshape(1, 256),
        k_gain1=jnp.tile(k_norm[l, 1], 4).reshape(1, 256),
        k_gain2=jnp.tile(k_norm[l, 2], 4).reshape(1, 256),
        k_gain0=pad8(k_norm[l, 0]),
        pos=jnp.pad(pos, ((0, 4), (0, 0))),
        w1k=jnp.concatenate([w1[0, 0:1024], w1[0, 1024:2048]], axis=1).astype(BF16),
        w1v=jnp.concatenate([w1[1, 0:1024], w1[1, 1024:2048]], axis=1).astype(BF16),
        w2k=jnp.concatenate([w2[0], zeros_w2], axis=1).astype(BF16),
        w2v=jnp.concatenate([zeros_w2, w2[1]], axis=1).astype(BF16),
        g_out=norm_out[l].reshape(1, D_MODEL), w_out=w_out[l].astype(BF16),
        g_ffn=norm_ffn[l].reshape(1, D_MODEL),
        w_gate=w_gate[l].astype(BF16), w_up=w_up[l].astype(BF16), w_down=w_down[l].astype(BF16),
    )


def _layer(x, nseq, t, q0, p, slopes, conv_a_prefix, conv_b_prefix, ssm_h0, past):
    u = _in_proj(x, p["g_mix"], p["w_in"])
    ya, new_conv_a = _mixer_a(u, conv_a_prefix, p["conv_a_w"], nseq, t)
    yb, new_conv_b, new_ssm = _ssd(u, conv_b_prefix, ssm_h0, p["conv_b_w"], p["conv_b_bias"],
                                   p["dt_bias"], p["a_log"], p["d_skip"], nseq, t)
    prep_out = _prep(u, p["mavg"], p["q_gain"], p["k_gain1"], p["k_gain2"],
                     BF16 if past is None else F32, past is None)
    qhm, new_k, new_v, kwin = prep_out[:4]
    vwin = u[:, U_KV + 1280:U_KV + 1536]
    cmp_args = (p["pos"], p["w1k"], p["w1v"], p["w2k"], p["w2v"], p["k_gain0"])
    if past is None:
        npages = t // PAGE_SIZE
        ptab = jnp.arange(nseq * npages, dtype=jnp.int32)
        kvcmp, kvsel = _pagepass(ptab, new_k.reshape(-1, PAGE_SIZE, PAGE_COLS),
                                 new_v.reshape(-1, PAGE_SIZE, PAGE_COLS), *cmp_args, nseq, npages)
        kvwin_arr = prep_out[4].reshape(KV_HEADS, nseq, t // KEY_TILE, 2 * HEAD_DIM, KEY_TILE)
        yc = _attn(slopes, qhm, u, kvcmp, kvsel, kvwin_arr, None, nseq, t, 0, 0, True)
        new_win_k = kwin.reshape(nseq, t, 256)[:, t - WINDOW:]
        new_win_v = vwin.reshape(nseq, t, 256)[:, t - WINDOW:]
    else:
        ptab, k_pages, v_pages, npages, win_k_t, win_v_t, win_k, win_v = past
        kvcmp, kvsel = _pagepass(ptab, k_pages, v_pages, *cmp_args, nseq, npages)
        kvwin_arr, tail = _winpack(win_k_t, win_v_t, kwin, u, new_k, new_v, nseq, t)
        yc = _attn(slopes, qhm, u, kvcmp, kvsel, kvwin_arr, tail, nseq, t, q0, q0 - WINDOW, False)
        new_win_k = jnp.concatenate([win_k[:, t:], kwin.reshape(nseq, t, 256)], axis=1)
        new_win_v = jnp.concatenate([win_v[:, t:], vwin.reshape(nseq, t, 256)], axis=1)
    x = _out_proj(x, ya, yb, yc, p["g_out"], p["w_out"])
    x = _ffn(x, p["g_ffn"], p["w_gate"], p["w_up"], p["w_down"])
    hd = (KV_HEADS, HEAD_DIM)
    state = (new_k.reshape(nseq, t, 2, *hd), new_v.reshape(nseq, t, 2, *hd),
             new_win_k.reshape(nseq, WINDOW, *hd), new_win_v.reshape(nseq, WINDOW, *hd),
             new_conv_a, new_conv_b, new_ssm)
    return x, state


def kernel(x_prompt, x_sample, cache_k, cache_v, cache_win_k, cache_win_v, state_conv_a, state_conv_b,
           state_ssm, page_table, norm_mix, w_in, conv_a_w, conv_b_w, conv_b_bias, dt_bias, a_log, d_skip,
           q_norm, k_norm, cmp_pos, cmp_w1, cmp_w2, norm_out, w_out, norm_ffn, w_gate, w_up, w_down):
    bsz, t_prompt, _ = x_prompt.shape
    dec_b, t_dec, _ = x_sample.shape
    depth, n_pool = cache_k.shape[0], cache_k.shape[1]
    npages = page_table.shape[1]
    past_len = npages * PAGE_SIZE
    assert cache_win_k.shape[2] == WINDOW and t_prompt >= WINDOW + Q_BLOCK and t_prompt % Q_BLOCK == 0
    assert t_dec == 8 and npages % PAGES_PER_STEP == 0 and (t_prompt // PAGE_SIZE) % PAGES_PER_STEP == 0

    slopes = jnp.exp2(-8.0 * jnp.arange(1, ATTN_HEADS + 1, dtype=F32) / ATTN_HEADS)
    to_pages_t = lambda c: jnp.transpose(c, (0, 1, 3, 4, 5, 2)).reshape(depth * n_pool, PAGE_COLS, PAGE_SIZE)
    to_win_t = lambda w: jnp.transpose(w, (0, 2, 3, 1)).reshape(dec_b, 256, WINDOW)
    k_pages = to_pages_t(cache_k)
    v_pages = to_pages_t(cache_v)
    hp = x_prompt.reshape(bsz * t_prompt, D_MODEL)
    hs = x_sample.reshape(dec_b * t_dec, D_MODEL)
    zeros = lambda *shape: jnp.zeros(shape, F32)
    prompt_states, sample_states = [], []
    for l in range(depth):
        p = _layer_params(l, norm_mix, w_in, conv_a_w, conv_b_w, conv_b_bias, dt_bias, a_log, d_skip,
                          q_norm, k_norm, cmp_pos, cmp_w1, cmp_w2, norm_out, w_out, norm_ffn,
                          w_gate, w_up, w_down)
        hp, st_p = _layer(hp, bsz, t_prompt, 0, p, slopes,
                          zeros(bsz, CONV_A_WIDTH - 1, D_CONV), zeros(bsz, SSM_CONV_WIDTH - 1, SSM_CONV_DIM),
                          zeros(bsz, SSM_HEADS, HEAD_DIM, SSM_STATE), None)
        ptab = (page_table + l * n_pool).reshape(-1).astype(jnp.int32)
        past = (ptab, k_pages, v_pages, npages, to_win_t(cache_win_k[l]), to_win_t(cache_win_v[l]),
                cache_win_k[l].reshape(dec_b, WINDOW, 256), cache_win_v[l].reshape(dec_b, WINDOW, 256))
        hs, st_s = _layer(hs, dec_b, t_dec, past_len, p, slopes,
                          state_conv_a[l], state_conv_b[l], state_ssm[l], past)
        prompt_states.append(st_p)
        sample_states.append(st_s)

    stack = lambda states, j: jnp.stack([s[j] for s in states], axis=0)
    return ((hp.reshape(bsz, t_prompt, D_MODEL), hs.reshape(dec_b, t_dec, D_MODEL))
            + tuple(stack(prompt_states, j) for j in range(7))
            + tuple(stack(sample_states, j) for j in range(7)))
```

```python
import functools

import jax
import jax.numpy as jnp
from jax import lax
from jax.experimental import pallas as pl
from jax.experimental.pallas import tpu as pltpu

F32 = jnp.float32
BF16 = jnp.bfloat16

D_MODEL = 2048
HEAD_DIM = 64
D_CONV = 512
D_SSM = 512
D_ATTN = 1024
CONV_A_WIDTH = 3
SSM_HEADS = 8
SSM_STATE = 128
SSM_CONV_WIDTH = 4
SSM_CHUNK = 128
SSM_CONV_DIM = 1024
ATTN_HEADS = 16
KV_HEADS = 4
Q_PER_KV = 4
KV_COLS = 1536
CMP_STRIDE = 16
CMP_HIDDEN = 128
SEL_BLOCK = 64
TOP_BLOCKS = 16
WINDOW = 512
Q_BLOCK = 128
PAGE_SIZE = 128
PAGE_COLS = 512
D_FF = 5632
RMS_EPS = 1e-6
NEG_INF = -1e30
TINY = 1e-30
FORCE_BONUS = 1e3
ATTN_SCALE = HEAD_DIM ** -0.5

U_KV = 0
U_A = 1536
U_XBC = 3072
U_Q = 4096
U_Z = 5120
U_MISC = 5632
U_COLS = 5760
GATE_LANE0 = 8

LANES = 128
KEY_TILE = 128
SEL_CHUNK = 256
VMEM_LIMIT_BYTES = 56 * 2 ** 20

NT_DIMS = (((1,), (1,)), ((), ()))


def _cparams(*sem):
    return pltpu.CompilerParams(dimension_semantics=sem, vmem_limit_bytes=VMEM_LIMIT_BYTES)


def _tile(n, pref):
    t = min(n, pref)
    while n % t:
        t //= 2
    return t


def _dot(a, b):
    return jnp.dot(a, b, preferred_element_type=F32)


def _split3(x):
    hi = x.astype(BF16)
    r1 = x - hi.astype(F32)
    mid = r1.astype(BF16)
    lo = (r1 - mid.astype(F32)).astype(BF16)
    return hi, mid, lo


def _lo_half(rows):
    return lax.broadcasted_iota(jnp.int32, (rows, LANES), 1) < HEAD_DIM


def _swap_halves(x):
    return pltpu.roll(x, HEAD_DIM, 1)


def _store_kv_t(dst, k_t, v_t):
    for g in range(KV_HEADS):
        view = dst(g)
        view[0:HEAD_DIM, :] = k_t[g * HEAD_DIM:(g + 1) * HEAD_DIM].astype(BF16)
        view[HEAD_DIM:2 * HEAD_DIM, :] = v_t[g * HEAD_DIM:(g + 1) * HEAD_DIM].astype(BF16)


def _in_proj_kernel(x_ref, g_ref, w_ref, o_ref, xn_ref):
    @pl.when(pl.program_id(1) == 0)
    def _():
        x = x_ref[...]
        ms = jnp.mean(x * x, axis=-1, keepdims=True)
        xn_ref[...] = (x * lax.rsqrt(ms + RMS_EPS) * g_ref[...]).astype(BF16)

    o_ref[...] = _dot(xn_ref[...], w_ref[...])


def _in_proj(x, gain, w):
    n = x.shape[0]
    tm = _tile(n, 1024)
    tn = 1152
    return pl.pallas_call(
        _in_proj_kernel,
        grid=(n // tm, U_COLS // tn),
        in_specs=[pl.BlockSpec((tm, D_MODEL), lambda i, j: (i, 0)),
                  pl.BlockSpec((1, D_MODEL), lambda i, j: (0, 0)),
                  pl.BlockSpec((D_MODEL, tn), lambda i, j: (0, j))],
        out_specs=pl.BlockSpec((tm, tn), lambda i, j: (i, j)),
        out_shape=jax.ShapeDtypeStruct((n, U_COLS), F32),
        scratch_shapes=[pltpu.VMEM((tm, D_MODEL), BF16)],
        compiler_params=_cparams("parallel", "arbitrary"),
        name="in_proj",
    )(x, gain, w)


def _out_proj_kernel(x_ref, ya_ref, yb_ref, yc_ref, g_ref, w_ref, o_ref, mg_ref):
    @pl.when(pl.program_id(1) == 0)
    def _():
        def nrm(y, g):
            ms = jnp.mean(y * y, axis=-1, keepdims=True)
            return (y * lax.rsqrt(ms + RMS_EPS) * g).astype(BF16)

        mg_ref[:, 0:512] = nrm(ya_ref[...], g_ref[:, 0:512])
        mg_ref[:, 512:1024] = nrm(yb_ref[...], g_ref[:, 512:1024])
        mg_ref[:, 1024:2048] = nrm(yc_ref[...], g_ref[:, 1024:2048])

    o_ref[...] = x_ref[...] + _dot(mg_ref[...], w_ref[...])


def _out_proj(x, ya, yb, yc, gain, w):
    n = x.shape[0]
    tm = _tile(n, 512)
    tn = 1024
    return pl.pallas_call(
        _out_proj_kernel,
        grid=(n // tm, D_MODEL // tn),
        in_specs=[pl.BlockSpec((tm, tn), lambda i, j: (i, j)),
                  pl.BlockSpec((tm, D_CONV), lambda i, j: (i, 0)),
                  pl.BlockSpec((tm, D_SSM), lambda i, j: (i, 0)),
                  pl.BlockSpec((tm, D_ATTN), lambda i, j: (i, 0)),
                  pl.BlockSpec((1, D_MODEL), lambda i, j: (0, 0)),
                  pl.BlockSpec((D_MODEL, tn), lambda i, j: (0, j))],
        out_specs=pl.BlockSpec((tm, tn), lambda i, j: (i, j)),
        out_shape=jax.ShapeDtypeStruct((n, D_MODEL), F32),
        scratch_shapes=[pltpu.VMEM((tm, D_MODEL), BF16)],
        compiler_params=_cparams("parallel", "arbitrary"),
        name="out_proj",
    )(x, ya, yb, yc, gain, w)


def _ffn_kernel(x_ref, g_ref, wg_ref, wu_ref, wd_ref, o_ref, h_ref, acc_ref):
    f = pl.program_id(1)

    @pl.when(f == 0)
    def _():
        x = x_ref[...]
        ms = jnp.mean(x * x, axis=-1, keepdims=True)
        h_ref[...] = (x * lax.rsqrt(ms + RMS_EPS) * g_ref[...]).astype(BF16)
        acc_ref[...] = jnp.zeros_like(acc_ref)

    h = h_ref[...]
    a = _dot(h, wg_ref[...])
    b = _dot(h, wu_ref[...])
    acc_ref[...] += _dot((a * jax.nn.sigmoid(a) * b).astype(BF16), wd_ref[...])

    @pl.when(f == pl.num_programs(1) - 1)
    def _():
        o_ref[...] = x_ref[...] + acc_ref[...]


def _ffn(x, gain, wg, wu, wd):
    n = x.shape[0]
    tm = _tile(n, 1024)
    tf = 256
    return pl.pallas_call(
        _ffn_kernel,
        grid=(n // tm, D_FF // tf),
        in_specs=[pl.BlockSpec((tm, D_MODEL), lambda i, f: (i, 0)),
                  pl.BlockSpec((1, D_MODEL), lambda i, f: (0, 0)),
                  pl.BlockSpec((D_MODEL, tf), lambda i, f: (0, f)),
                  pl.BlockSpec((D_MODEL, tf), lambda i, f: (0, f)),
                  pl.BlockSpec((tf, D_MODEL), lambda i, f: (f, 0))],
        out_specs=pl.BlockSpec((tm, D_MODEL), lambda i, f: (i, 0)),
        out_shape=jax.ShapeDtypeStruct((n, D_MODEL), F32),
        scratch_shapes=[pltpu.VMEM((tm, D_MODEL), BF16), pltpu.VMEM((tm, D_MODEL), F32)],
        compiler_params=_cparams("parallel", "arbitrary"),
        name="ffn",
    )(x, gain, wg, wu, wd)


def _mixer_a_kernel(u_ref, pre_ref, w_ref, y_ref, newpre_ref, ext_ref, *, rows):
    @pl.when(pl.program_id(1) == 0)
    def _():
        ext_ref[6:8, :] = pre_ref[0]

    u = u_ref[...]
    v = u[:, 512:1024] * u[:, 1024:1536]
    ext_ref[8:8 + rows, :] = v
    w = w_ref[...]
    y = ext_ref[6:6 + rows, :] * w[0:1] + ext_ref[7:7 + rows, :] * w[1:2] + v * w[2:3]
    y_ref[...] = u[:, 0:512] * y
    last = ext_ref[8 + rows - 2:8 + rows, :]
    ext_ref[6:8, :] = last
    newpre_ref[0] = last


def _mixer_a(u, prefix, w, nseq, t):
    rows = _tile(t, 512)
    nblk = t // rows
    return pl.pallas_call(
        functools.partial(_mixer_a_kernel, rows=rows),
        grid=(nseq, nblk),
        in_specs=[pl.BlockSpec((rows, 3 * D_CONV), lambda s, j: (s * nblk + j, U_A // (3 * D_CONV))),
                  pl.BlockSpec((1, CONV_A_WIDTH - 1, D_CONV), lambda s, j: (s, 0, 0)),
                  pl.BlockSpec((CONV_A_WIDTH, D_CONV), lambda s, j: (0, 0))],
        out_specs=[pl.BlockSpec((rows, D_CONV), lambda s, j: (s * nblk + j, 0)),
                   pl.BlockSpec((1, CONV_A_WIDTH - 1, D_CONV), lambda s, j: (s, 0, 0))],
        out_shape=[jax.ShapeDtypeStruct((nseq * t, D_CONV), F32),
                   jax.ShapeDtypeStruct((nseq, CONV_A_WIDTH - 1, D_CONV), F32)],
        scratch_shapes=[pltpu.VMEM((8 + rows, D_CONV), F32)],
        compiler_params=_cparams("parallel", "arbitrary"),
        name="mixer_a",
    )(u, prefix, w)


def _ssd_kernel(z_ref, xbc_ref, misc_ref, pre_ref, h0_ref, cw_ref, cb_ref, dtb_ref, alog_ref, dsk_ref,
                y_ref, newpre_ref, hout_ref, ext_ref, st_ref, pad_ref, *, lin):
    L = SSM_CHUNK
    c = pl.program_id(1)

    @pl.when(c == 0)
    def _():
        if lin < L:
            ext_ref[...] = jnp.zeros_like(ext_ref)
            pad_ref[...] = jnp.zeros_like(pad_ref)
        ext_ref[5:8, :] = pre_ref[0]
        st_ref[...] = h0_ref[0].reshape(SSM_HEADS * HEAD_DIM, SSM_STATE)

    ext_ref[8:8 + lin, :] = xbc_ref[...]
    cw = cw_ref[...]
    acc = (ext_ref[5:5 + L, :] * cw[0:1] + ext_ref[6:6 + L, :] * cw[1:2]
           + ext_ref[7:7 + L, :] * cw[2:3] + ext_ref[8:8 + L, :] * cw[3:4])
    acc = acc + cb_ref[...]
    xbc = acc * jax.nn.sigmoid(acc)
    newp = ext_ref[8 + lin - 3:8 + lin, :]
    ext_ref[5:8, :] = newp
    newpre_ref[0] = newp

    if lin < L:
        pad_ref[0:lin, :] = misc_ref[...]
        misc = pad_ref[...]
    else:
        misc = misc_ref[...]
    rows = lax.broadcasted_iota(jnp.int32, (L, L), 0)
    cols = lax.broadcasted_iota(jnp.int32, (L, L), 1)
    x = misc + dtb_ref[...]
    dt = jnp.maximum(x, 0.0) + jnp.log1p(jnp.exp(-jnp.abs(x)))
    if lin < L:
        dt = jnp.where(rows < lin, dt, 0.0)
    la = dt * (-jnp.exp(alog_ref[...]))

    causal = rows >= cols
    tril = jnp.where(causal, 1.0, 0.0).astype(BF16)
    hi, mid, lo = _split3(la)
    cum = _dot(tril, hi) + _dot(tril, mid) + _dot(tril, lo)
    cum_t = cum.T
    last = cum[L - 1:L, :]
    lo_half = cols < HEAD_DIM
    top_rows = rows < HEAD_DIM

    z = z_ref[...]
    dsk = dsk_ref[...]
    for g in range(2):
        bg = xbc[:, 512 + g * 128:512 + (g + 1) * 128].astype(BF16)
        cg = xbc[:, 768 + g * 128:768 + (g + 1) * 128].astype(BF16)
        gram = lax.dot_general(cg, bg, NT_DIMS, preferred_element_type=F32)
        for i in (2 * g, 2 * g + 1):
            a, b = 2 * i, 2 * i + 1
            sl = slice(i * LANES, (i + 1) * LANES)
            xs_p = xbc[:, sl]
            col_a = cum[:, a:a + 1]
            col_b = cum[:, b:b + 1]
            xdt = xs_p * jnp.where(lo_half, dt[:, a:a + 1], dt[:, b:b + 1])
            dec_a = jnp.exp(jnp.where(causal, col_a - cum_t[a:a + 1, :], NEG_INF))
            dec_b = jnp.exp(jnp.where(causal, col_b - cum_t[b:b + 1, :], NEG_INF))
            xa = jnp.where(lo_half, xdt, 0.0).astype(BF16)
            xb = jnp.where(lo_half, 0.0, xdt).astype(BF16)
            y_intra = _dot((gram * dec_a).astype(BF16), xa) + _dot((gram * dec_b).astype(BF16), xb)
            st = st_ref[sl, :]
            y_inter = (lax.dot_general(cg, st.astype(BF16), NT_DIMS, preferred_element_type=F32)
                       * jnp.where(lo_half, jnp.exp(col_a), jnp.exp(col_b)))
            to_end = jnp.where(lo_half, jnp.exp(last[:, a:a + 1] - col_a), jnp.exp(last[:, b:b + 1] - col_b))
            xw_t = (xdt * to_end).T.astype(BF16)
            decay = jnp.where(top_rows, jnp.exp(last[:, a:a + 1]), jnp.exp(last[:, b:b + 1]))
            st_ref[sl, :] = st * decay + _dot(xw_t, bg)
            y = y_intra + y_inter + dsk[:, sl] * xs_p
            zp = z[:, sl]
            y_ref[:, sl] = y[0:lin] * (zp * jax.nn.sigmoid(zp))

    @pl.when(c == pl.num_programs(1) - 1)
    def _():
        hout_ref[0] = st_ref[...].reshape(SSM_HEADS, HEAD_DIM, SSM_STATE)


def _ssd(u, prefix, h0, cw, cb, dtb, alog, dsk, nseq, t):
    lin = min(t, SSM_CHUNK)
    nc = t // lin
    full = lambda shape: pl.BlockSpec(shape, lambda s, c: (0,) * len(shape))
    return pl.pallas_call(
        functools.partial(_ssd_kernel, lin=lin),
        grid=(nseq, nc),
        in_specs=[pl.BlockSpec((lin, D_SSM), lambda s, c: (s * nc + c, U_Z // D_SSM)),
                  pl.BlockSpec((lin, SSM_CONV_DIM), lambda s, c: (s * nc + c, U_XBC // SSM_CONV_DIM)),
                  pl.BlockSpec((lin, LANES), lambda s, c: (s * nc + c, U_MISC // LANES)),
                  pl.BlockSpec((1, SSM_CONV_WIDTH - 1, SSM_CONV_DIM), lambda s, c: (s, 0, 0)),
                  pl.BlockSpec((1, SSM_HEADS, HEAD_DIM, SSM_STATE), lambda s, c: (s, 0, 0, 0)),
                  full((SSM_CONV_WIDTH, SSM_CONV_DIM)), full((1, SSM_CONV_DIM)),
                  full((1, LANES)), full((1, LANES)), full((1, D_SSM))],
        out_specs=[pl.BlockSpec((lin, D_SSM), lambda s, c: (s * nc + c, 0)),
                   pl.BlockSpec((1, SSM_CONV_WIDTH - 1, SSM_CONV_DIM), lambda s, c: (s, 0, 0)),
                   pl.BlockSpec((1, SSM_HEADS, HEAD_DIM, SSM_STATE), lambda s, c: (s, 0, 0, 0))],
        out_shape=[jax.ShapeDtypeStruct((nseq * t, D_SSM), F32),
                   jax.ShapeDtypeStruct((nseq, SSM_CONV_WIDTH - 1, SSM_CONV_DIM), F32),
                   jax.ShapeDtypeStruct((nseq, SSM_HEADS, HEAD_DIM, SSM_STATE), F32)],
        scratch_shapes=[pltpu.VMEM((8 + SSM_CHUNK, SSM_CONV_DIM), F32),
                        pltpu.VMEM((SSM_HEADS * HEAD_DIM, SSM_STATE), F32),
                        pltpu.VMEM((SSM_CHUNK, LANES), F32)],
        compiler_params=_cparams("parallel", "arbitrary"),
        name="ssd",
    )(u, u, u, prefix, h0, cw, cb, dtb, alog, dsk)


def _headnorm(x, mavg, gain):
    x2 = x * x
    hi = x2.astype(BF16)
    lo = (x2 - hi.astype(F32)).astype(BF16)
    ms = _dot(hi, mavg) + _dot(lo, mavg)
    return x * lax.rsqrt(ms + RMS_EPS) * gain


def _prep_kernel(q_ref, kv_ref, mavg_ref, qg_ref, kg1_ref, kg2_ref,
                 qhm_ref, nk_ref, nv_ref, kwin_ref, *maybe_kvwin_ref):
    mavg = mavg_ref[...]
    lo = _lo_half(q_ref.shape[0])
    for c in range(4):
        qn = _headnorm(q_ref[:, c * 256:(c + 1) * 256], mavg, qg_ref[...]) * ATTN_SCALE
        for cc in range(2):
            col = qn[:, cc * LANES:(cc + 1) * LANES]
            h = c * 4 + cc * 2
            qhm_ref[h] = jnp.where(lo, col, 0.0).astype(qhm_ref.dtype)
            qhm_ref[h + 1] = jnp.where(lo, _swap_halves(col), 0.0).astype(qhm_ref.dtype)
    ksel = _headnorm(kv_ref[:, 512:768], mavg, kg1_ref[...])
    kwin = _headnorm(kv_ref[:, 1024:1280], mavg, kg2_ref[...])
    nk_ref[:, 0:256] = kv_ref[:, 0:256]
    nk_ref[:, 256:512] = ksel
    nv_ref[:, 0:256] = kv_ref[:, 256:512]
    nv_ref[:, 256:512] = kv_ref[:, 768:1024]
    kwin_ref[...] = kwin
    if maybe_kvwin_ref:
        kvwin_ref, = maybe_kvwin_ref
        for c in range(q_ref.shape[0] // KEY_TILE):
            rs = slice(c * KEY_TILE, (c + 1) * KEY_TILE)
            _store_kv_t(lambda g: kvwin_ref.at[g, c], kwin[rs].T, kv_ref[rs, 1280:1536].T)


def _prep(u, mavg, qg, kg1, kg2, q_dtype, emit_window_tiles):
    n = u.shape[0]
    tm = _tile(n, 256)
    full = lambda shape: pl.BlockSpec(shape, lambda i: (0,) * len(shape))
    out_specs = [pl.BlockSpec((ATTN_HEADS, tm, LANES), lambda i: (0, i, 0)),
                 pl.BlockSpec((tm, PAGE_COLS), lambda i: (i, 0)),
                 pl.BlockSpec((tm, PAGE_COLS), lambda i: (i, 0)),
                 pl.BlockSpec((tm, 256), lambda i: (i, 0))]
    out_shape = [jax.ShapeDtypeStruct((ATTN_HEADS, n, LANES), q_dtype),
                 jax.ShapeDtypeStruct((n, PAGE_COLS), F32),
                 jax.ShapeDtypeStruct((n, PAGE_COLS), F32),
                 jax.ShapeDtypeStruct((n, 256), F32)]
    if emit_window_tiles:
        out_specs.append(pl.BlockSpec((KV_HEADS, tm // KEY_TILE, 2 * HEAD_DIM, KEY_TILE), lambda i: (0, i, 0, 0)))
        out_shape.append(jax.ShapeDtypeStruct((KV_HEADS, n // KEY_TILE, 2 * HEAD_DIM, KEY_TILE), BF16))
    return pl.pallas_call(
        _prep_kernel,
        grid=(n // tm,),
        in_specs=[pl.BlockSpec((tm, D_ATTN), lambda i: (i, U_Q // D_ATTN)),
                  pl.BlockSpec((tm, KV_COLS), lambda i: (i, 0)),
                  full((256, 256)), full((1, 256)), full((1, 256)), full((1, 256))],
        out_specs=out_specs,
        out_shape=out_shape,
        compiler_params=_cparams("parallel"),
        name="nsa_prep",
    )(u, u, mavg, qg, kg1, kg2)


PAGES_PER_STEP = 8


def _gelu_tanh(x):
    return 0.5 * x * (1.0 + jnp.tanh(0.7978845608028654 * (x + 0.044715 * (x * x * x))))


def _pagepass_kernel(pt_ref, *refs, nchunk, feature_major):
    pp = PAGES_PER_STEP

    def cmp_rows(page):
        return page[0, 0:256, :].T if feature_major else page[0, :, 0:256]

    def sel_t(page):
        return page[0, 256:512, :] if feature_major else page[0, :, 256:512].T

    kpages = refs[0:pp]
    vpages = refs[pp:2 * pp]
    pos_ref, w1k_ref, w1v_ref, w2k_ref, w2v_ref, kg_ref = refs[2 * pp:2 * pp + 6]
    kvcmp_ref, kvsel_ref = refs[2 * pp + 6:2 * pp + 8]
    xk_ref, xv_ref, hs_ref, stage_ref = refs[2 * pp + 8:]
    j = pl.program_id(1)
    lo16 = _lo_half(16)

    for i2 in range(pp // 2):
        r0 = pl.multiple_of((j * pp + 2 * i2) * 8, 16)
        for pages, xs_ref in ((kpages, xk_ref), (vpages, xv_ref)):
            for half in range(2):
                rows = cmp_rows(pages[2 * i2 + half])
                for c in range(2):
                    stage_ref[half * 2 + c] = rows[:, c * LANES:(c + 1) * LANES]
            for a in range(8):
                for c in range(2):
                    ec = jnp.concatenate([stage_ref[c, pl.ds(2 * a, 8, stride=16), :],
                                          stage_ref[2 + c, pl.ds(2 * a, 8, stride=16), :]], axis=0)
                    oc = jnp.concatenate([stage_ref[c, pl.ds(2 * a + 1, 8, stride=16), :],
                                          stage_ref[2 + c, pl.ds(2 * a + 1, 8, stride=16), :]], axis=0)
                    xs_ref[2 * c, pl.ds(r0, 16), a * LANES:(a + 1) * LANES] = (
                        jnp.where(lo16, ec, _swap_halves(oc)).astype(BF16))
                    xs_ref[2 * c + 1, pl.ds(r0, 16), a * LANES:(a + 1) * LANES] = (
                        jnp.where(lo16, _swap_halves(ec), oc).astype(BF16))

    per_chunk = SEL_CHUNK // PAGE_SIZE
    for i in range(pp):
        ls = slice((i % per_chunk) * PAGE_SIZE, (i % per_chunk + 1) * PAGE_SIZE)
        _store_kv_t(lambda g: kvsel_ref.at[0, g, i // per_chunk, :, ls], sel_t(kpages[i]), sel_t(vpages[i]))

    @pl.when(j == pl.num_programs(1) - 1)
    def _():
        hs_ref[nchunk:nchunk + 8, :] = jnp.zeros((8, 256), F32)
        pos = pos_ref[...]
        phi = pos.astype(BF16)
        plo = (pos - phi.astype(F32)).astype(BF16)
        bias_k = _dot(phi, w1k_ref[...]) + _dot(plo, w1k_ref[...])
        bias_v = _dot(phi, w1v_ref[...]) + _dot(plo, w1v_ref[...])
        bias_k = bias_k[0:1, 0:128] + bias_k[1:2, 128:256]
        bias_v = bias_v[2:3, 0:128] + bias_v[3:4, 128:256]

        def summarise(xs, w1_ref, bias, w2_ref):
            hs_ref[0:nchunk, :] = _dot(xs, w1_ref[...])
            pre = hs_ref[0:nchunk, 0:128] + hs_ref[1:nchunk + 1, 128:256] + bias
            return _dot(_gelu_tanh(pre).astype(BF16), w2_ref[...])

        for g in range(KV_HEADS):
            ko = summarise(xk_ref[g], w1k_ref, bias_k, w2k_ref)
            vo = summarise(xv_ref[g], w1v_ref, bias_v, w2v_ref)
            ms = jnp.sum(ko * ko, axis=-1, keepdims=True) * (1.0 / HEAD_DIM)
            kvcmp_ref[0, g] = (ko * lax.rsqrt(ms + RMS_EPS) * kg_ref[...] + vo).astype(BF16)


def _pagepass(ptab, kpages, vpages, pos, w1k, w1v, w2k, w2v, kg, nseq, npages):
    pp = PAGES_PER_STEP
    nsteps = npages // pp
    nchunk = npages * PAGE_SIZE // CMP_STRIDE
    feature_major = kpages.shape[1] == PAGE_COLS
    chunks_per_step = pp * PAGE_SIZE // SEL_CHUNK

    def page_spec(i):
        return pl.BlockSpec((1,) + kpages.shape[1:],
                            lambda s, j, pt: (pt[s * npages + j * pp + i], 0, 0))

    full = lambda shape: pl.BlockSpec(shape, lambda s, j, pt: (0,) * len(shape))
    grid_spec = pltpu.PrefetchScalarGridSpec(
        num_scalar_prefetch=1,
        grid=(nseq, nsteps),
        in_specs=([page_spec(i) for i in range(pp)] + [page_spec(i) for i in range(pp)]
                  + [full((8, 1024)), full((1024, 256)), full((1024, 256)),
                     full((128, 128)), full((128, 128)), full((1, 128))]),
        out_specs=[pl.BlockSpec((1, KV_HEADS, nchunk, LANES), lambda s, j, pt: (s, 0, 0, 0)),
                   pl.BlockSpec((1, KV_HEADS, chunks_per_step, 2 * HEAD_DIM, SEL_CHUNK),
                                lambda s, j, pt: (s, 0, j, 0, 0))],
        scratch_shapes=[pltpu.VMEM((KV_HEADS, nchunk, 1024), BF16),
                        pltpu.VMEM((KV_HEADS, nchunk, 1024), BF16),
                        pltpu.VMEM((nchunk + 8, 256), F32),
                        pltpu.VMEM((4, PAGE_SIZE, LANES), F32)],
    )
    return pl.pallas_call(
        functools.partial(_pagepass_kernel, nchunk=nchunk, feature_major=feature_major),
        grid_spec=grid_spec,
        out_shape=[jax.ShapeDtypeStruct((nseq, KV_HEADS, nchunk, LANES), BF16),
                   jax.ShapeDtypeStruct((nseq, KV_HEADS, npages * PAGE_SIZE // SEL_CHUNK, 2 * HEAD_DIM, SEL_CHUNK),
                                        BF16)],
        compiler_params=_cparams("parallel", "arbitrary"),
        name="pagepass",
    )(ptab, *([kpages] * pp), *([vpages] * pp), pos, w1k, w1v, w2k, w2v, kg)


WIN_TILES = (WINDOW + Q_BLOCK) // KEY_TILE


def _winpack_kernel(wk_ref, wv_ref, kwn_ref, vwn_ref, ksn_ref, vsn_ref, kvw_ref, tail_ref):
    t = kwn_ref.shape[0]
    past_tiles = WINDOW // KEY_TILE
    for c in range(past_tiles):
        ls = slice(c * KEY_TILE, (c + 1) * KEY_TILE)
        _store_kv_t(lambda g: kvw_ref.at[0, g, c], wk_ref[0, :, ls], wv_ref[0, :, ls])
    pad_t = lambda ref, n: jnp.concatenate([ref[...], jnp.zeros((n - t, 256), F32)], axis=0).T
    _store_kv_t(lambda g: kvw_ref.at[0, g, past_tiles], pad_t(kwn_ref, KEY_TILE), pad_t(vwn_ref, KEY_TILE))
    _store_kv_t(lambda g: tail_ref.at[0, g], pad_t(ksn_ref, SEL_CHUNK), pad_t(vsn_ref, SEL_CHUNK))


def _winpack(wk_t, wv_t, kwin, u, new_k, new_v, nseq, t):
    tile = (1, KV_HEADS, 2 * HEAD_DIM, SEL_CHUNK)
    return pl.pallas_call(
        _winpack_kernel,
        grid=(nseq,),
        in_specs=[pl.BlockSpec((1, 256, WINDOW), lambda s: (s, 0, 0)),
                  pl.BlockSpec((1, 256, WINDOW), lambda s: (s, 0, 0)),
                  pl.BlockSpec((t, 256), lambda s: (s, 0)),
                  pl.BlockSpec((t, 256), lambda s: (s, 5)),
                  pl.BlockSpec((t, 256), lambda s: (s, 1)),
                  pl.BlockSpec((t, 256), lambda s: (s, 1))],
        out_specs=[pl.BlockSpec((1, KV_HEADS, WIN_TILES, 2 * HEAD_DIM, KEY_TILE), lambda s: (s, 0, 0, 0, 0)),
                   pl.BlockSpec(tile, lambda s: (s, 0, 0, 0))],
        out_shape=[jax.ShapeDtypeStruct((nseq, KV_HEADS, WIN_TILES, 2 * HEAD_DIM, KEY_TILE), BF16),
                   jax.ShapeDtypeStruct((nseq,) + tile[1:], BF16)],
        compiler_params=_cparams("parallel"),
        name="winpack",
    )(wk_t, wv_t, kwin, u, new_k, new_v)


MASK_BIAS = -2.0 ** 100


def _attn_tables(slopes, n_sel_chunks, n_blocks):
    slope_tab = jnp.broadcast_to(slopes[:, None, None], (ATTN_HEADS, 8, LANES))
    pos = (jnp.arange(n_sel_chunks, dtype=jnp.int32)[:, None, None] * SEL_CHUNK
           + jnp.arange(SEL_CHUNK, dtype=jnp.int32)[None, None, :])
    blk = jnp.arange(n_blocks, dtype=jnp.int32)[None, :, None]
    esel = jnp.where(blk == (pos >> 6), MASK_BIAS, 0.0).astype(BF16)
    lane_src = jnp.arange(LANES)[None, :, None]
    gate_id = jnp.arange(12)[None, None, :]
    g_id = jnp.arange(KV_HEADS)[:, None, None]
    onehot = (lane_src == GATE_LANE0 + g_id * 12 + gate_id).astype(BF16)
    gexp = jnp.broadcast_to(onehot[..., None], (KV_HEADS, LANES, 12, LANES)).reshape(KV_HEADS, LANES, 12 * LANES)
    return slope_tab, esel, gexp


def _top_blocks_unselected(val, valid):
    tq, nselp = val.shape
    if tq < LANES:
        val = jnp.concatenate([val, jnp.zeros((LANES - tq, nselp), F32)], axis=0)
    val_t = jnp.concatenate([val[:, c * LANES:(c + 1) * LANES].T for c in range(nselp // LANES)], axis=0)
    blk_t = lax.broadcasted_iota(jnp.int32, (nselp, LANES), 0).astype(F32)

    def pick_top(_, carry):
        v, sel = carry
        best = jnp.max(v, axis=0, keepdims=True)
        idx = jnp.min(jnp.where(v == best, blk_t, 1e9), axis=0, keepdims=True)
        pick = blk_t == idx
        return jnp.where(pick, -3e38, v), jnp.where(pick, 1.0, sel)

    _, sel_t = lax.fori_loop(0, TOP_BLOCKS, pick_top, (val_t, jnp.zeros((nselp, LANES), F32)))
    sel = jnp.concatenate([sel_t[c * LANES:(c + 1) * LANES].T for c in range(nselp // LANES)], axis=1)
    return jnp.where(valid & (sel[0:tq] > 0.5), 0.0, 1.0)


def _attn_kernel(q_ref, misc_ref, kvc_ref, kvs_ref, kvw_ref, slope_ref, esel_ref, gexp_ref, *rest,
                 tq, q0, nchunk, nselp, lmain, wpos0, has_tail, fold_mask):
    if has_tail:
        tail_ref, o_ref, m_ref, l_ref, acc_ref = rest
    else:
        o_ref, m_ref, l_ref, acc_ref = rest
    qs = pl.program_id(2) * tq
    m_rows = Q_PER_KV * tq
    q = q_ref[...].reshape(m_rows, LANES)
    qb = q.astype(BF16)
    tpos = q0 + qs + lax.broadcasted_iota(jnp.int32, (tq, 1), 0)

    def per_head(x):
        return jnp.concatenate([x] * Q_PER_KV, axis=0)

    def add_by_head(s, fn):
        return jnp.concatenate([s[r * tq:(r + 1) * tq] + fn(r) for r in range(Q_PER_KV)], axis=0)

    def slope(r):
        return slope_ref[r, 0:1, 0:1]

    kvc = kvc_ref[0, 0]
    cmp_end = lax.broadcasted_iota(jnp.int32, (tq, nchunk), 1) * CMP_STRIDE + (2 * CMP_STRIDE - 1)
    cmp_mask = per_head(tpos >= cmp_end)
    cmp_pos = (lax.broadcasted_iota(jnp.int32, (1, nchunk), 1) * CMP_STRIDE + (2 * CMP_STRIDE - 1)).astype(F32)
    s = lax.dot_general(qb, kvc, NT_DIMS, preferred_element_type=F32)
    s = jnp.where(cmp_mask, add_by_head(s, lambda r: slope(r) * cmp_pos), NEG_INF)
    p = jnp.exp(s - jnp.max(s, axis=-1, keepdims=True)) * jnp.where(cmp_mask, 1.0, 0.0)
    p = p / jnp.maximum(jnp.sum(p, axis=-1, keepdims=True), TINY)
    o_cmp = _dot(p.astype(BF16), kvc)
    psum = p[0:tq] + p[tq:2 * tq] + p[2 * tq:3 * tq] + p[3 * tq:4 * tq]

    pool = jnp.where((lax.broadcasted_iota(jnp.int32, (nchunk, nselp), 0) >> 2)
                     == lax.broadcasted_iota(jnp.int32, (nchunk, nselp), 1), 1.0, 0.0).astype(BF16)
    hi, mid, lo = _split3(psum)
    imp = _dot(hi, pool) + _dot(mid, pool) + _dot(lo, pool)
    blk = lax.broadcasted_iota(jnp.int32, (tq, nselp), 1)
    cur = tpos >> 6
    forced = (blk == 0) | (blk == cur) | (blk == cur - 1)
    valid = blk <= cur
    val = jnp.where(valid, imp + jnp.where(forced, FORCE_BONUS, 0.0), NEG_INF)
    unsel = _top_blocks_unselected(val, valid)
    if fold_mask:
        lhs = jnp.where(_lo_half(m_rows), q.astype(F32), per_head(_swap_halves(unsel))).astype(BF16)
    else:
        lhs = per_head(unsel).astype(BF16)

    m_ref[...] = jnp.full_like(m_ref, NEG_INF)
    l_ref[...] = jnp.zeros_like(l_ref)
    acc_ref[...] = jnp.zeros_like(acc_ref)

    def sel_chunk(stream, kv_t, e_c, k0, causal_bias):
        n = kv_t.shape[1]
        if fold_mask:
            s = _dot(lhs, jnp.concatenate([kv_t[0:HEAD_DIM], e_c], axis=0))
        else:
            s = _dot(qb, kv_t) + _dot(lhs, e_c)
        pos = (k0 + lax.broadcasted_iota(jnp.int32, (1, n), 1)).astype(F32)
        if causal_bias is None:
            s = add_by_head(s, lambda r: slope(r) * pos)
        else:
            s = add_by_head(s, lambda r: slope(r) * pos + causal_bias)
        m_old = m_ref[stream]
        m_new = jnp.maximum(m_old, jnp.max(s, axis=-1, keepdims=True))
        alpha = jnp.exp(m_old - m_new)
        p = [jnp.exp(s[:, c * LANES:(c + 1) * LANES] - m_new) for c in range(n // LANES)]
        part = p[0]
        for c in range(1, n // LANES):
            part = part + p[c]
        l_ref[stream] = alpha * l_ref[stream] + part
        acc_ref[stream] = alpha * acc_ref[stream] + lax.dot_general(
            jnp.concatenate(p, axis=1).astype(BF16), kv_t, NT_DIMS, preferred_element_type=F32)
        m_ref[stream] = m_new

    def pair(i, carry):
        for stream in range(2):
            kc = 2 * i + stream
            sel_chunk(stream, kvs_ref[0, 0, kc], esel_ref[kc], kc * SEL_CHUNK, None)
        return carry

    n_full = jnp.minimum((q0 + qs) // SEL_CHUNK, lmain // SEL_CHUNK)
    lax.fori_loop(0, n_full // 2, pair, 0)

    @pl.when(n_full % 2 == 1)
    def _():
        sel_chunk(0, kvs_ref[0, 0, n_full - 1], esel_ref[n_full - 1], (n_full - 1) * SEL_CHUNK, None)

    key_j = lax.broadcasted_iota(jnp.int32, (tq, SEL_CHUNK), 1)
    if has_tail:
        causal = jnp.where(tpos >= lmain + key_j, 0.0, NEG_INF)
        sel_chunk(1, tail_ref[0, 0], esel_ref[lmain // SEL_CHUNK], lmain, causal)
    else:
        causal = jnp.where(tpos >= n_full * SEL_CHUNK + key_j, 0.0, NEG_INF)
        sel_chunk(1, kvs_ref[0, 0, n_full], esel_ref[n_full], n_full * SEL_CHUNK, causal)
    m = jnp.maximum(m_ref[0], m_ref[1])
    w0 = jnp.exp(m_ref[0] - m)
    w1 = jnp.exp(m_ref[1] - m)
    l_sel = jnp.sum(w0 * l_ref[0] + w1 * l_ref[1], axis=-1, keepdims=True)
    o_sel = (w0 * acc_ref[0] + w1 * acc_ref[1]) / jnp.maximum(l_sel, TINY)

    tile0 = jnp.maximum(q0 + qs - WINDOW - wpos0, 0) // KEY_TILE
    tiles = [kvw_ref[0, 0, tile0 + j] for j in range(WIN_TILES)]
    key_b = lax.broadcasted_iota(jnp.int32, (1, KEY_TILE), 1)
    s = []
    for j in range(WIN_TILES):
        pos = wpos0 + (tile0 + j) * KEY_TILE + key_b
        dist = tpos - pos
        bias = jnp.where((dist >= 0) & (dist <= WINDOW), 0.0, NEG_INF)
        s.append(add_by_head(_dot(qb, tiles[j]), lambda r: slope(r) * pos.astype(F32) + bias))
    m = s[0]
    for j in range(1, WIN_TILES):
        m = jnp.maximum(m, s[j])
    m = jnp.max(m, axis=-1, keepdims=True)
    p = [jnp.exp(sj - m) for sj in s]
    part = p[0]
    for j in range(1, WIN_TILES):
        part = part + p[j]
    o_win = lax.dot_general(p[0].astype(BF16), tiles[0], NT_DIMS, preferred_element_type=F32)
    for j in range(1, WIN_TILES):
        o_win = o_win + lax.dot_general(p[j].astype(BF16), tiles[j], NT_DIMS, preferred_element_type=F32)
    o_win = o_win / jnp.maximum(jnp.sum(part, axis=-1, keepdims=True), TINY)

    hi, mid, lo = _split3(jax.nn.sigmoid(misc_ref[...]))
    gexp = gexp_ref[0]
    gates = _dot(hi, gexp) + _dot(mid, gexp) + _dot(lo, gexp)
    comb = []
    for r in range(Q_PER_KV):
        rs = slice(r * tq, (r + 1) * tq)
        gate = [gates[:, (3 * r + c) * LANES:(3 * r + c + 1) * LANES] for c in range(3)]
        comb.append(gate[0] * o_cmp[rs] + gate[1] * o_sel[rs] + gate[2] * o_win[rs])
    lo_half = _lo_half(tq)
    o_ref[:, 0:LANES] = jnp.where(lo_half, _swap_halves(comb[0]), comb[1])
    o_ref[:, LANES:2 * LANES] = jnp.where(lo_half, _swap_halves(comb[2]), comb[3])


def _attn(tables, qhm, u, kvcmp, kvsel, kvwin, tail, nseq, t, q0, wpos0, win_head_major):
    slope_tab, esel, gexp = tables
    tq = min(t, Q_BLOCK)
    nqb = t // tq
    nchunk = kvcmp.shape[2]
    sel_chunks = kvsel.shape[2]
    lmain = sel_chunks * SEL_CHUNK
    fold_mask = esel.shape[1] == HEAD_DIM
    nselp = LANES if fold_mask else esel.shape[1]
    win_tiles = kvwin.shape[2]
    m_rows = Q_PER_KV * tq
    if win_head_major:
        win_map = lambda s, g, i: (g, s, 0, 0, 0)
    else:
        win_map = lambda s, g, i: (s, g, 0, 0, 0)
    in_specs = [pl.BlockSpec((Q_PER_KV, tq, LANES), lambda s, g, i: (g, s * nqb + i, 0)),
                pl.BlockSpec((tq, LANES), lambda s, g, i: (s * nqb + i, U_MISC // LANES)),
                pl.BlockSpec((1, 1, nchunk, LANES), lambda s, g, i: (s, g, 0, 0)),
                pl.BlockSpec((1, 1, sel_chunks, 2 * HEAD_DIM, SEL_CHUNK), lambda s, g, i: (s, g, 0, 0, 0)),
                pl.BlockSpec((1, 1, win_tiles, 2 * HEAD_DIM, KEY_TILE), win_map),
                pl.BlockSpec((Q_PER_KV, 8, LANES), lambda s, g, i: (g, 0, 0)),
                pl.BlockSpec(esel.shape, lambda s, g, i: (0, 0, 0)),
                pl.BlockSpec((1, LANES, 12 * LANES), lambda s, g, i: (g, 0, 0))]
    args = [qhm, u, kvcmp, kvsel, kvwin, slope_tab, esel, gexp]
    if tail is not None:
        in_specs.append(pl.BlockSpec((1, 1, 2 * HEAD_DIM, SEL_CHUNK), lambda s, g, i: (s, g, 0, 0)))
        args.append(tail)
    stream_state = pltpu.VMEM((2, m_rows, LANES), F32)
    return pl.pallas_call(
        functools.partial(_attn_kernel, tq=tq, q0=q0, nchunk=nchunk, nselp=nselp, lmain=lmain,
                          wpos0=wpos0, has_tail=tail is not None, fold_mask=fold_mask),
        grid=(nseq, KV_HEADS, nqb),
        in_specs=in_specs,
        out_specs=pl.BlockSpec((tq, 2 * LANES), lambda s, g, i: (s * nqb + i, g)),
        out_shape=jax.ShapeDtypeStruct((nseq * t, D_ATTN), F32),
        scratch_shapes=[stream_state, stream_state, stream_state],
        compiler_params=_cparams("parallel", "parallel", "arbitrary"),
        name="nsa_attn",
    )(*args)


def _layer_params(l, norm_mix, w_in, conv_a_w, conv_b_w, conv_b_bias, dt_bias, a_log, d_skip,
                  q_norm, k_norm, cmp_pos, cmp_w1, cmp_w2, norm_out, w_out, norm_ffn, w_gate, w_up, w_down):
    w = w_in[l]
    w_perm = jnp.concatenate(
        [w[:, 4104:5640], w[:, 0:1536], w[:, 2048:3072], w[:, 3080:4104], w[:, 1536:2048],
         w[:, 3072:3080], w[:, 5640:5688], jnp.zeros((D_MODEL, U_COLS - 5688), F32)], axis=1).astype(BF16)
    pad8 = lambda v: jnp.pad(v, (0, LANES - v.shape[0])).reshape(1, LANES)
    head_id = jnp.arange(256) // HEAD_DIM
    pos = cmp_pos[l].reshape(4, 1024)
    w1 = cmp_w1[l]
    w2 = cmp_w2[l]
    zeros_w2 = jnp.zeros((CMP_HIDDEN, HEAD_DIM), F32)
    return dict(
        g_mix=norm_mix[l].reshape(1, D_MODEL), w_in=w_perm,
        conv_a_w=conv_a_w[l], conv_b_w=conv_b_w[l], conv_b_bias=conv_b_bias[l].reshape(1, SSM_CONV_DIM),
        dt_bias=pad8(dt_bias[l]), a_log=pad8(a_log[l]),
        d_skip=jnp.repeat(d_skip[l], HEAD_DIM).reshape(1, D_SSM),
        mavg=jnp.where(head_id[:, None] == head_id[None, :], 1.0 / HEAD_DIM, 0.0).astype(BF16),
        q_gain=jnp.tile(q_norm[l], 4).reshape(1, 256),
        k_gain1=jnp.tile(k_norm[l, 1], 4).reshape(1, 256),
        k_gain2=jnp.tile(k_norm[l, 2], 4).reshape(1, 256),
        k_gain0=pad8(k_norm[l, 0]),
        pos=jnp.pad(pos, ((0, 4), (0, 0))),
        w1k=jnp.concatenate([w1[0, 0:1024], w1[0, 1024:2048]], axis=1).astype(BF16),
        w1v=jnp.concatenate([w1[1, 0:1024], w1[1, 1024:2048]], axis=1).astype(BF16),
        w2k=jnp.concatenate([w2[0], zeros_w2], axis=1).astype(BF16),
        w2v=jnp.concatenate([zeros_w2, w2[1]], axis=1).astype(BF16),
        g_out=norm_out[l].reshape(1, D_MODEL), w_out=w_out[l].astype(BF16),
        g_ffn=norm_ffn[l].reshape(1, D_MODEL),
        w_gate=w_gate[l].astype(BF16), w_up=w_up[l].astype(BF16), w_down=w_down[l].astype(BF16),
    )


def _group_tables(slopes, t, q0):
    lmain = q0 if q0 else t
    n_sel_chunks = lmain // SEL_CHUNK + (1 if q0 else 0)
    n_blocks = n_sel_chunks * SEL_CHUNK // SEL_BLOCK
    n_blocks = HEAD_DIM if n_blocks <= HEAD_DIM else -(-n_blocks // LANES) * LANES
    return _attn_tables(slopes, n_sel_chunks, n_blocks)


def _layer(x, nseq, t, q0, p, tables, conv_a_prefix, conv_b_prefix, ssm_h0, past):
    u = _in_proj(x, p["g_mix"], p["w_in"])
    ya, new_conv_a = _mixer_a(u, conv_a_prefix, p["conv_a_w"], nseq, t)
    yb, new_conv_b, new_ssm = _ssd(u, conv_b_prefix, ssm_h0, p["conv_b_w"], p["conv_b_bias"],
                                   p["dt_bias"], p["a_log"], p["d_skip"], nseq, t)
    prep_out = _prep(u, p["mavg"], p["q_gain"], p["k_gain1"], p["k_gain2"],
                     BF16 if past is None else F32, past is None)
    qhm, new_k, new_v, kwin = prep_out[:4]
    vwin = u[:, U_KV + 1280:U_KV + 1536]
    cmp_args = (p["pos"], p["w1k"], p["w1v"], p["w2k"], p["w2v"], p["k_gain0"])
    if past is None:
        npages = t // PAGE_SIZE
        ptab = jnp.arange(nseq * npages, dtype=jnp.int32)
        kvcmp, kvsel = _pagepass(ptab, new_k.reshape(-1, PAGE_SIZE, PAGE_COLS),
                                 new_v.reshape(-1, PAGE_SIZE, PAGE_COLS), *cmp_args, nseq, npages)
        kvwin_arr = prep_out[4].reshape(KV_HEADS, nseq, t // KEY_TILE, 2 * HEAD_DIM, KEY_TILE)
        yc = _attn(tables, qhm, u, kvcmp, kvsel, kvwin_arr, None, nseq, t, 0, 0, True)
        new_win_k = kwin.reshape(nseq, t, 256)[:, t - WINDOW:]
        new_win_v = vwin.reshape(nseq, t, 256)[:, t - WINDOW:]
    else:
        ptab, k_pages, v_pages, npages, win_k_t, win_v_t, win_k, win_v = past
        kvcmp, kvsel = _pagepass(ptab, k_pages, v_pages, *cmp_args, nseq, npages)
        kvwin_arr, tail = _winpack(win_k_t, win_v_t, kwin, u, new_k, new_v, nseq, t)
        yc = _attn(tables, qhm, u, kvcmp, kvsel, kvwin_arr, tail, nseq, t, q0, q0 - WINDOW, False)
        new_win_k = jnp.concatenate([win_k[:, t:], kwin.reshape(nseq, t, 256)], axis=1)
        new_win_v = jnp.concatenate([win_v[:, t:], vwin.reshape(nseq, t, 256)], axis=1)
    x = _out_proj(x, ya, yb, yc, p["g_out"], p["w_out"])
    x = _ffn(x, p["g_ffn"], p["w_gate"], p["w_up"], p["w_down"])
    hd = (KV_HEADS, HEAD_DIM)
    state = (new_k.reshape(nseq, t, 2, *hd), new_v.reshape(nseq, t, 2, *hd),
             new_win_k.reshape(nseq, WINDOW, *hd), new_win_v.reshape(nseq, WINDOW, *hd),
             new_conv_a, new_conv_b, new_ssm)
    return x, state


def kernel(x_prompt, x_sample, cache_k, cache_v, cache_win_k, cache_win_v, state_conv_a, state_conv_b,
           state_ssm, page_table, norm_mix, w_in, conv_a_w, conv_b_w, conv_b_bias, dt_bias, a_log, d_skip,
           q_norm, k_norm, cmp_pos, cmp_w1, cmp_w2, norm_out, w_out, norm_ffn, w_gate, w_up, w_down):
    bsz, t_prompt, _ = x_prompt.shape
    dec_b, t_dec, _ = x_sample.shape
    depth, n_pool = cache_k.shape[0], cache_k.shape[1]
    npages = page_table.shape[1]
    past_len = npages * PAGE_SIZE
    assert cache_win_k.shape[2] == WINDOW and t_prompt >= WINDOW + Q_BLOCK and t_prompt % Q_BLOCK == 0
    assert t_dec == 8 and npages % PAGES_PER_STEP == 0 and (t_prompt // PAGE_SIZE) % PAGES_PER_STEP == 0

    slopes = jnp.exp2(-8.0 * jnp.arange(1, ATTN_HEADS + 1, dtype=F32) / ATTN_HEADS)
    prompt_tables = _group_tables(slopes, t_prompt, 0)
    sample_tables = _group_tables(slopes, t_dec, past_len)
    to_pages_t = lambda c: jnp.transpose(c, (0, 1, 3, 4, 5, 2)).reshape(depth * n_pool, PAGE_COLS, PAGE_SIZE)
    to_win_t = lambda w: jnp.transpose(w, (0, 2, 3, 1)).reshape(dec_b, 256, WINDOW)
    k_pages = to_pages_t(cache_k)
    v_pages = to_pages_t(cache_v)
    hp = x_prompt.reshape(bsz * t_prompt, D_MODEL)
    hs = x_sample.reshape(dec_b * t_dec, D_MODEL)
    zeros = lambda *shape: jnp.zeros(shape, F32)
    prompt_states, sample_states = [], []
    for l in range(depth):
        p = _layer_params(l, norm_mix, w_in, conv_a_w, conv_b_w, conv_b_bias, dt_bias, a_log, d_skip,
                          q_norm, k_norm, cmp_pos, cmp_w1, cmp_w2, norm_out, w_out, norm_ffn,
                          w_gate, w_up, w_down)
        hp, st_p = _layer(hp, bsz, t_prompt, 0, p, prompt_tables,
                          zeros(bsz, CONV_A_WIDTH - 1, D_CONV), zeros(bsz, SSM_CONV_WIDTH - 1, SSM_CONV_DIM),
                          zeros(bsz, SSM_HEADS, HEAD_DIM, SSM_STATE), None)
        ptab = (page_table + l * n_pool).reshape(-1).astype(jnp.int32)
        past = (ptab, k_pages, v_pages, npages, to_win_t(cache_win_k[l]), to_win_t(cache_win_v[l]),
                cache_win_k[l].reshape(dec_b, WINDOW, 256), cache_win_v[l].reshape(dec_b, WINDOW, 256))
        hs, st_s = _layer(hs, dec_b, t_dec, past_len, p, sample_tables,
                          state_conv_a[l], state_conv_b[l], state_ssm[l], past)
        prompt_states.append(st_p)
        sample_states.append(st_s)

    stack = lambda states, j: jnp.stack([s[j] for s in states], axis=0)
    return ((hp.reshape(bsz, t_prompt, D_MODEL), hs.reshape(dec_b, t_dec, D_MODEL))
            + tuple(stack(prompt_states, j) for j in range(7))
            + tuple(stack(sample_states, j) for j in range(7)))
```

```python
import functools

import jax
import jax.numpy as jnp
from jax import lax
from jax.experimental import pallas as pl
from jax.experimental.pallas import tpu as pltpu

F32 = jnp.float32
BF16 = jnp.bfloat16

D_MODEL = 2048
HEAD_DIM = 64
D_CONV = 512
D_SSM = 512
D_ATTN = 1024
CONV_A_WIDTH = 3
SSM_HEADS = 8
SSM_STATE = 128
SSM_CONV_WIDTH = 4
SSM_CHUNK = 128
SSM_CONV_DIM = 1024
ATTN_HEADS = 16
KV_HEADS = 4
Q_PER_KV = 4
KV_COLS = 1536
CMP_STRIDE = 16
CMP_HIDDEN = 128
SEL_BLOCK = 64
TOP_BLOCKS = 16
WINDOW = 512
Q_BLOCK = 128
PAGE_SIZE = 128
PAGE_COLS = 512
D_FF = 5632
RMS_EPS = 1e-6
NEG_INF = -1e30
TINY = 1e-30
FORCE_BONUS = 1e3
ATTN_SCALE = HEAD_DIM ** -0.5

U_KV = 0
U_A = 1536
U_XBC = 3072
U_Q = 4096
U_Z = 5120
U_MISC = 5632
U_COLS = 5760
GATE_LANE0 = 8

LANES = 128
KEY_TILE = 128
SEL_CHUNK = 256
VMEM_LIMIT_BYTES = 56 * 2 ** 20

NT_DIMS = (((1,), (1,)), ((), ()))


def _cparams(*sem):
    return pltpu.CompilerParams(dimension_semantics=sem, vmem_limit_bytes=VMEM_LIMIT_BYTES)


def _tile(n, pref):
    t = min(n, pref)
    while n % t:
        t //= 2
    return t


def _dot(a, b):
    return jnp.dot(a, b, preferred_element_type=F32)


def _split3(x):
    hi = x.astype(BF16)
    r1 = x - hi.astype(F32)
    mid = r1.astype(BF16)
    lo = (r1 - mid.astype(F32)).astype(BF16)
    return hi, mid, lo


def _lo_half(rows):
    return lax.broadcasted_iota(jnp.int32, (rows, LANES), 1) < HEAD_DIM


def _swap_halves(x):
    return pltpu.roll(x, HEAD_DIM, 1)


def _store_kv_t(dst, k_t, v_t):
    for g in range(KV_HEADS):
        view = dst(g)
        view[0:HEAD_DIM, :] = k_t[g * HEAD_DIM:(g + 1) * HEAD_DIM].astype(BF16)
        view[HEAD_DIM:2 * HEAD_DIM, :] = v_t[g * HEAD_DIM:(g + 1) * HEAD_DIM].astype(BF16)


def _in_proj_kernel(x_ref, g_ref, w_ref, o_ref, xn_ref):
    @pl.when(pl.program_id(1) == 0)
    def _():
        x = x_ref[...]
        ms = jnp.mean(x * x, axis=-1, keepdims=True)
        xn_ref[...] = (x * lax.rsqrt(ms + RMS_EPS) * g_ref[...]).astype(BF16)

    o_ref[...] = _dot(xn_ref[...], w_ref[...])


def _in_proj(x, gain, w):
    n = x.shape[0]
    tm = _tile(n, 1024)
    tn = 1152
    return pl.pallas_call(
        _in_proj_kernel,
        grid=(n // tm, U_COLS // tn),
        in_specs=[pl.BlockSpec((tm, D_MODEL), lambda i, j: (i, 0)),
                  pl.BlockSpec((1, D_MODEL), lambda i, j: (0, 0)),
                  pl.BlockSpec((D_MODEL, tn), lambda i, j: (0, j))],
        out_specs=pl.BlockSpec((tm, tn), lambda i, j: (i, j)),
        out_shape=jax.ShapeDtypeStruct((n, U_COLS), F32),
        scratch_shapes=[pltpu.VMEM((tm, D_MODEL), BF16)],
        compiler_params=_cparams("parallel", "arbitrary"),
        name="in_proj",
    )(x, gain, w)


def _out_proj_kernel(x_ref, ya_ref, yb_ref, yc_ref, g_ref, w_ref, o_ref, mg_ref):
    @pl.when(pl.program_id(1) == 0)
    def _():
        def nrm(y, g):
            ms = jnp.mean(y * y, axis=-1, keepdims=True)
            return (y * lax.rsqrt(ms + RMS_EPS) * g).astype(BF16)

        mg_ref[:, 0:512] = nrm(ya_ref[...], g_ref[:, 0:512])
        mg_ref[:, 512:1024] = nrm(yb_ref[...], g_ref[:, 512:1024])
        mg_ref[:, 1024:2048] = nrm(yc_ref[...], g_ref[:, 1024:2048])

    o_ref[...] = x_ref[...] + _dot(mg_ref[...], w_ref[...])


def _out_proj(x, ya, yb, yc, gain, w):
    n = x.shape[0]
    tm = _tile(n, 512)
    tn = 1024
    return pl.pallas_call(
        _out_proj_kernel,
        grid=(n // tm, D_MODEL // tn),
        in_specs=[pl.BlockSpec((tm, tn), lambda i, j: (i, j)),
                  pl.BlockSpec((tm, D_CONV), lambda i, j: (i, 0)),
                  pl.BlockSpec((tm, D_SSM), lambda i, j: (i, 0)),
                  pl.BlockSpec((tm, D_ATTN), lambda i, j: (i, 0)),
                  pl.BlockSpec((1, D_MODEL), lambda i, j: (0, 0)),
                  pl.BlockSpec((D_MODEL, tn), lambda i, j: (0, j))],
        out_specs=pl.BlockSpec((tm, tn), lambda i, j: (i, j)),
        out_shape=jax.ShapeDtypeStruct((n, D_MODEL), F32),
        scratch_shapes=[pltpu.VMEM((tm, D_MODEL), BF16)],
        compiler_params=_cparams("parallel", "arbitrary"),
        name="out_proj",
    )(x, ya, yb, yc, gain, w)


FFN_CHAINS = 2


def _ffn_kernel(x_ref, g_ref, wg_ref, wu_ref, wd_ref, o_ref, h_ref, acc_ref):
    f = pl.program_id(1)

    @pl.when(f == 0)
    def _():
        x = x_ref[...]
        ms = jnp.mean(x * x, axis=-1, keepdims=True)
        h_ref[...] = (x * lax.rsqrt(ms + RMS_EPS) * g_ref[...]).astype(BF16)
        acc_ref[...] = jnp.zeros_like(acc_ref)

    h = h_ref[...]
    tf = wg_ref.shape[1]
    part = None
    for c in range(FFN_CHAINS):
        cs = slice(c * tf // FFN_CHAINS, (c + 1) * tf // FFN_CHAINS)
        a = _dot(h, wg_ref[:, cs])
        b = _dot(h, wu_ref[:, cs])
        d = _dot((a * jax.nn.sigmoid(a) * b).astype(BF16), wd_ref[cs, :])
        part = d if part is None else part + d
    acc_ref[...] += part

    @pl.when(f == pl.num_programs(1) - 1)
    def _():
        o_ref[...] = x_ref[...] + acc_ref[...]


def _ffn(x, gain, wg, wu, wd):
    n = x.shape[0]
    tm = _tile(n, 512)
    tf = 512
    return pl.pallas_call(
        _ffn_kernel,
        grid=(n // tm, D_FF // tf),
        in_specs=[pl.BlockSpec((tm, D_MODEL), lambda i, f: (i, 0)),
                  pl.BlockSpec((1, D_MODEL), lambda i, f: (0, 0)),
                  pl.BlockSpec((D_MODEL, tf), lambda i, f: (0, f)),
                  pl.BlockSpec((D_MODEL, tf), lambda i, f: (0, f)),
                  pl.BlockSpec((tf, D_MODEL), lambda i, f: (f, 0))],
        out_specs=pl.BlockSpec((tm, D_MODEL), lambda i, f: (i, 0)),
        out_shape=jax.ShapeDtypeStruct((n, D_MODEL), F32),
        scratch_shapes=[pltpu.VMEM((tm, D_MODEL), BF16), pltpu.VMEM((tm, D_MODEL), F32)],
        compiler_params=_cparams("parallel", "arbitrary"),
        name="ffn",
    )(x, gain, wg, wu, wd)


def _mixer_a_kernel(u_ref, pre_ref, w_ref, y_ref, newpre_ref, ext_ref, *, rows):
    @pl.when(pl.program_id(1) == 0)
    def _():
        ext_ref[6:8, :] = pre_ref[0]

    u = u_ref[...]
    v = u[:, 512:1024] * u[:, 1024:1536]
    ext_ref[8:8 + rows, :] = v
    w = w_ref[...]
    y = ext_ref[6:6 + rows, :] * w[0:1] + ext_ref[7:7 + rows, :] * w[1:2] + v * w[2:3]
    y_ref[...] = u[:, 0:512] * y
    last = ext_ref[8 + rows - 2:8 + rows, :]
    ext_ref[6:8, :] = last
    newpre_ref[0] = last


def _mixer_a(u, prefix, w, nseq, t):
    rows = _tile(t, 512)
    nblk = t // rows
    return pl.pallas_call(
        functools.partial(_mixer_a_kernel, rows=rows),
        grid=(nseq, nblk),
        in_specs=[pl.BlockSpec((rows, 3 * D_CONV), lambda s, j: (s * nblk + j, U_A // (3 * D_CONV))),
                  pl.BlockSpec((1, CONV_A_WIDTH - 1, D_CONV), lambda s, j: (s, 0, 0)),
                  pl.BlockSpec((CONV_A_WIDTH, D_CONV), lambda s, j: (0, 0))],
        out_specs=[pl.BlockSpec((rows, D_CONV), lambda s, j: (s * nblk + j, 0)),
                   pl.BlockSpec((1, CONV_A_WIDTH - 1, D_CONV), lambda s, j: (s, 0, 0))],
        out_shape=[jax.ShapeDtypeStruct((nseq * t, D_CONV), F32),
                   jax.ShapeDtypeStruct((nseq, CONV_A_WIDTH - 1, D_CONV), F32)],
        scratch_shapes=[pltpu.VMEM((8 + rows, D_CONV), F32)],
        compiler_params=_cparams("parallel", "arbitrary"),
        name="mixer_a",
    )(u, prefix, w)


def _ssd_kernel(z_ref, xbc_ref, misc_ref, pre_ref, h0_ref, cw_ref, cb_ref, dtb_ref, alog_ref, dsk_ref,
                y_ref, newpre_ref, hout_ref, ext_ref, st_ref, pad_ref, *, lin):
    L = SSM_CHUNK
    c = pl.program_id(1)

    @pl.when(c == 0)
    def _():
        if lin < L:
            ext_ref[...] = jnp.zeros_like(ext_ref)
            pad_ref[...] = jnp.zeros_like(pad_ref)
        ext_ref[5:8, :] = pre_ref[0]
        st_ref[...] = h0_ref[0].reshape(SSM_HEADS * HEAD_DIM, SSM_STATE)

    ext_ref[8:8 + lin, :] = xbc_ref[...]
    cw = cw_ref[...]
    acc = (ext_ref[5:5 + L, :] * cw[0:1] + ext_ref[6:6 + L, :] * cw[1:2]
           + ext_ref[7:7 + L, :] * cw[2:3] + ext_ref[8:8 + L, :] * cw[3:4])
    acc = acc + cb_ref[...]
    xbc = acc * jax.nn.sigmoid(acc)
    newp = ext_ref[8 + lin - 3:8 + lin, :]
    ext_ref[5:8, :] = newp
    newpre_ref[0] = newp

    if lin < L:
        pad_ref[0:lin, :] = misc_ref[...]
        misc = pad_ref[...]
    else:
        misc = misc_ref[...]
    rows = lax.broadcasted_iota(jnp.int32, (L, L), 0)
    cols = lax.broadcasted_iota(jnp.int32, (L, L), 1)
    x = misc + dtb_ref[...]
    dt = jnp.maximum(x, 0.0) + jnp.log1p(jnp.exp(-jnp.abs(x)))
    if lin < L:
        dt = jnp.where(rows < lin, dt, 0.0)
    la = dt * (-jnp.exp(alog_ref[...]))

    causal = rows >= cols
    tril = jnp.where(causal, 1.0, 0.0).astype(BF16)
    hi, mid, lo = _split3(la)
    cum = _dot(tril, hi) + _dot(tril, mid) + _dot(tril, lo)
    cum_t = cum.T
    last = cum[L - 1:L, :]
    lo_half = cols < HEAD_DIM
    top_rows = rows < HEAD_DIM

    z = z_ref[...]
    dsk = dsk_ref[...]
    for g in range(2):
        bg = xbc[:, 512 + g * 128:512 + (g + 1) * 128].astype(BF16)
        cg = xbc[:, 768 + g * 128:768 + (g + 1) * 128].astype(BF16)
        gram = lax.dot_general(cg, bg, NT_DIMS, preferred_element_type=F32)
        for i in (2 * g, 2 * g + 1):
            a, b = 2 * i, 2 * i + 1
            sl = slice(i * LANES, (i + 1) * LANES)
            xs_p = xbc[:, sl]
            col_a = cum[:, a:a + 1]
            col_b = cum[:, b:b + 1]
            xdt = xs_p * jnp.where(lo_half, dt[:, a:a + 1], dt[:, b:b + 1])
            dec_a = jnp.exp(jnp.where(causal, col_a - cum_t[a:a + 1, :], NEG_INF))
            dec_b = jnp.exp(jnp.where(causal, col_b - cum_t[b:b + 1, :], NEG_INF))
            xa = jnp.where(lo_half, xdt, 0.0).astype(BF16)
            xb = jnp.where(lo_half, 0.0, xdt).astype(BF16)
            y_intra = _dot((gram * dec_a).astype(BF16), xa) + _dot((gram * dec_b).astype(BF16), xb)
            st = st_ref[sl, :]
            y_inter = (lax.dot_general(cg, st.astype(BF16), NT_DIMS, preferred_element_type=F32)
                       * jnp.where(lo_half, jnp.exp(col_a), jnp.exp(col_b)))
            to_end = jnp.where(lo_half, jnp.exp(last[:, a:a + 1] - col_a), jnp.exp(last[:, b:b + 1] - col_b))
            xw_t = (xdt * to_end).T.astype(BF16)
            decay = jnp.where(top_rows, jnp.exp(last[:, a:a + 1]), jnp.exp(last[:, b:b + 1]))
            st_ref[sl, :] = st * decay + _dot(xw_t, bg)
            y = y_intra + y_inter + dsk[:, sl] * xs_p
            zp = z[:, sl]
            y_ref[:, sl] = y[0:lin] * (zp * jax.nn.sigmoid(zp))

    @pl.when(c == pl.num_programs(1) - 1)
    def _():
        hout_ref[0] = st_ref[...].reshape(SSM_HEADS, HEAD_DIM, SSM_STATE)


def _ssd(u, prefix, h0, cw, cb, dtb, alog, dsk, nseq, t):
    lin = min(t, SSM_CHUNK)
    nc = t // lin
    full = lambda shape: pl.BlockSpec(shape, lambda s, c: (0,) * len(shape))
    return pl.pallas_call(
        functools.partial(_ssd_kernel, lin=lin),
        grid=(nseq, nc),
        in_specs=[pl.BlockSpec((lin, D_SSM), lambda s, c: (s * nc + c, U_Z // D_SSM)),
                  pl.BlockSpec((lin, SSM_CONV_DIM), lambda s, c: (s * nc + c, U_XBC // SSM_CONV_DIM)),
                  pl.BlockSpec((lin, LANES), lambda s, c: (s * nc + c, U_MISC // LANES)),
                  pl.BlockSpec((1, SSM_CONV_WIDTH - 1, SSM_CONV_DIM), lambda s, c: (s, 0, 0)),
                  pl.BlockSpec((1, SSM_HEADS, HEAD_DIM, SSM_STATE), lambda s, c: (s, 0, 0, 0)),
                  full((SSM_CONV_WIDTH, SSM_CONV_DIM)), full((1, SSM_CONV_DIM)),
                  full((1, LANES)), full((1, LANES)), full((1, D_SSM))],
        out_specs=[pl.BlockSpec((lin, D_SSM), lambda s, c: (s * nc + c, 0)),
                   pl.BlockSpec((1, SSM_CONV_WIDTH - 1, SSM_CONV_DIM), lambda s, c: (s, 0, 0)),
                   pl.BlockSpec((1, SSM_HEADS, HEAD_DIM, SSM_STATE), lambda s, c: (s, 0, 0, 0))],
        out_shape=[jax.ShapeDtypeStruct((nseq * t, D_SSM), F32),
                   jax.ShapeDtypeStruct((nseq, SSM_CONV_WIDTH - 1, SSM_CONV_DIM), F32),
                   jax.ShapeDtypeStruct((nseq, SSM_HEADS, HEAD_DIM, SSM_STATE), F32)],
        scratch_shapes=[pltpu.VMEM((8 + SSM_CHUNK, SSM_CONV_DIM), F32),
                        pltpu.VMEM((SSM_HEADS * HEAD_DIM, SSM_STATE), F32),
                        pltpu.VMEM((SSM_CHUNK, LANES), F32)],
        compiler_params=_cparams("parallel", "arbitrary"),
        name="ssd",
    )(u, u, u, prefix, h0, cw, cb, dtb, alog, dsk)


def _headnorm(x, mavg, gain):
    x2 = x * x
    hi = x2.astype(BF16)
    lo = (x2 - hi.astype(F32)).astype(BF16)
    ms = _dot(hi, mavg) + _dot(lo, mavg)
    return x * lax.rsqrt(ms + RMS_EPS) * gain


def _prep_kernel(q_ref, kv_ref, mavg_ref, qg_ref, kg1_ref, kg2_ref,
                 qhm_ref, nk_ref, nv_ref, kwin_ref, *maybe_kvwin_ref):
    mavg = mavg_ref[...]
    lo = _lo_half(q_ref.shape[0])
    for c in range(4):
        qn = _headnorm(q_ref[:, c * 256:(c + 1) * 256], mavg, qg_ref[...]) * ATTN_SCALE
        for cc in range(2):
            col = qn[:, cc * LANES:(cc + 1) * LANES]
            h = c * 4 + cc * 2
            qhm_ref[h] = jnp.where(lo, col, 0.0).astype(qhm_ref.dtype)
            qhm_ref[h + 1] = jnp.where(lo, _swap_halves(col), 0.0).astype(qhm_ref.dtype)
    ksel = _headnorm(kv_ref[:, 512:768], mavg, kg1_ref[...])
    kwin = _headnorm(kv_ref[:, 1024:1280], mavg, kg2_ref[...])
    nk_ref[:, 0:256] = kv_ref[:, 0:256]
    nk_ref[:, 256:512] = ksel
    nv_ref[:, 0:256] = kv_ref[:, 256:512]
    nv_ref[:, 256:512] = kv_ref[:, 768:1024]
    kwin_ref[...] = kwin
    if maybe_kvwin_ref:
        kvwin_ref, = maybe_kvwin_ref
        for c in range(q_ref.shape[0] // KEY_TILE):
            rs = slice(c * KEY_TILE, (c + 1) * KEY_TILE)
            _store_kv_t(lambda g: kvwin_ref.at[g, c], kwin[rs].T, kv_ref[rs, 1280:1536].T)


def _prep(u, mavg, qg, kg1, kg2, q_dtype, emit_window_tiles):
    n = u.shape[0]
    tm = _tile(n, 256)
    full = lambda shape: pl.BlockSpec(shape, lambda i: (0,) * len(shape))
    out_specs = [pl.BlockSpec((ATTN_HEADS, tm, LANES), lambda i: (0, i, 0)),
                 pl.BlockSpec((tm, PAGE_COLS), lambda i: (i, 0)),
                 pl.BlockSpec((tm, PAGE_COLS), lambda i: (i, 0)),
                 pl.BlockSpec((tm, 256), lambda i: (i, 0))]
    out_shape = [jax.ShapeDtypeStruct((ATTN_HEADS, n, LANES), q_dtype),
                 jax.ShapeDtypeStruct((n, PAGE_COLS), F32),
                 jax.ShapeDtypeStruct((n, PAGE_COLS), F32),
                 jax.ShapeDtypeStruct((n, 256), F32)]
    if emit_window_tiles:
        out_specs.append(pl.BlockSpec((KV_HEADS, tm // KEY_TILE, 2 * HEAD_DIM, KEY_TILE), lambda i: (0, i, 0, 0)))
        out_shape.append(jax.ShapeDtypeStruct((KV_HEADS, n // KEY_TILE, 2 * HEAD_DIM, KEY_TILE), BF16))
    return pl.pallas_call(
        _prep_kernel,
        grid=(n // tm,),
        in_specs=[pl.BlockSpec((tm, D_ATTN), lambda i: (i, U_Q // D_ATTN)),
                  pl.BlockSpec((tm, KV_COLS), lambda i: (i, 0)),
                  full((256, 256)), full((1, 256)), full((1, 256)), full((1, 256))],
        out_specs=out_specs,
        out_shape=out_shape,
        compiler_params=_cparams("parallel"),
        name="nsa_prep",
    )(u, u, mavg, qg, kg1, kg2)


PAGES_PER_STEP = 8


def _gelu_tanh(x):
    return 0.5 * x * (1.0 + jnp.tanh(0.7978845608028654 * (x + 0.044715 * (x * x * x))))


def _pagepass_kernel(pt_ref, *refs, nchunk, feature_major):
    pp = PAGES_PER_STEP

    pi = lax.broadcasted_iota(jnp.int32, (PAGE_SIZE, PAGE_SIZE), 0)
    pk = lax.broadcasted_iota(jnp.int32, (PAGE_SIZE, PAGE_SIZE), 1)
    perm = jnp.where(pk == CMP_STRIDE * (pi & 7) + (pi >> 3), 1.0, 0.0).astype(BF16)

    def cmp_rows_permuted(page, c):
        if feature_major:
            return lax.dot_general(perm, page[0, c * LANES:(c + 1) * LANES, :].astype(BF16), NT_DIMS,
                                   preferred_element_type=F32)
        return _dot(perm, page[0, :, c * LANES:(c + 1) * LANES].astype(BF16))

    def sel_t(page):
        return page[0, 256:512, :] if feature_major else page[0, :, 256:512].T

    kpages = refs[0:pp]
    vpages = refs[pp:2 * pp]
    pos_ref, w1k_ref, w1v_ref, w2k_ref, w2v_ref, kg_ref = refs[2 * pp:2 * pp + 6]
    kvcmp_ref, kvsel_ref = refs[2 * pp + 6:2 * pp + 8]
    xk_ref, xv_ref, hs_ref = refs[2 * pp + 8:]
    j = pl.program_id(1)
    lo16 = _lo_half(16)

    for i2 in range(pp // 2):
        r0 = pl.multiple_of((j * pp + 2 * i2) * 8, 16)
        for pages, xs_ref in ((kpages, xk_ref), (vpages, xv_ref)):
            for c in range(2):
                pa = cmp_rows_permuted(pages[2 * i2], c)
                pb = cmp_rows_permuted(pages[2 * i2 + 1], c)
                for a in range(8):
                    ev, od = slice(16 * a, 16 * a + 8), slice(16 * a + 8, 16 * a + 16)
                    ec = jnp.concatenate([pa[ev], pb[ev]], axis=0)
                    oc = jnp.concatenate([pa[od], pb[od]], axis=0)
                    xs_ref[2 * c, pl.ds(r0, 16), a * LANES:(a + 1) * LANES] = (
                        jnp.where(lo16, ec, _swap_halves(oc)).astype(BF16))
                    xs_ref[2 * c + 1, pl.ds(r0, 16), a * LANES:(a + 1) * LANES] = (
                        jnp.where(lo16, _swap_halves(ec), oc).astype(BF16))

    per_chunk = SEL_CHUNK // PAGE_SIZE
    for i in range(pp):
        ls = slice((i % per_chunk) * PAGE_SIZE, (i % per_chunk + 1) * PAGE_SIZE)
        _store_kv_t(lambda g: kvsel_ref.at[0, g, i // per_chunk, :, ls], sel_t(kpages[i]), sel_t(vpages[i]))

    @pl.when(j == pl.num_programs(1) - 1)
    def _():
        hs_ref[nchunk:nchunk + 8, :] = jnp.zeros((8, 256), F32)
        pos = pos_ref[...]
        phi = pos.astype(BF16)
        plo = (pos - phi.astype(F32)).astype(BF16)
        bias_k = _dot(phi, w1k_ref[...]) + _dot(plo, w1k_ref[...])
        bias_v = _dot(phi, w1v_ref[...]) + _dot(plo, w1v_ref[...])
        bias_k = bias_k[0:1, 0:128] + bias_k[1:2, 128:256]
        bias_v = bias_v[2:3, 0:128] + bias_v[3:4, 128:256]

        def summarise(xs, w1_ref, bias, w2_ref):
            hs_ref[0:nchunk, :] = _dot(xs, w1_ref[...])
            pre = hs_ref[0:nchunk, 0:128] + hs_ref[1:nchunk + 1, 128:256] + bias
            return _dot(_gelu_tanh(pre).astype(BF16), w2_ref[...])

        for g in range(KV_HEADS):
            ko = summarise(xk_ref[g], w1k_ref, bias_k, w2k_ref)
            vo = summarise(xv_ref[g], w1v_ref, bias_v, w2v_ref)
            ms = jnp.sum(ko * ko, axis=-1, keepdims=True) * (1.0 / HEAD_DIM)
            kvcmp_ref[0, g] = (ko * lax.rsqrt(ms + RMS_EPS) * kg_ref[...] + vo).astype(BF16)


def _pagepass(ptab, kpages, vpages, pos, w1k, w1v, w2k, w2v, kg, nseq, npages):
    pp = PAGES_PER_STEP
    nsteps = npages // pp
    nchunk = npages * PAGE_SIZE // CMP_STRIDE
    feature_major = kpages.shape[1] == PAGE_COLS
    chunks_per_step = pp * PAGE_SIZE // SEL_CHUNK

    def page_spec(i):
        return pl.BlockSpec((1,) + kpages.shape[1:],
                            lambda s, j, pt: (pt[s * npages + j * pp + i], 0, 0))

    full = lambda shape: pl.BlockSpec(shape, lambda s, j, pt: (0,) * len(shape))
    grid_spec = pltpu.PrefetchScalarGridSpec(
        num_scalar_prefetch=1,
        grid=(nseq, nsteps),
        in_specs=([page_spec(i) for i in range(pp)] + [page_spec(i) for i in range(pp)]
                  + [full((8, 1024)), full((1024, 256)), full((1024, 256)),
                     full((128, 128)), full((128, 128)), full((1, 128))]),
        out_specs=[pl.BlockSpec((1, KV_HEADS, nchunk, LANES), lambda s, j, pt: (s, 0, 0, 0)),
                   pl.BlockSpec((1, KV_HEADS, chunks_per_step, 2 * HEAD_DIM, SEL_CHUNK),
                                lambda s, j, pt: (s, 0, j, 0, 0))],
        scratch_shapes=[pltpu.VMEM((KV_HEADS, nchunk, 1024), BF16),
                        pltpu.VMEM((KV_HEADS, nchunk, 1024), BF16),
                        pltpu.VMEM((nchunk + 8, 256), F32)],
    )
    return pl.pallas_call(
        functools.partial(_pagepass_kernel, nchunk=nchunk, feature_major=feature_major),
        grid_spec=grid_spec,
        out_shape=[jax.ShapeDtypeStruct((nseq, KV_HEADS, nchunk, LANES), BF16),
                   jax.ShapeDtypeStruct((nseq, KV_HEADS, npages * PAGE_SIZE // SEL_CHUNK, 2 * HEAD_DIM, SEL_CHUNK),
                                        BF16)],
        compiler_params=_cparams("parallel", "arbitrary"),
        name="pagepass",
    )(ptab, *([kpages] * pp), *([vpages] * pp), pos, w1k, w1v, w2k, w2v, kg)


WIN_TILES = (WINDOW + Q_BLOCK) // KEY_TILE


def _winpack_kernel(wk_ref, wv_ref, kwn_ref, vwn_ref, ksn_ref, vsn_ref, kvw_ref, tail_ref):
    t = kwn_ref.shape[0]
    past_tiles = WINDOW // KEY_TILE
    for c in range(past_tiles):
        ls = slice(c * KEY_TILE, (c + 1) * KEY_TILE)
        _store_kv_t(lambda g: kvw_ref.at[0, g, c], wk_ref[0, :, ls], wv_ref[0, :, ls])
    pad_t = lambda ref, n: jnp.concatenate([ref[...], jnp.zeros((n - t, 256), F32)], axis=0).T
    _store_kv_t(lambda g: kvw_ref.at[0, g, past_tiles], pad_t(kwn_ref, KEY_TILE), pad_t(vwn_ref, KEY_TILE))
    _store_kv_t(lambda g: tail_ref.at[0, g], pad_t(ksn_ref, SEL_CHUNK), pad_t(vsn_ref, SEL_CHUNK))


def _winpack(wk_t, wv_t, kwin, u, new_k, new_v, nseq, t):
    tile = (1, KV_HEADS, 2 * HEAD_DIM, SEL_CHUNK)
    return pl.pallas_call(
        _winpack_kernel,
        grid=(nseq,),
        in_specs=[pl.BlockSpec((1, 256, WINDOW), lambda s: (s, 0, 0)),
                  pl.BlockSpec((1, 256, WINDOW), lambda s: (s, 0, 0)),
                  pl.BlockSpec((t, 256), lambda s: (s, 0)),
                  pl.BlockSpec((t, 256), lambda s: (s, 5)),
                  pl.BlockSpec((t, 256), lambda s: (s, 1)),
                  pl.BlockSpec((t, 256), lambda s: (s, 1))],
        out_specs=[pl.BlockSpec((1, KV_HEADS, WIN_TILES, 2 * HEAD_DIM, KEY_TILE), lambda s: (s, 0, 0, 0, 0)),
                   pl.BlockSpec(tile, lambda s: (s, 0, 0, 0))],
        out_shape=[jax.ShapeDtypeStruct((nseq, KV_HEADS, WIN_TILES, 2 * HEAD_DIM, KEY_TILE), BF16),
                   jax.ShapeDtypeStruct((nseq,) + tile[1:], BF16)],
        compiler_params=_cparams("parallel"),
        name="winpack",
    )(wk_t, wv_t, kwin, u, new_k, new_v)


MASK_BIAS = -2.0 ** 100


def _attn_tables(slopes, n_sel_chunks, n_blocks):
    slope_tab = jnp.broadcast_to(slopes[:, None, None], (ATTN_HEADS, 8, LANES))
    pos = (jnp.arange(n_sel_chunks, dtype=jnp.int32)[:, None, None] * SEL_CHUNK
           + jnp.arange(SEL_CHUNK, dtype=jnp.int32)[None, None, :])
    blk = jnp.arange(n_blocks, dtype=jnp.int32)[None, :, None]
    esel = jnp.where(blk == (pos >> 6), MASK_BIAS, 0.0).astype(BF16)
    lane_src = jnp.arange(LANES)[None, :, None]
    gate_id = jnp.arange(12)[None, None, :]
    g_id = jnp.arange(KV_HEADS)[:, None, None]
    onehot = (lane_src == GATE_LANE0 + g_id * 12 + gate_id).astype(BF16)
    gexp = jnp.broadcast_to(onehot[..., None], (KV_HEADS, LANES, 12, LANES)).reshape(KV_HEADS, LANES, 12 * LANES)
    return slope_tab, esel, gexp


def _top_blocks_unselected(val, valid):
    tq, nselp = val.shape
    if tq < LANES:
        val = jnp.concatenate([val, jnp.zeros((LANES - tq, nselp), F32)], axis=0)
    val_t = jnp.concatenate([val[:, c * LANES:(c + 1) * LANES].T for c in range(nselp // LANES)], axis=0)
    blk_t = lax.broadcasted_iota(jnp.int32, (nselp, LANES), 0).astype(F32)

    def pick_top(_, carry):
        v, sel = carry
        best = jnp.max(v, axis=0, keepdims=True)
        idx = jnp.min(jnp.where(v == best, blk_t, 1e9), axis=0, keepdims=True)
        pick = blk_t == idx
        return jnp.where(pick, -3e38, v), jnp.where(pick, 1.0, sel)

    _, sel_t = lax.fori_loop(0, TOP_BLOCKS, pick_top, (val_t, jnp.zeros((nselp, LANES), F32)))
    sel = jnp.concatenate([sel_t[c * LANES:(c + 1) * LANES].T for c in range(nselp // LANES)], axis=1)
    return jnp.where(valid & (sel[0:tq] > 0.5), 0.0, 1.0)


CHUNKS_PER_WORD = 8


def _list_needed_chunks(unsel, n_chunks, words_ref, list_ref):
    nselp = unsel.shape[1]
    sel_any = jnp.max(1.0 - unsel, axis=0, keepdims=True)
    lane = lax.broadcasted_iota(jnp.int32, (1, nselp), 1)
    chunk_shift = (SEL_CHUNK // SEL_BLOCK).bit_length() - 1
    word_shift = chunk_shift + CHUNKS_PER_WORD.bit_length() - 1
    digit = (lane >> chunk_shift) & (CHUNKS_PER_WORD - 1)
    weight = lax.shift_left(jnp.ones_like(lane), 3 * digit).astype(F32)
    for w in range(words_ref.shape[0]):
        in_word = (lane >> word_shift) == w
        words_ref[w] = jnp.sum(jnp.where(in_word, sel_any * weight, 0.0)).astype(jnp.int32)

    def build(kc, cnt):
        used = (words_ref[kc // CHUNKS_PER_WORD] >> (3 * (kc % CHUNKS_PER_WORD))) & 7
        list_ref[cnt] = kc
        return cnt + jnp.where(used > 0, 1, 0)

    return lax.fori_loop(0, n_chunks, build, 0)


def _attn_kernel(q_ref, misc_ref, kvc_ref, kvs_ref, kvw_ref, slope_ref, esel_ref, gexp_ref,
                 o_ref, m_ref, l_ref, acc_ref, words_ref, list_ref, *, tq, nchunk):
    q0 = 0
    wpos0 = 0
    nselp = LANES
    qs = pl.program_id(2) * tq
    m_rows = Q_PER_KV * tq
    q = q_ref[...].reshape(m_rows, LANES)
    qb = q.astype(BF16)
    tpos = q0 + qs + lax.broadcasted_iota(jnp.int32, (tq, 1), 0)

    def per_head(x):
        return jnp.concatenate([x] * Q_PER_KV, axis=0)

    def add_by_head(s, fn):
        return jnp.concatenate([s[r * tq:(r + 1) * tq] + fn(r) for r in range(Q_PER_KV)], axis=0)

    def slope(r):
        return slope_ref[r, 0:1, 0:1]

    kvc = kvc_ref[0, 0]
    cmp_end = lax.broadcasted_iota(jnp.int32, (tq, nchunk), 1) * CMP_STRIDE + (2 * CMP_STRIDE - 1)
    cmp_mask = per_head(tpos >= cmp_end)
    cmp_pos = (lax.broadcasted_iota(jnp.int32, (1, nchunk), 1) * CMP_STRIDE + (2 * CMP_STRIDE - 1)).astype(F32)
    s = lax.dot_general(qb, kvc, NT_DIMS, preferred_element_type=F32)
    s = jnp.where(cmp_mask, add_by_head(s, lambda r: slope(r) * cmp_pos), NEG_INF)
    p = jnp.exp(s - jnp.max(s, axis=-1, keepdims=True)) * jnp.where(cmp_mask, 1.0, 0.0)
    p = p / jnp.maximum(jnp.sum(p, axis=-1, keepdims=True), TINY)
    o_cmp = _dot(p.astype(BF16), kvc)
    psum = p[0:tq] + p[tq:2 * tq] + p[2 * tq:3 * tq] + p[3 * tq:4 * tq]

    pool = jnp.where((lax.broadcasted_iota(jnp.int32, (nchunk, nselp), 0) >> 2)
                     == lax.broadcasted_iota(jnp.int32, (nchunk, nselp), 1), 1.0, 0.0).astype(BF16)
    hi, mid, lo = _split3(psum)
    imp = _dot(hi, pool) + _dot(mid, pool) + _dot(lo, pool)
    blk = lax.broadcasted_iota(jnp.int32, (tq, nselp), 1)
    cur = tpos >> 6
    forced = (blk == 0) | (blk == cur) | (blk == cur - 1)
    valid = blk <= cur
    val = jnp.where(valid, imp + jnp.where(forced, FORCE_BONUS, 0.0), NEG_INF)
    unsel = _top_blocks_unselected(val, valid)
    lhs = jnp.where(_lo_half(m_rows), q.astype(F32), per_head(_swap_halves(unsel))).astype(BF16)

    m_ref[...] = jnp.full_like(m_ref, NEG_INF)
    l_ref[...] = jnp.zeros_like(l_ref)
    acc_ref[...] = jnp.zeros_like(acc_ref)

    def sel_chunk(stream, kv_t, e_c, k0, causal_bias):
        n = kv_t.shape[1]
        s = _dot(lhs, jnp.concatenate([kv_t[0:HEAD_DIM], e_c], axis=0))
        pos = (k0 + lax.broadcasted_iota(jnp.int32, (1, n), 1)).astype(F32)
        if causal_bias is None:
            s = add_by_head(s, lambda r: slope(r) * pos)
        else:
            s = add_by_head(s, lambda r: slope(r) * pos + causal_bias)
        m_old = m_ref[stream]
        m_new = jnp.maximum(m_old, jnp.max(s, axis=-1, keepdims=True))
        alpha = jnp.exp(m_old - m_new)
        p = [jnp.exp(s[:, c * LANES:(c + 1) * LANES] - m_new) for c in range(n // LANES)]
        part = p[0]
        for c in range(1, n // LANES):
            part = part + p[c]
        l_ref[stream] = alpha * l_ref[stream] + part
        acc_ref[stream] = alpha * acc_ref[stream] + lax.dot_general(
            jnp.concatenate(p, axis=1).astype(BF16), kv_t, NT_DIMS, preferred_element_type=F32)
        m_ref[stream] = m_new

    def past_chunk(stream, slot):
        kc = list_ref[slot]
        sel_chunk(stream, kvs_ref[0, 0, kc], esel_ref[kc], kc * SEL_CHUNK, None)

    def pair(i, carry):
        for stream in range(2):
            past_chunk(stream, 2 * i + stream)
        return carry

    n_full = qs // SEL_CHUNK
    n_need = _list_needed_chunks(unsel, n_full, words_ref, list_ref)
    lax.fori_loop(0, n_need // 2, pair, 0)

    @pl.when(n_need % 2 == 1)
    def _():
        past_chunk(0, n_need - 1)

    key_j = lax.broadcasted_iota(jnp.int32, (tq, SEL_CHUNK), 1)
    causal = jnp.where(tpos >= n_full * SEL_CHUNK + key_j, 0.0, NEG_INF)
    sel_chunk(1, kvs_ref[0, 0, n_full], esel_ref[n_full], n_full * SEL_CHUNK, causal)
    m = jnp.maximum(m_ref[0], m_ref[1])
    w0 = jnp.exp(m_ref[0] - m)
    w1 = jnp.exp(m_ref[1] - m)
    l_sel = jnp.sum(w0 * l_ref[0] + w1 * l_ref[1], axis=-1, keepdims=True)
    o_sel = (w0 * acc_ref[0] + w1 * acc_ref[1]) / jnp.maximum(l_sel, TINY)

    tile0 = jnp.maximum(q0 + qs - WINDOW - wpos0, 0) // KEY_TILE
    tiles = [kvw_ref[0, 0, tile0 + j] for j in range(WIN_TILES)]
    key_b = lax.broadcasted_iota(jnp.int32, (1, KEY_TILE), 1)
    s = []
    for j in range(WIN_TILES):
        pos = wpos0 + (tile0 + j) * KEY_TILE + key_b
        dist = tpos - pos
        bias = jnp.where((dist >= 0) & (dist <= WINDOW), 0.0, NEG_INF)
        s.append(add_by_head(_dot(qb, tiles[j]), lambda r: slope(r) * pos.astype(F32) + bias))
    m = s[0]
    for j in range(1, WIN_TILES):
        m = jnp.maximum(m, s[j])
    m = jnp.max(m, axis=-1, keepdims=True)
    p = [jnp.exp(sj - m) for sj in s]
    part = p[0]
    for j in range(1, WIN_TILES):
        part = part + p[j]
    o_win = lax.dot_general(p[0].astype(BF16), tiles[0], NT_DIMS, preferred_element_type=F32)
    for j in range(1, WIN_TILES):
        o_win = o_win + lax.dot_general(p[j].astype(BF16), tiles[j], NT_DIMS, preferred_element_type=F32)
    o_win = o_win / jnp.maximum(jnp.sum(part, axis=-1, keepdims=True), TINY)

    hi, mid, lo = _split3(jax.nn.sigmoid(misc_ref[...]))
    gexp = gexp_ref[0]
    gates = _dot(hi, gexp) + _dot(mid, gexp) + _dot(lo, gexp)
    comb = []
    for r in range(Q_PER_KV):
        rs = slice(r * tq, (r + 1) * tq)
        gate = [gates[:, (3 * r + c) * LANES:(3 * r + c + 1) * LANES] for c in range(3)]
        comb.append(gate[0] * o_cmp[rs] + gate[1] * o_sel[rs] + gate[2] * o_win[rs])
    lo_half = _lo_half(tq)
    o_ref[:, 0:LANES] = jnp.where(lo_half, _swap_halves(comb[0]), comb[1])
    o_ref[:, LANES:2 * LANES] = jnp.where(lo_half, _swap_halves(comb[2]), comb[3])


def _attn(tables, qhm, u, kvcmp, kvsel, kvwin, nseq, t):
    slope_tab, esel, gexp = tables
    tq = Q_BLOCK
    nqb = t // tq
    nchunk = kvcmp.shape[2]
    sel_chunks = kvsel.shape[2]
    win_tiles = kvwin.shape[2]
    m_rows = Q_PER_KV * tq
    assert esel.shape[1] == HEAD_DIM
    in_specs = [pl.BlockSpec((Q_PER_KV, tq, LANES), lambda s, g, i: (g, s * nqb + i, 0)),
                pl.BlockSpec((tq, LANES), lambda s, g, i: (s * nqb + i, U_MISC // LANES)),
                pl.BlockSpec((1, 1, nchunk, LANES), lambda s, g, i: (s, g, 0, 0)),
                pl.BlockSpec((1, 1, sel_chunks, 2 * HEAD_DIM, SEL_CHUNK), lambda s, g, i: (s, g, 0, 0, 0)),
                pl.BlockSpec((1, 1, win_tiles, 2 * HEAD_DIM, KEY_TILE), lambda s, g, i: (g, s, 0, 0, 0)),
                pl.BlockSpec((Q_PER_KV, 8, LANES), lambda s, g, i: (g, 0, 0)),
                pl.BlockSpec(esel.shape, lambda s, g, i: (0, 0, 0)),
                pl.BlockSpec((1, LANES, 12 * LANES), lambda s, g, i: (g, 0, 0))]
    args = [qhm, u, kvcmp, kvsel, kvwin, slope_tab, esel, gexp]
    stream_state = pltpu.VMEM((2, m_rows, LANES), F32)
    return pl.pallas_call(
        functools.partial(_attn_kernel, tq=tq, nchunk=nchunk),
        grid=(nseq, KV_HEADS, nqb),
        in_specs=in_specs,
        out_specs=pl.BlockSpec((tq, 2 * LANES), lambda s, g, i: (s * nqb + i, g)),
        out_shape=jax.ShapeDtypeStruct((nseq * t, D_ATTN), F32),
        scratch_shapes=[stream_state, stream_state, stream_state,
                        pltpu.SMEM((-(-sel_chunks // CHUNKS_PER_WORD),), jnp.int32),
                        pltpu.SMEM((sel_chunks,), jnp.int32)],
        compiler_params=_cparams("parallel", "parallel", "arbitrary"),
        name="nsa_attn",
    )(*args)


def _attn_sample_kernel(q_ref, misc_ref, kvc_ref, kvs_ref, kvw_ref, tail_ref, slope_ref, esel_ref, gexp_ref,
                        o_ref, m_ref, l_ref, acc_ref, words_ref, list_ref, *, tq, q0, nchunk, nselp, lmain):
    n_rows = ATTN_HEADS * tq
    grp_rows = Q_PER_KV * tq
    wide = KV_HEADS * LANES
    q = q_ref[...].reshape(n_rows, LANES)
    qb = q.astype(BF16)
    tpos = q0 + lax.broadcasted_iota(jnp.int32, (tq, 1), 0)
    slope_col = jnp.concatenate([jnp.broadcast_to(slope_ref[h, 0:1, 0:1], (tq, 1)) for h in range(ATTN_HEADS)],
                                axis=0)
    grp_shift = grp_rows.bit_length() - 1
    own = ((lax.broadcasted_iota(jnp.int32, (n_rows, wide), 1) >> 7)
           == (lax.broadcasted_iota(jnp.int32, (n_rows, wide), 0) >> grp_shift))
    q_diag = jnp.where(own, jnp.concatenate([q.astype(F32)] * KV_HEADS, axis=1), 0.0).astype(BF16)
    row_grp = lax.broadcasted_iota(jnp.int32, (n_rows, LANES), 0) >> grp_shift

    def per_rows(x, copies):
        return jnp.concatenate([x] * copies, axis=0)

    def own_block(x):
        out = x[:, 0:LANES]
        for g in range(1, KV_HEADS):
            out = jnp.where(row_grp == g, x[:, g * LANES:(g + 1) * LANES], out)
        return out

    cmp_end = lax.broadcasted_iota(jnp.int32, (tq, nchunk), 1) * CMP_STRIDE + (2 * CMP_STRIDE - 1)
    cmp_mask = per_rows(tpos >= cmp_end, ATTN_HEADS)
    cmp_pos = (lax.broadcasted_iota(jnp.int32, (1, nchunk), 1) * CMP_STRIDE + (2 * CMP_STRIDE - 1)).astype(F32)
    grp = lambda x, g: x[g * grp_rows:(g + 1) * grp_rows]
    s = jnp.concatenate([lax.dot_general(grp(qb, g), kvc_ref[0, g], NT_DIMS, preferred_element_type=F32)
                         for g in range(KV_HEADS)], axis=0)
    s = jnp.where(cmp_mask, s + slope_col * cmp_pos, NEG_INF)
    p = jnp.exp(s - jnp.max(s, axis=-1, keepdims=True)) * jnp.where(cmp_mask, 1.0, 0.0)
    p = p / jnp.maximum(jnp.sum(p, axis=-1, keepdims=True), TINY)
    o_cmp = jnp.concatenate([_dot(grp(p, g).astype(BF16), kvc_ref[0, g]) for g in range(KV_HEADS)], axis=0)
    head_p = lambda h: p[h * tq:(h + 1) * tq]
    psum = jnp.concatenate(
        [head_p(4 * g) + head_p(4 * g + 1) + head_p(4 * g + 2) + head_p(4 * g + 3) for g in range(KV_HEADS)],
        axis=0)

    pool = jnp.where((lax.broadcasted_iota(jnp.int32, (nchunk, nselp), 0) >> 2)
                     == lax.broadcasted_iota(jnp.int32, (nchunk, nselp), 1), 1.0, 0.0).astype(BF16)
    hi, mid, lo = _split3(psum)
    imp = _dot(hi, pool) + _dot(mid, pool) + _dot(lo, pool)
    blk = lax.broadcasted_iota(jnp.int32, (grp_rows, nselp), 1)
    cur = per_rows(tpos, KV_HEADS) >> 6
    forced = (blk == 0) | (blk == cur) | (blk == cur - 1)
    valid = blk <= cur
    val = jnp.where(valid, imp + jnp.where(forced, FORCE_BONUS, 0.0), NEG_INF)
    unsel = _top_blocks_unselected(val, valid)
    unsel_rows = jnp.concatenate([unsel[(h // Q_PER_KV) * tq:(h // Q_PER_KV + 1) * tq]
                                  for h in range(ATTN_HEADS)], axis=0).astype(BF16)

    m_ref[...] = jnp.full_like(m_ref, NEG_INF)
    l_ref[...] = jnp.zeros_like(l_ref)
    acc_ref[...] = jnp.zeros_like(acc_ref)

    def sel_chunk(stream, kv_t, e_c, k0, causal_bias):
        n = kv_t.shape[1]
        pos = (k0 + lax.broadcasted_iota(jnp.int32, (1, n), 1)).astype(F32)
        s = _dot(q_diag, kv_t) + _dot(unsel_rows, e_c) + slope_col * pos
        if causal_bias is not None:
            s = s + causal_bias
        m_old = m_ref[stream]
        m_new = jnp.maximum(m_old, jnp.max(s, axis=-1, keepdims=True))
        alpha = jnp.exp(m_old - m_new)
        p = [jnp.exp(s[:, c * LANES:(c + 1) * LANES] - m_new) for c in range(n // LANES)]
        part = p[0]
        for c in range(1, n // LANES):
            part = part + p[c]
        l_ref[stream] = alpha * l_ref[stream] + part
        acc_ref[stream] = jnp.concatenate([alpha] * KV_HEADS, axis=1) * acc_ref[stream] + lax.dot_general(
            jnp.concatenate(p, axis=1).astype(BF16), kv_t, NT_DIMS, preferred_element_type=F32)
        m_ref[stream] = m_new

    def past_chunk(stream, slot):
        kc = list_ref[slot]
        sel_chunk(stream, kvs_ref[0, :, kc].reshape(wide, SEL_CHUNK), esel_ref[kc], kc * SEL_CHUNK, None)

    def pair(i, carry):
        for stream in range(2):
            past_chunk(stream, 2 * i + stream)
        return carry

    n_need = _list_needed_chunks(unsel, lmain // SEL_CHUNK, words_ref, list_ref)
    lax.fori_loop(0, n_need // 2, pair, 0)

    @pl.when(n_need % 2 == 1)
    def _():
        past_chunk(0, n_need - 1)

    key_j = lax.broadcasted_iota(jnp.int32, (tq, SEL_CHUNK), 1)
    causal = per_rows(jnp.where(tpos >= lmain + key_j, 0.0, NEG_INF), ATTN_HEADS)
    sel_chunk(1, tail_ref[0].reshape(wide, SEL_CHUNK), esel_ref[lmain // SEL_CHUNK], lmain, causal)
    m = jnp.maximum(m_ref[0], m_ref[1])
    w0 = jnp.exp(m_ref[0] - m)
    w1 = jnp.exp(m_ref[1] - m)
    l_sel = jnp.sum(w0 * l_ref[0] + w1 * l_ref[1], axis=-1, keepdims=True)
    o_sel = (w0 * own_block(acc_ref[0]) + w1 * own_block(acc_ref[1])) / jnp.maximum(l_sel, TINY)

    key_b = lax.broadcasted_iota(jnp.int32, (1, KEY_TILE), 1)
    tiles = [kvw_ref[0, :, j].reshape(wide, KEY_TILE) for j in range(WIN_TILES)]
    s = []
    for j in range(WIN_TILES):
        pos = q0 - WINDOW + j * KEY_TILE + key_b
        dist = tpos - pos
        bias = per_rows(jnp.where((dist >= 0) & (dist <= WINDOW), 0.0, NEG_INF), ATTN_HEADS)
        s.append(_dot(q_diag, tiles[j]) + slope_col * pos.astype(F32) + bias)
    m = s[0]
    for j in range(1, WIN_TILES):
        m = jnp.maximum(m, s[j])
    m = jnp.max(m, axis=-1, keepdims=True)
    p = [jnp.exp(sj - m) for sj in s]
    part = p[0]
    for j in range(1, WIN_TILES):
        part = part + p[j]
    o_win = lax.dot_general(p[0].astype(BF16), tiles[0], NT_DIMS, preferred_element_type=F32)
    for j in range(1, WIN_TILES):
        o_win = o_win + lax.dot_general(p[j].astype(BF16), tiles[j], NT_DIMS, preferred_element_type=F32)
    o_win = own_block(o_win) / jnp.maximum(jnp.sum(part, axis=-1, keepdims=True), TINY)

    hi, mid, lo = _split3(jax.nn.sigmoid(misc_ref[...]))
    comb = []
    for g in range(KV_HEADS):
        gates = _dot(hi, gexp_ref[g]) + _dot(mid, gexp_ref[g]) + _dot(lo, gexp_ref[g])
        for r in range(Q_PER_KV):
            hs = slice((g * Q_PER_KV + r) * tq, (g * Q_PER_KV + r + 1) * tq)
            gate = [gates[:, (3 * r + c) * LANES:(3 * r + c + 1) * LANES] for c in range(3)]
            comb.append(gate[0] * o_cmp[hs] + gate[1] * o_sel[hs] + gate[2] * o_win[hs])
    lo_half = _lo_half(tq)
    for i in range(ATTN_HEADS // 2):
        o_ref[:, i * LANES:(i + 1) * LANES] = jnp.where(lo_half, _swap_halves(comb[2 * i]), comb[2 * i + 1])


def _attn_sample(tables, qhm, u, kvcmp, kvsel, kvwin, tail, nseq, t, q0):
    slope_tab, esel, gexp = tables
    nchunk = kvcmp.shape[2]
    sel_chunks = kvsel.shape[2]
    n_rows = ATTN_HEADS * t
    full = lambda a: pl.BlockSpec(a.shape, lambda s: (0,) * a.ndim)
    per_seq = lambda a: pl.BlockSpec((1,) + a.shape[1:], lambda s: (s,) + (0,) * (a.ndim - 1))
    wide_state = pltpu.VMEM((2, n_rows, KV_HEADS * LANES), F32)
    lane_state = pltpu.VMEM((2, n_rows, LANES), F32)
    return pl.pallas_call(
        functools.partial(_attn_sample_kernel, tq=t, q0=q0, nchunk=nchunk, nselp=esel.shape[1],
                          lmain=sel_chunks * SEL_CHUNK),
        grid=(nseq,),
        in_specs=[pl.BlockSpec((ATTN_HEADS, t, LANES), lambda s: (0, s, 0)),
                  pl.BlockSpec((t, LANES), lambda s: (s, U_MISC // LANES)),
                  per_seq(kvcmp), per_seq(kvsel), per_seq(kvwin), per_seq(tail),
                  full(slope_tab), full(esel), full(gexp)],
        out_specs=pl.BlockSpec((t, D_ATTN), lambda s: (s, 0)),
        out_shape=jax.ShapeDtypeStruct((nseq * t, D_ATTN), F32),
        scratch_shapes=[lane_state, lane_state, wide_state,
                        pltpu.SMEM((-(-sel_chunks // CHUNKS_PER_WORD),), jnp.int32),
                        pltpu.SMEM((sel_chunks,), jnp.int32)],
        compiler_params=_cparams("parallel"),
        name="nsa_attn_sample",
    )(qhm, u, kvcmp, kvsel, kvwin, tail, slope_tab, esel, gexp)


def _layer_params(l, norm_mix, w_in, conv_a_w, conv_b_w, conv_b_bias, dt_bias, a_log, d_skip,
                  q_norm, k_norm, cmp_pos, cmp_w1, cmp_w2, norm_out, w_out, norm_ffn, w_gate, w_up, w_down):
    w = w_in[l]
    w_perm = jnp.concatenate(
        [w[:, 4104:5640], w[:, 0:1536], w[:, 2048:3072], w[:, 3080:4104], w[:, 1536:2048],
         w[:, 3072:3080], w[:, 5640:5688], jnp.zeros((D_MODEL, U_COLS - 5688), F32)], axis=1).astype(BF16)
    pad8 = lambda v: jnp.pad(v, (0, LANES - v.shape[0])).reshape(1, LANES)
    head_id = jnp.arange(256) // HEAD_DIM
    pos = cmp_pos[l].reshape(4, 1024)
    w1 = cmp_w1[l]
    w2 = cmp_w2[l]
    zeros_w2 = jnp.zeros((CMP_HIDDEN, HEAD_DIM), F32)
    return dict(
        g_mix=norm_mix[l].reshape(1, D_MODEL), w_in=w_perm,
        conv_a_w=conv_a_w[l], conv_b_w=conv_b_w[l], conv_b_bias=conv_b_bias[l].reshape(1, SSM_CONV_DIM),
        dt_bias=pad8(dt_bias[l]), a_log=pad8(a_log[l]),
        d_skip=jnp.repeat(d_skip[l], HEAD_DIM).reshape(1, D_SSM),
        mavg=jnp.where(head_id[:, None] == head_id[None, :], 1.0 / HEAD_DIM, 0.0).astype(BF16),
        q_gain=jnp.tile(q_norm[l], 4).reshape(1, 256),
        k_gain1=jnp.tile(k_norm[l, 1], 4).reshape(1, 256),
        k_gain2=jnp.tile(k_norm[l, 2], 4).reshape(1, 256),
        k_gain0=pad8(k_norm[l, 0]),
        pos=jnp.pad(pos, ((0, 4), (0, 0))),
        w1k=jnp.concatenate([w1[0, 0:1024], w1[0, 1024:2048]], axis=1).astype(BF16),
        w1v=jnp.concatenate([w1[1, 0:1024], w1[1, 1024:2048]], axis=1).astype(BF16),
        w2k=jnp.concatenate([w2[0], zeros_w2], axis=1).astype(BF16),
        w2v=jnp.concatenate([zeros_w2, w2[1]], axis=1).astype(BF16),
        g_out=norm_out[l].reshape(1, D_MODEL), w_out=w_out[l].astype(BF16),
        g_ffn=norm_ffn[l].reshape(1, D_MODEL),
        w_gate=w_gate[l].astype(BF16), w_up=w_up[l].astype(BF16), w_down=w_down[l].astype(BF16),
    )


def _group_tables(slopes, t, q0):
    lmain = q0 if q0 else t
    n_sel_chunks = lmain // SEL_CHUNK + (1 if q0 else 0)
    n_blocks = n_sel_chunks * SEL_CHUNK // SEL_BLOCK
    if q0:
        n_blocks = -(-n_blocks // LANES) * LANES
    else:
        assert n_blocks <= HEAD_DIM
        n_blocks = HEAD_DIM
    return _attn_tables(slopes, n_sel_chunks, n_blocks)


def _layer(x, nseq, t, q0, p, tables, conv_a_prefix, conv_b_prefix, ssm_h0, past):
    u = _in_proj(x, p["g_mix"], p["w_in"])
    ya, new_conv_a = _mixer_a(u, conv_a_prefix, p["conv_a_w"], nseq, t)
    yb, new_conv_b, new_ssm = _ssd(u, conv_b_prefix, ssm_h0, p["conv_b_w"], p["conv_b_bias"],
                                   p["dt_bias"], p["a_log"], p["d_skip"], nseq, t)
    prep_out = _prep(u, p["mavg"], p["q_gain"], p["k_gain1"], p["k_gain2"],
                     BF16 if past is None else F32, past is None)
    qhm, new_k, new_v, kwin = prep_out[:4]
    vwin = u[:, U_KV + 1280:U_KV + 1536]
    cmp_args = (p["pos"], p["w1k"], p["w1v"], p["w2k"], p["w2v"], p["k_gain0"])
    if past is None:
        npages = t // PAGE_SIZE
        ptab = jnp.arange(nseq * npages, dtype=jnp.int32)
        kvcmp, kvsel = _pagepass(ptab, new_k.reshape(-1, PAGE_SIZE, PAGE_COLS),
                                 new_v.reshape(-1, PAGE_SIZE, PAGE_COLS), *cmp_args, nseq, npages)
        kvwin_arr = prep_out[4].reshape(KV_HEADS, nseq, t // KEY_TILE, 2 * HEAD_DIM, KEY_TILE)
        yc = _attn(tables, qhm, u, kvcmp, kvsel, kvwin_arr, nseq, t)
        new_win_k = kwin.reshape(nseq, t, 256)[:, t - WINDOW:]
        new_win_v = vwin.reshape(nseq, t, 256)[:, t - WINDOW:]
    else:
        ptab, k_pages, v_pages, npages, win_k_t, win_v_t, win_k, win_v = past
        kvcmp, kvsel = _pagepass(ptab, k_pages, v_pages, *cmp_args, nseq, npages)
        kvwin_arr, tail = _winpack(win_k_t, win_v_t, kwin, u, new_k, new_v, nseq, t)
        yc = _attn_sample(tables, qhm, u, kvcmp, kvsel, kvwin_arr, tail, nseq, t, q0)
        new_win_k = jnp.concatenate([win_k[:, t:], kwin.reshape(nseq, t, 256)], axis=1)
        new_win_v = jnp.concatenate([win_v[:, t:], vwin.reshape(nseq, t, 256)], axis=1)
    x = _out_proj(x, ya, yb, yc, p["g_out"], p["w_out"])
    x = _ffn(x, p["g_ffn"], p["w_gate"], p["w_up"], p["w_down"])
    hd = (KV_HEADS, HEAD_DIM)
    state = (new_k.reshape(nseq, t, 2, *hd), new_v.reshape(nseq, t, 2, *hd),
             new_win_k.reshape(nseq, WINDOW, *hd), new_win_v.reshape(nseq, WINDOW, *hd),
             new_conv_a, new_conv_b, new_ssm)
    return x, state


def kernel(x_prompt, x_sample, cache_k, cache_v, cache_win_k, cache_win_v, state_conv_a, state_conv_b,
           state_ssm, page_table, norm_mix, w_in, conv_a_w, conv_b_w, conv_b_bias, dt_bias, a_log, d_skip,
           q_norm, k_norm, cmp_pos, cmp_w1, cmp_w2, norm_out, w_out, norm_ffn, w_gate, w_up, w_down):
    bsz, t_prompt, _ = x_prompt.shape
    dec_b, t_dec, _ = x_sample.shape
    depth, n_pool = cache_k.shape[0], cache_k.shape[1]
    npages = page_table.shape[1]
    past_len = npages * PAGE_SIZE
    assert cache_win_k.shape[2] == WINDOW and t_prompt >= WINDOW + Q_BLOCK and t_prompt % Q_BLOCK == 0
    assert t_dec == 8 and npages % PAGES_PER_STEP == 0 and (t_prompt // PAGE_SIZE) % PAGES_PER_STEP == 0

    slopes = jnp.exp2(-8.0 * jnp.arange(1, ATTN_HEADS + 1, dtype=F32) / ATTN_HEADS)
    prompt_tables = _group_tables(slopes, t_prompt, 0)
    sample_tables = _group_tables(slopes, t_dec, past_len)
    to_pages_t = lambda c: jnp.transpose(c, (0, 1, 3, 4, 5, 2)).reshape(depth * n_pool, PAGE_COLS, PAGE_SIZE)
    to_win_t = lambda w: jnp.transpose(w, (0, 2, 3, 1)).reshape(dec_b, 256, WINDOW)
    k_pages = to_pages_t(cache_k)
    v_pages = to_pages_t(cache_v)
    hp = x_prompt.reshape(bsz * t_prompt, D_MODEL)
    hs = x_sample.reshape(dec_b * t_dec, D_MODEL)
    zeros = lambda *shape: jnp.zeros(shape, F32)
    prompt_states, sample_states = [], []
    for l in range(depth):
        p = _layer_params(l, norm_mix, w_in, conv_a_w, conv_b_w, conv_b_bias, dt_bias, a_log, d_skip,
                          q_norm, k_norm, cmp_pos, cmp_w1, cmp_w2, norm_out, w_out, norm_ffn,
                          w_gate, w_up, w_down)
        hp, st_p = _layer(hp, bsz, t_prompt, 0, p, prompt_tables,
                          zeros(bsz, CONV_A_WIDTH - 1, D_CONV), zeros(bsz, SSM_CONV_WIDTH - 1, SSM_CONV_DIM),
                          zeros(bsz, SSM_HEADS, HEAD_DIM, SSM_STATE), None)
        ptab = (page_table + l * n_pool).reshape(-1).astype(jnp.int32)
        past = (ptab, k_pages, v_pages, npages, to_win_t(cache_win_k[l]), to_win_t(cache_win_v[l]),
                cache_win_k[l].reshape(dec_b, WINDOW, 256), cache_win_v[l].reshape(dec_b, WINDOW, 256))
        hs, st_s = _layer(hs, dec_b, t_dec, past_len, p, sample_tables,
                          state_conv_a[l], state_conv_b[l], state_ssm[l], past)
        prompt_states.append(st_p)
        sample_states.append(st_s)

    stack = lambda states, j: jnp.stack([s[j] for s in states], axis=0)
    return ((hp.reshape(bsz, t_prompt, D_MODEL), hs.reshape(dec_b, t_dec, D_MODEL))
            + tuple(stack(prompt_states, j) for j in range(7))
            + tuple(stack(sample_states, j) for j in range(7)))
```

```python
import functools

import jax
import jax.numpy as jnp
from jax import lax
from jax.experimental import pallas as pl
from jax.experimental.pallas import tpu as pltpu

F32 = jnp.float32
BF16 = jnp.bfloat16

D_MODEL = 2048
HEAD_DIM = 64
D_CONV = 512
D_SSM = 512
D_ATTN = 1024
CONV_A_WIDTH = 3
SSM_HEADS = 8
SSM_STATE = 128
SSM_CONV_WIDTH = 4
SSM_CHUNK = 128
SSM_CONV_DIM = 1024
ATTN_HEADS = 16
KV_HEADS = 4
Q_PER_KV = 4
KV_COLS = 1536
CMP_STRIDE = 16
CMP_HIDDEN = 128
SEL_BLOCK = 64
TOP_BLOCKS = 16
WINDOW = 512
Q_BLOCK = 128
PAGE_SIZE = 128
PAGE_COLS = 512
D_FF = 5632
RMS_EPS = 1e-6
NEG_INF = -1e30
TINY = 1e-30
FORCE_BONUS = 1e3
ATTN_SCALE = HEAD_DIM ** -0.5

U_KV = 0
U_A = 1536
U_XBC = 3072
U_Q = 4096
U_Z = 5120
U_MISC = 5632
U_COLS = 5760
GATE_LANE0 = 8

LANES = 128
KEY_TILE = 128
SEL_CHUNK = 256
VMEM_LIMIT_BYTES = 56 * 2 ** 20

NT_DIMS = (((1,), (1,)), ((), ()))


def _cparams(*sem):
    return pltpu.CompilerParams(dimension_semantics=sem, vmem_limit_bytes=VMEM_LIMIT_BYTES)


def _tile(n, pref):
    t = min(n, pref)
    while n % t:
        t //= 2
    return t


def _dot(a, b):
    return jnp.dot(a, b, preferred_element_type=F32)


def _split3(x):
    hi = x.astype(BF16)
    r1 = x - hi.astype(F32)
    mid = r1.astype(BF16)
    lo = (r1 - mid.astype(F32)).astype(BF16)
    return hi, mid, lo


def _lo_half(rows):
    return lax.broadcasted_iota(jnp.int32, (rows, LANES), 1) < HEAD_DIM


def _swap_halves(x):
    return pltpu.roll(x, HEAD_DIM, 1)


def _store_kv_t(dst, k_t, v_t):
    for g in range(KV_HEADS):
        view = dst(g)
        view[0:HEAD_DIM, :] = k_t[g * HEAD_DIM:(g + 1) * HEAD_DIM].astype(BF16)
        view[HEAD_DIM:2 * HEAD_DIM, :] = v_t[g * HEAD_DIM:(g + 1) * HEAD_DIM].astype(BF16)


def _in_proj_kernel(x_ref, g_ref, w_ref, o_ref, xn_ref):
    @pl.when(pl.program_id(1) == 0)
    def _():
        x = x_ref[...]
        ms = jnp.mean(x * x, axis=-1, keepdims=True)
        xn_ref[...] = (x * lax.rsqrt(ms + RMS_EPS) * g_ref[...]).astype(BF16)

    o_ref[...] = _dot(xn_ref[...], w_ref[...])


def _in_proj(x, gain, w):
    n = x.shape[0]
    tm = _tile(n, 1024)
    tn = 1152
    return pl.pallas_call(
        _in_proj_kernel,
        grid=(n // tm, U_COLS // tn),
        in_specs=[pl.BlockSpec((tm, D_MODEL), lambda i, j: (i, 0)),
                  pl.BlockSpec((1, D_MODEL), lambda i, j: (0, 0)),
                  pl.BlockSpec((D_MODEL, tn), lambda i, j: (0, j))],
        out_specs=pl.BlockSpec((tm, tn), lambda i, j: (i, j)),
        out_shape=jax.ShapeDtypeStruct((n, U_COLS), F32),
        scratch_shapes=[pltpu.VMEM((tm, D_MODEL), BF16)],
        compiler_params=_cparams("parallel", "arbitrary"),
        name="in_proj",
    )(x, gain, w)


def _out_proj_kernel(x_ref, ya_ref, yb_ref, yc_ref, g_ref, w_ref, o_ref, mg_ref):
    @pl.when(pl.program_id(1) == 0)
    def _():
        def nrm(y, g):
            ms = jnp.mean(y * y, axis=-1, keepdims=True)
            return (y * lax.rsqrt(ms + RMS_EPS) * g).astype(BF16)

        mg_ref[:, 0:512] = nrm(ya_ref[...], g_ref[:, 0:512])
        mg_ref[:, 512:1024] = nrm(yb_ref[...], g_ref[:, 512:1024])
        mg_ref[:, 1024:2048] = nrm(yc_ref[...], g_ref[:, 1024:2048])

    o_ref[...] = x_ref[...] + _dot(mg_ref[...], w_ref[...])


def _out_proj(x, ya, yb, yc, gain, w):
    n = x.shape[0]
    tm = _tile(n, 512)
    tn = 1024
    return pl.pallas_call(
        _out_proj_kernel,
        grid=(n // tm, D_MODEL // tn),
        in_specs=[pl.BlockSpec((tm, tn), lambda i, j: (i, j)),
                  pl.BlockSpec((tm, D_CONV), lambda i, j: (i, 0)),
                  pl.BlockSpec((tm, D_SSM), lambda i, j: (i, 0)),
                  pl.BlockSpec((tm, D_ATTN), lambda i, j: (i, 0)),
                  pl.BlockSpec((1, D_MODEL), lambda i, j: (0, 0)),
                  pl.BlockSpec((D_MODEL, tn), lambda i, j: (0, j))],
        out_specs=pl.BlockSpec((tm, tn), lambda i, j: (i, j)),
        out_shape=jax.ShapeDtypeStruct((n, D_MODEL), F32),
        scratch_shapes=[pltpu.VMEM((tm, D_MODEL), BF16)],
        compiler_params=_cparams("parallel", "arbitrary"),
        name="out_proj",
    )(x, ya, yb, yc, gain, w)


FFN_CHAINS = 2


def _ffn_kernel(x_ref, g_ref, wg_ref, wu_ref, wd_ref, o_ref, h_ref, acc_ref):
    f = pl.program_id(1)

    @pl.when(f == 0)
    def _():
        x = x_ref[...]
        ms = jnp.mean(x * x, axis=-1, keepdims=True)
        h_ref[...] = (x * lax.rsqrt(ms + RMS_EPS) * g_ref[...]).astype(BF16)
        acc_ref[...] = jnp.zeros_like(acc_ref)

    h = h_ref[...]
    tf = wg_ref.shape[1]
    part = None
    for c in range(FFN_CHAINS):
        cs = slice(c * tf // FFN_CHAINS, (c + 1) * tf // FFN_CHAINS)
        a = _dot(h, wg_ref[:, cs])
        b = _dot(h, wu_ref[:, cs])
        d = _dot((a * jax.nn.sigmoid(a) * b).astype(BF16), wd_ref[cs, :])
        part = d if part is None else part + d
    acc_ref[...] += part

    @pl.when(f == pl.num_programs(1) - 1)
    def _():
        o_ref[...] = x_ref[...] + acc_ref[...]


def _ffn(x, gain, wg, wu, wd):
    n = x.shape[0]
    tm = _tile(n, 512)
    tf = 512
    return pl.pallas_call(
        _ffn_kernel,
        grid=(n // tm, D_FF // tf),
        in_specs=[pl.BlockSpec((tm, D_MODEL), lambda i, f: (i, 0)),
                  pl.BlockSpec((1, D_MODEL), lambda i, f: (0, 0)),
                  pl.BlockSpec((D_MODEL, tf), lambda i, f: (0, f)),
                  pl.BlockSpec((D_MODEL, tf), lambda i, f: (0, f)),
                  pl.BlockSpec((tf, D_MODEL), lambda i, f: (f, 0))],
        out_specs=pl.BlockSpec((tm, D_MODEL), lambda i, f: (i, 0)),
        out_shape=jax.ShapeDtypeStruct((n, D_MODEL), F32),
        scratch_shapes=[pltpu.VMEM((tm, D_MODEL), BF16), pltpu.VMEM((tm, D_MODEL), F32)],
        compiler_params=_cparams("parallel", "arbitrary"),
        name="ffn",
    )(x, gain, wg, wu, wd)


def _mixer_a_kernel(u_ref, pre_ref, w_ref, y_ref, newpre_ref, ext_ref, *, rows):
    @pl.when(pl.program_id(1) == 0)
    def _():
        ext_ref[6:8, :] = pre_ref[0]

    u = u_ref[...]
    v = u[:, 512:1024] * u[:, 1024:1536]
    ext_ref[8:8 + rows, :] = v
    w = w_ref[...]
    y = ext_ref[6:6 + rows, :] * w[0:1] + ext_ref[7:7 + rows, :] * w[1:2] + v * w[2:3]
    y_ref[...] = u[:, 0:512] * y
    last = ext_ref[8 + rows - 2:8 + rows, :]
    ext_ref[6:8, :] = last
    newpre_ref[0] = last


def _mixer_a(u, prefix, w, nseq, t):
    rows = _tile(t, 512)
    nblk = t // rows
    return pl.pallas_call(
        functools.partial(_mixer_a_kernel, rows=rows),
        grid=(nseq, nblk),
        in_specs=[pl.BlockSpec((rows, 3 * D_CONV), lambda s, j: (s * nblk + j, U_A // (3 * D_CONV))),
                  pl.BlockSpec((1, CONV_A_WIDTH - 1, D_CONV), lambda s, j: (s, 0, 0)),
                  pl.BlockSpec((CONV_A_WIDTH, D_CONV), lambda s, j: (0, 0))],
        out_specs=[pl.BlockSpec((rows, D_CONV), lambda s, j: (s * nblk + j, 0)),
                   pl.BlockSpec((1, CONV_A_WIDTH - 1, D_CONV), lambda s, j: (s, 0, 0))],
        out_shape=[jax.ShapeDtypeStruct((nseq * t, D_CONV), F32),
                   jax.ShapeDtypeStruct((nseq, CONV_A_WIDTH - 1, D_CONV), F32)],
        scratch_shapes=[pltpu.VMEM((8 + rows, D_CONV), F32)],
        compiler_params=_cparams("parallel", "arbitrary"),
        name="mixer_a",
    )(u, prefix, w)


def _ssd_kernel(z_ref, xbc_ref, misc_ref, pre_ref, h0_ref, cw_ref, cb_ref, dtb_ref, alog_ref, dsk_ref,
                y_ref, newpre_ref, hout_ref, ext_ref, st_ref, pad_ref, *, lin):
    L = SSM_CHUNK
    c = pl.program_id(1)

    @pl.when(c == 0)
    def _():
        if lin < L:
            ext_ref[...] = jnp.zeros_like(ext_ref)
            pad_ref[...] = jnp.zeros_like(pad_ref)
        ext_ref[5:8, :] = pre_ref[0]
        st_ref[...] = h0_ref[0].reshape(SSM_HEADS * HEAD_DIM, SSM_STATE)

    ext_ref[8:8 + lin, :] = xbc_ref[...]
    cw = cw_ref[...]
    acc = (ext_ref[5:5 + L, :] * cw[0:1] + ext_ref[6:6 + L, :] * cw[1:2]
           + ext_ref[7:7 + L, :] * cw[2:3] + ext_ref[8:8 + L, :] * cw[3:4])
    acc = acc + cb_ref[...]
    xbc = acc * jax.nn.sigmoid(acc)
    newp = ext_ref[8 + lin - 3:8 + lin, :]
    ext_ref[5:8, :] = newp
    newpre_ref[0] = newp

    if lin < L:
        pad_ref[0:lin, :] = misc_ref[...]
        misc = pad_ref[...]
    else:
        misc = misc_ref[...]
    rows = lax.broadcasted_iota(jnp.int32, (L, L), 0)
    cols = lax.broadcasted_iota(jnp.int32, (L, L), 1)
    x = misc + dtb_ref[...]
    dt = jnp.maximum(x, 0.0) + jnp.log1p(jnp.exp(-jnp.abs(x)))
    if lin < L:
        dt = jnp.where(rows < lin, dt, 0.0)
    la = dt * (-jnp.exp(alog_ref[...]))

    causal = rows >= cols
    tril = jnp.where(causal, 1.0, 0.0).astype(BF16)
    hi, mid, lo = _split3(la)
    cum = _dot(tril, hi) + _dot(tril, mid) + _dot(tril, lo)
    cum_t = cum.T
    last = cum[L - 1:L, :]
    lo_half = cols < HEAD_DIM
    top_rows = rows < HEAD_DIM

    z = z_ref[...]
    dsk = dsk_ref[...]
    for g in range(2):
        bg = xbc[:, 512 + g * 128:512 + (g + 1) * 128].astype(BF16)
        cg = xbc[:, 768 + g * 128:768 + (g + 1) * 128].astype(BF16)
        gram = lax.dot_general(cg, bg, NT_DIMS, preferred_element_type=F32)
        for i in (2 * g, 2 * g + 1):
            a, b = 2 * i, 2 * i + 1
            sl = slice(i * LANES, (i + 1) * LANES)
            xs_p = xbc[:, sl]
            col_a = cum[:, a:a + 1]
            col_b = cum[:, b:b + 1]
            xdt = xs_p * jnp.where(lo_half, dt[:, a:a + 1], dt[:, b:b + 1])
            dec_a = jnp.exp(jnp.where(causal, col_a - cum_t[a:a + 1, :], NEG_INF))
            dec_b = jnp.exp(jnp.where(causal, col_b - cum_t[b:b + 1, :], NEG_INF))
            xa = jnp.where(lo_half, xdt, 0.0).astype(BF16)
            xb = jnp.where(lo_half, 0.0, xdt).astype(BF16)
            y_intra = _dot((gram * dec_a).astype(BF16), xa) + _dot((gram * dec_b).astype(BF16), xb)
            st = st_ref[sl, :]
            y_inter = (lax.dot_general(cg, st.astype(BF16), NT_DIMS, preferred_element_type=F32)
                       * jnp.where(lo_half, jnp.exp(col_a), jnp.exp(col_b)))
            to_end = jnp.where(lo_half, jnp.exp(last[:, a:a + 1] - col_a), jnp.exp(last[:, b:b + 1] - col_b))
            xw_t = (xdt * to_end).T.astype(BF16)
            decay = jnp.where(top_rows, jnp.exp(last[:, a:a + 1]), jnp.exp(last[:, b:b + 1]))
            st_ref[sl, :] = st * decay + _dot(xw_t, bg)
            y = y_intra + y_inter + dsk[:, sl] * xs_p
            zp = z[:, sl]
            y_ref[:, sl] = y[0:lin] * (zp * jax.nn.sigmoid(zp))

    @pl.when(c == pl.num_programs(1) - 1)
    def _():
        hout_ref[0] = st_ref[...].reshape(SSM_HEADS, HEAD_DIM, SSM_STATE)


def _ssd(u, prefix, h0, cw, cb, dtb, alog, dsk, nseq, t):
    lin = min(t, SSM_CHUNK)
    nc = t // lin
    full = lambda shape: pl.BlockSpec(shape, lambda s, c: (0,) * len(shape))
    return pl.pallas_call(
        functools.partial(_ssd_kernel, lin=lin),
        grid=(nseq, nc),
        in_specs=[pl.BlockSpec((lin, D_SSM), lambda s, c: (s * nc + c, U_Z // D_SSM)),
                  pl.BlockSpec((lin, SSM_CONV_DIM), lambda s, c: (s * nc + c, U_XBC // SSM_CONV_DIM)),
                  pl.BlockSpec((lin, LANES), lambda s, c: (s * nc + c, U_MISC // LANES)),
                  pl.BlockSpec((1, SSM_CONV_WIDTH - 1, SSM_CONV_DIM), lambda s, c: (s, 0, 0)),
                  pl.BlockSpec((1, SSM_HEADS, HEAD_DIM, SSM_STATE), lambda s, c: (s, 0, 0, 0)),
                  full((SSM_CONV_WIDTH, SSM_CONV_DIM)), full((1, SSM_CONV_DIM)),
                  full((1, LANES)), full((1, LANES)), full((1, D_SSM))],
        out_specs=[pl.BlockSpec((lin, D_SSM), lambda s, c: (s * nc + c, 0)),
                   pl.BlockSpec((1, SSM_CONV_WIDTH - 1, SSM_CONV_DIM), lambda s, c: (s, 0, 0)),
                   pl.BlockSpec((1, SSM_HEADS, HEAD_DIM, SSM_STATE), lambda s, c: (s, 0, 0, 0))],
        out_shape=[jax.ShapeDtypeStruct((nseq * t, D_SSM), F32),
                   jax.ShapeDtypeStruct((nseq, SSM_CONV_WIDTH - 1, SSM_CONV_DIM), F32),
                   jax.ShapeDtypeStruct((nseq, SSM_HEADS, HEAD_DIM, SSM_STATE), F32)],
        scratch_shapes=[pltpu.VMEM((8 + SSM_CHUNK, SSM_CONV_DIM), F32),
                        pltpu.VMEM((SSM_HEADS * HEAD_DIM, SSM_STATE), F32),
                        pltpu.VMEM((SSM_CHUNK, LANES), F32)],
        compiler_params=_cparams("parallel", "arbitrary"),
        name="ssd",
    )(u, u, u, prefix, h0, cw, cb, dtb, alog, dsk)


def _headnorm(x, mavg, gain):
    x2 = x * x
    hi = x2.astype(BF16)
    lo = (x2 - hi.astype(F32)).astype(BF16)
    ms = _dot(hi, mavg) + _dot(lo, mavg)
    return x * lax.rsqrt(ms + RMS_EPS) * gain


def _prep_kernel(q_ref, kv_ref, mavg_ref, qg_ref, kg1_ref, kg2_ref,
                 qhm_ref, nk_ref, nv_ref, kwin_ref, *maybe_kvwin_ref):
    mavg = mavg_ref[...]
    lo = _lo_half(q_ref.shape[0])
    for c in range(4):
        qn = _headnorm(q_ref[:, c * 256:(c + 1) * 256], mavg, qg_ref[...]) * ATTN_SCALE
        for cc in range(2):
            col = qn[:, cc * LANES:(cc + 1) * LANES]
            h = c * 4 + cc * 2
            qhm_ref[h] = jnp.where(lo, col, 0.0).astype(qhm_ref.dtype)
            qhm_ref[h + 1] = jnp.where(lo, _swap_halves(col), 0.0).astype(qhm_ref.dtype)
    ksel = _headnorm(kv_ref[:, 512:768], mavg, kg1_ref[...])
    kwin = _headnorm(kv_ref[:, 1024:1280], mavg, kg2_ref[...])
    nk_ref[:, 0:256] = kv_ref[:, 0:256]
    nk_ref[:, 256:512] = ksel
    nv_ref[:, 0:256] = kv_ref[:, 256:512]
    nv_ref[:, 256:512] = kv_ref[:, 768:1024]
    kwin_ref[...] = kwin
    if maybe_kvwin_ref:
        kvwin_ref, = maybe_kvwin_ref
        for c in range(q_ref.shape[0] // KEY_TILE):
            rs = slice(c * KEY_TILE, (c + 1) * KEY_TILE)
            _store_kv_t(lambda g: kvwin_ref.at[g, c], kwin[rs].T, kv_ref[rs, 1280:1536].T)


def _prep(u, mavg, qg, kg1, kg2, q_dtype, emit_window_tiles):
    n = u.shape[0]
    tm = _tile(n, 256)
    full = lambda shape: pl.BlockSpec(shape, lambda i: (0,) * len(shape))
    out_specs = [pl.BlockSpec((ATTN_HEADS, tm, LANES), lambda i: (0, i, 0)),
                 pl.BlockSpec((tm, PAGE_COLS), lambda i: (i, 0)),
                 pl.BlockSpec((tm, PAGE_COLS), lambda i: (i, 0)),
                 pl.BlockSpec((tm, 256), lambda i: (i, 0))]
    out_shape = [jax.ShapeDtypeStruct((ATTN_HEADS, n, LANES), q_dtype),
                 jax.ShapeDtypeStruct((n, PAGE_COLS), F32),
                 jax.ShapeDtypeStruct((n, PAGE_COLS), F32),
                 jax.ShapeDtypeStruct((n, 256), F32)]
    if emit_window_tiles:
        out_specs.append(pl.BlockSpec((KV_HEADS, tm // KEY_TILE, 2 * HEAD_DIM, KEY_TILE), lambda i: (0, i, 0, 0)))
        out_shape.append(jax.ShapeDtypeStruct((KV_HEADS, n // KEY_TILE, 2 * HEAD_DIM, KEY_TILE), BF16))
    return pl.pallas_call(
        _prep_kernel,
        grid=(n // tm,),
        in_specs=[pl.BlockSpec((tm, D_ATTN), lambda i: (i, U_Q // D_ATTN)),
                  pl.BlockSpec((tm, KV_COLS), lambda i: (i, 0)),
                  full((256, 256)), full((1, 256)), full((1, 256)), full((1, 256))],
        out_specs=out_specs,
        out_shape=out_shape,
        compiler_params=_cparams("parallel"),
        name="nsa_prep",
    )(u, u, mavg, qg, kg1, kg2)


PAGES_PER_STEP = 16


def _gelu_tanh(x):
    return 0.5 * x * (1.0 + jnp.tanh(0.7978845608028654 * (x + 0.044715 * (x * x * x))))


def _pagepass_kernel(pt_ref, *refs, nchunk, feature_major):
    pp = PAGES_PER_STEP

    pi = lax.broadcasted_iota(jnp.int32, (PAGE_SIZE, PAGE_SIZE), 0)
    pk = lax.broadcasted_iota(jnp.int32, (PAGE_SIZE, PAGE_SIZE), 1)
    perm = jnp.where(pk == CMP_STRIDE * (pi & 7) + (pi >> 3), 1.0, 0.0).astype(BF16)

    def cmp_rows_permuted(page, c):
        if feature_major:
            return lax.dot_general(perm, page[0, c * LANES:(c + 1) * LANES, :].astype(BF16), NT_DIMS,
                                   preferred_element_type=F32)
        return _dot(perm, page[0, :, c * LANES:(c + 1) * LANES].astype(BF16))

    def sel_t(page):
        return page[0, 256:512, :] if feature_major else page[0, :, 256:512].T

    kpages = refs[0:pp]
    vpages = refs[pp:2 * pp]
    pos_ref, w1k_ref, w1v_ref, w2k_ref, w2v_ref, kg_ref = refs[2 * pp:2 * pp + 6]
    kvcmp_ref, kvsel_ref = refs[2 * pp + 6:2 * pp + 8]
    xk_ref, xv_ref, hs_ref = refs[2 * pp + 8:]
    j = pl.program_id(1)
    lo16 = _lo_half(16)

    for i2 in range(pp // 2):
        r0 = pl.multiple_of((j * pp + 2 * i2) * 8, 16)
        for pages, xs_ref in ((kpages, xk_ref), (vpages, xv_ref)):
            for c in range(2):
                pa = cmp_rows_permuted(pages[2 * i2], c)
                pb = cmp_rows_permuted(pages[2 * i2 + 1], c)
                for a in range(8):
                    ev, od = slice(16 * a, 16 * a + 8), slice(16 * a + 8, 16 * a + 16)
                    ec = jnp.concatenate([pa[ev], pb[ev]], axis=0)
                    oc = jnp.concatenate([pa[od], pb[od]], axis=0)
                    xs_ref[2 * c, pl.ds(r0, 16), a * LANES:(a + 1) * LANES] = (
                        jnp.where(lo16, ec, _swap_halves(oc)).astype(BF16))
                    xs_ref[2 * c + 1, pl.ds(r0, 16), a * LANES:(a + 1) * LANES] = (
                        jnp.where(lo16, _swap_halves(ec), oc).astype(BF16))

    per_chunk = SEL_CHUNK // PAGE_SIZE
    for i in range(pp):
        ls = slice((i % per_chunk) * PAGE_SIZE, (i % per_chunk + 1) * PAGE_SIZE)
        _store_kv_t(lambda g: kvsel_ref.at[0, g, i // per_chunk, :, ls], sel_t(kpages[i]), sel_t(vpages[i]))

    @pl.when(j == pl.num_programs(1) - 1)
    def _():
        for slot in range(2 * KV_HEADS):
            hs_ref[slot, nchunk:nchunk + 8, :] = jnp.zeros((8, 256), F32)
        pos = pos_ref[...]
        phi = pos.astype(BF16)
        plo = (pos - phi.astype(F32)).astype(BF16)
        bias_k = _dot(phi, w1k_ref[...]) + _dot(plo, w1k_ref[...])
        bias_v = _dot(phi, w1v_ref[...]) + _dot(plo, w1v_ref[...])
        bias_k = bias_k[0:1, 0:128] + bias_k[1:2, 128:256]
        bias_v = bias_v[2:3, 0:128] + bias_v[3:4, 128:256]

        for g in range(KV_HEADS):
            hs_ref[2 * g, 0:nchunk, :] = _dot(xk_ref[g], w1k_ref[...])
            hs_ref[2 * g + 1, 0:nchunk, :] = _dot(xv_ref[g], w1v_ref[...])

        def summarise(slot, bias, w2_ref):
            pre = hs_ref[slot, 0:nchunk, 0:128] + hs_ref[slot, 1:nchunk + 1, 128:256] + bias
            return _dot(_gelu_tanh(pre).astype(BF16), w2_ref[...])

        for g in range(KV_HEADS):
            ko = summarise(2 * g, bias_k, w2k_ref)
            vo = summarise(2 * g + 1, bias_v, w2v_ref)
            ms = jnp.sum(ko * ko, axis=-1, keepdims=True) * (1.0 / HEAD_DIM)
            kvcmp_ref[0, g] = (ko * lax.rsqrt(ms + RMS_EPS) * kg_ref[...] + vo).astype(BF16)


def _pagepass(ptab, kpages, vpages, pos, w1k, w1v, w2k, w2v, kg, nseq, npages):
    pp = PAGES_PER_STEP
    nsteps = npages // pp
    nchunk = npages * PAGE_SIZE // CMP_STRIDE
    feature_major = kpages.shape[1] == PAGE_COLS
    chunks_per_step = pp * PAGE_SIZE // SEL_CHUNK

    def page_spec(i):
        return pl.BlockSpec((1,) + kpages.shape[1:],
                            lambda s, j, pt: (pt[s * npages + j * pp + i], 0, 0))

    full = lambda shape: pl.BlockSpec(shape, lambda s, j, pt: (0,) * len(shape))
    grid_spec = pltpu.PrefetchScalarGridSpec(
        num_scalar_prefetch=1,
        grid=(nseq, nsteps),
        in_specs=([page_spec(i) for i in range(pp)] + [page_spec(i) for i in range(pp)]
                  + [full((8, 1024)), full((1024, 256)), full((1024, 256)),
                     full((128, 128)), full((128, 128)), full((1, 128))]),
        out_specs=[pl.BlockSpec((1, KV_HEADS, nchunk, LANES), lambda s, j, pt: (s, 0, 0, 0)),
                   pl.BlockSpec((1, KV_HEADS, chunks_per_step, 2 * HEAD_DIM, SEL_CHUNK),
                                lambda s, j, pt: (s, 0, j, 0, 0))],
        scratch_shapes=[pltpu.VMEM((KV_HEADS, nchunk, 1024), BF16),
                        pltpu.VMEM((KV_HEADS, nchunk, 1024), BF16),
                        pltpu.VMEM((2 * KV_HEADS, nchunk + 8, 256), F32)],
    )
    return pl.pallas_call(
        functools.partial(_pagepass_kernel, nchunk=nchunk, feature_major=feature_major),
        grid_spec=grid_spec,
        out_shape=[jax.ShapeDtypeStruct((nseq, KV_HEADS, nchunk, LANES), BF16),
                   jax.ShapeDtypeStruct((nseq, KV_HEADS, npages * PAGE_SIZE // SEL_CHUNK, 2 * HEAD_DIM, SEL_CHUNK),
                                        BF16)],
        compiler_params=_cparams("parallel", "arbitrary"),
        name="pagepass",
    )(ptab, *([kpages] * pp), *([vpages] * pp), pos, w1k, w1v, w2k, w2v, kg)


WIN_TILES = (WINDOW + Q_BLOCK) // KEY_TILE


def _winpack_kernel(wk_ref, wv_ref, kwn_ref, vwn_ref, ksn_ref, vsn_ref, kvw_ref, tail_ref):
    t = kwn_ref.shape[0]
    past_tiles = WINDOW // KEY_TILE
    for c in range(past_tiles):
        ls = slice(c * KEY_TILE, (c + 1) * KEY_TILE)
        _store_kv_t(lambda g: kvw_ref.at[0, g, c], wk_ref[0, :, ls], wv_ref[0, :, ls])
    pad_t = lambda ref, n: jnp.concatenate([ref[...], jnp.zeros((n - t, 256), F32)], axis=0).T
    _store_kv_t(lambda g: kvw_ref.at[0, g, past_tiles], pad_t(kwn_ref, KEY_TILE), pad_t(vwn_ref, KEY_TILE))
    _store_kv_t(lambda g: tail_ref.at[0, g], pad_t(ksn_ref, SEL_CHUNK), pad_t(vsn_ref, SEL_CHUNK))


def _winpack(wk_t, wv_t, kwin, u, new_k, new_v, nseq, t):
    tile = (1, KV_HEADS, 2 * HEAD_DIM, SEL_CHUNK)
    return pl.pallas_call(
        _winpack_kernel,
        grid=(nseq,),
        in_specs=[pl.BlockSpec((1, 256, WINDOW), lambda s: (s, 0, 0)),
                  pl.BlockSpec((1, 256, WINDOW), lambda s: (s, 0, 0)),
                  pl.BlockSpec((t, 256), lambda s: (s, 0)),
                  pl.BlockSpec((t, 256), lambda s: (s, 5)),
                  pl.BlockSpec((t, 256), lambda s: (s, 1)),
                  pl.BlockSpec((t, 256), lambda s: (s, 1))],
        out_specs=[pl.BlockSpec((1, KV_HEADS, WIN_TILES, 2 * HEAD_DIM, KEY_TILE), lambda s: (s, 0, 0, 0, 0)),
                   pl.BlockSpec(tile, lambda s: (s, 0, 0, 0))],
        out_shape=[jax.ShapeDtypeStruct((nseq, KV_HEADS, WIN_TILES, 2 * HEAD_DIM, KEY_TILE), BF16),
                   jax.ShapeDtypeStruct((nseq,) + tile[1:], BF16)],
        compiler_params=_cparams("parallel"),
        name="winpack",
    )(wk_t, wv_t, kwin, u, new_k, new_v)


MASK_BIAS = -2.0 ** 100


def _attn_tables(slopes, n_sel_chunks, n_blocks):
    slope_tab = jnp.broadcast_to(slopes[:, None, None], (ATTN_HEADS, 8, LANES))
    pos = (jnp.arange(n_sel_chunks, dtype=jnp.int32)[:, None, None] * SEL_CHUNK
           + jnp.arange(SEL_CHUNK, dtype=jnp.int32)[None, None, :])
    blk = jnp.arange(n_blocks, dtype=jnp.int32)[None, :, None]
    esel = jnp.where(blk == (pos >> 6), MASK_BIAS, 0.0).astype(BF16)
    lane_src = jnp.arange(LANES)[None, :, None]
    gate_id = jnp.arange(12)[None, None, :]
    g_id = jnp.arange(KV_HEADS)[:, None, None]
    onehot = (lane_src == GATE_LANE0 + g_id * 12 + gate_id).astype(BF16)
    gexp = jnp.broadcast_to(onehot[..., None], (KV_HEADS, LANES, 12, LANES)).reshape(KV_HEADS, LANES, 12 * LANES)
    return slope_tab, esel, gexp


def _top_blocks_unselected(val, valid):
    tq, nselp = val.shape
    if tq < LANES:
        val = jnp.concatenate([val, jnp.zeros((LANES - tq, nselp), F32)], axis=0)
    val_t = jnp.concatenate([val[:, c * LANES:(c + 1) * LANES].T for c in range(nselp // LANES)], axis=0)
    blk_t = lax.broadcasted_iota(jnp.int32, (nselp, LANES), 0).astype(F32)

    def pick_top(_, carry):
        v, sel = carry
        best = jnp.max(v, axis=0, keepdims=True)
        idx = jnp.min(jnp.where(v == best, blk_t, 1e9), axis=0, keepdims=True)
        pick = blk_t == idx
        return jnp.where(pick, -3e38, v), jnp.where(pick, 1.0, sel)

    _, sel_t = lax.fori_loop(0, TOP_BLOCKS, pick_top, (val_t, jnp.zeros((nselp, LANES), F32)), unroll=True)
    sel = jnp.concatenate([sel_t[c * LANES:(c + 1) * LANES].T for c in range(nselp // LANES)], axis=1)
    return jnp.where(valid & (sel[0:tq] > 0.5), 0.0, 1.0)


CHUNKS_PER_WORD = 8
SEL_STREAMS = 2


def _merge_streams(m_ref, l_ref, acc_of):
    m = m_ref[0]
    for st in range(1, SEL_STREAMS):
        m = jnp.maximum(m, m_ref[st])
    l_sum = None
    acc = None
    for st in range(SEL_STREAMS):
        w = jnp.exp(m_ref[st] - m)
        l_sum = w * l_ref[st] if l_sum is None else l_sum + w * l_ref[st]
        acc = w * acc_of(st) if acc is None else acc + w * acc_of(st)
    return acc / jnp.maximum(jnp.sum(l_sum, axis=-1, keepdims=True), TINY)


def _list_needed_chunks(unsel, n_chunks, words_ref, list_ref):
    nselp = unsel.shape[1]
    sel_any = jnp.max(1.0 - unsel, axis=0, keepdims=True)
    lane = lax.broadcasted_iota(jnp.int32, (1, nselp), 1)
    chunk_shift = (SEL_CHUNK // SEL_BLOCK).bit_length() - 1
    word_shift = chunk_shift + CHUNKS_PER_WORD.bit_length() - 1
    digit = (lane >> chunk_shift) & (CHUNKS_PER_WORD - 1)
    weight = lax.shift_left(jnp.ones_like(lane), 3 * digit).astype(F32)
    for w in range(words_ref.shape[0]):
        in_word = (lane >> word_shift) == w
        words_ref[w] = jnp.sum(jnp.where(in_word, sel_any * weight, 0.0)).astype(jnp.int32)

    def build(kc, cnt):
        used = (words_ref[kc // CHUNKS_PER_WORD] >> (3 * (kc % CHUNKS_PER_WORD))) & 7
        list_ref[cnt] = kc
        return cnt + jnp.where(used > 0, 1, 0)

    return lax.fori_loop(0, n_chunks, build, 0)


def _attn_kernel(q_ref, misc_ref, kvc_ref, kvs_ref, kvw_ref, slope_ref, esel_ref, gexp_ref,
                 o_ref, m_ref, l_ref, acc_ref, words_ref, list_ref, *, tq, nchunk):
    q0 = 0
    wpos0 = 0
    nselp = LANES
    qs = pl.program_id(2) * tq
    m_rows = Q_PER_KV * tq
    q = q_ref[...].reshape(m_rows, LANES)
    qb = q.astype(BF16)
    tpos = q0 + qs + lax.broadcasted_iota(jnp.int32, (tq, 1), 0)

    def per_head(x):
        return jnp.concatenate([x] * Q_PER_KV, axis=0)

    def add_by_head(s, fn):
        return jnp.concatenate([s[r * tq:(r + 1) * tq] + fn(r) for r in range(Q_PER_KV)], axis=0)

    def slope(r):
        return slope_ref[r, 0:1, 0:1]

    kvc = kvc_ref[0, 0]
    cmp_end = lax.broadcasted_iota(jnp.int32, (tq, nchunk), 1) * CMP_STRIDE + (2 * CMP_STRIDE - 1)
    cmp_mask = per_head(tpos >= cmp_end)
    cmp_pos = (lax.broadcasted_iota(jnp.int32, (1, nchunk), 1) * CMP_STRIDE + (2 * CMP_STRIDE - 1)).astype(F32)
    s = lax.dot_general(qb, kvc, NT_DIMS, preferred_element_type=F32)
    s = jnp.where(cmp_mask, add_by_head(s, lambda r: slope(r) * cmp_pos), NEG_INF)
    p = jnp.exp(s - jnp.max(s, axis=-1, keepdims=True)) * jnp.where(cmp_mask, 1.0, 0.0)
    p = p / jnp.maximum(jnp.sum(p, axis=-1, keepdims=True), TINY)
    o_cmp = _dot(p.astype(BF16), kvc)
    psum = p[0:tq] + p[tq:2 * tq] + p[2 * tq:3 * tq] + p[3 * tq:4 * tq]

    tile0 = jnp.maximum(q0 + qs - WINDOW - wpos0, 0) // KEY_TILE
    tiles = [kvw_ref[0, 0, tile0 + j] for j in range(WIN_TILES)]
    key_b = lax.broadcasted_iota(jnp.int32, (1, KEY_TILE), 1)
    s = []
    for j in range(WIN_TILES):
        pos = wpos0 + (tile0 + j) * KEY_TILE + key_b
        dist = tpos - pos
        bias = jnp.where((dist >= 0) & (dist <= WINDOW), 0.0, NEG_INF)
        s.append(add_by_head(_dot(qb, tiles[j]), lambda r: slope(r) * pos.astype(F32) + bias))
    m = s[0]
    for j in range(1, WIN_TILES):
        m = jnp.maximum(m, s[j])
    m = jnp.max(m, axis=-1, keepdims=True)
    p = [jnp.exp(sj - m) for sj in s]
    part = p[0]
    for j in range(1, WIN_TILES):
        part = part + p[j]
    o_win = lax.dot_general(p[0].astype(BF16), tiles[0], NT_DIMS, preferred_element_type=F32)
    for j in range(1, WIN_TILES):
        o_win = o_win + lax.dot_general(p[j].astype(BF16), tiles[j], NT_DIMS, preferred_element_type=F32)
    o_win = o_win / jnp.maximum(jnp.sum(part, axis=-1, keepdims=True), TINY)

    pool = jnp.where((lax.broadcasted_iota(jnp.int32, (nchunk, nselp), 0) >> 2)
                     == lax.broadcasted_iota(jnp.int32, (nchunk, nselp), 1), 1.0, 0.0).astype(BF16)
    hi, mid, lo = _split3(psum)
    imp = _dot(hi, pool) + _dot(mid, pool) + _dot(lo, pool)
    blk = lax.broadcasted_iota(jnp.int32, (tq, nselp), 1)
    cur = tpos >> 6
    forced = (blk == 0) | (blk == cur) | (blk == cur - 1)
    valid = blk <= cur
    val = jnp.where(valid, imp + jnp.where(forced, FORCE_BONUS, 0.0), NEG_INF)
    unsel = _top_blocks_unselected(val, valid)
    lhs = jnp.where(_lo_half(m_rows), q.astype(F32), per_head(_swap_halves(unsel))).astype(BF16)

    m_ref[...] = jnp.full_like(m_ref, NEG_INF)
    l_ref[...] = jnp.zeros_like(l_ref)
    acc_ref[...] = jnp.zeros_like(acc_ref)

    def sel_chunk(stream, kv_t, e_c, k0, causal_bias):
        n = kv_t.shape[1]
        s = _dot(lhs, jnp.concatenate([kv_t[0:HEAD_DIM], e_c], axis=0))
        pos = (k0 + lax.broadcasted_iota(jnp.int32, (1, n), 1)).astype(F32)
        if causal_bias is None:
            s = add_by_head(s, lambda r: slope(r) * pos)
        else:
            s = add_by_head(s, lambda r: slope(r) * pos + causal_bias)
        m_old = m_ref[stream]
        m_new = jnp.maximum(m_old, jnp.max(s, axis=-1, keepdims=True))
        alpha = jnp.exp(m_old - m_new)
        p = [jnp.exp(s[:, c * LANES:(c + 1) * LANES] - m_new) for c in range(n // LANES)]
        part = p[0]
        for c in range(1, n // LANES):
            part = part + p[c]
        l_ref[stream] = alpha * l_ref[stream] + part
        acc_ref[stream] = alpha * acc_ref[stream] + lax.dot_general(
            jnp.concatenate(p, axis=1).astype(BF16), kv_t, NT_DIMS, preferred_element_type=F32)
        m_ref[stream] = m_new

    def past_chunk(stream, slot):
        kc = list_ref[slot]
        sel_chunk(stream, kvs_ref[0, 0, kc], esel_ref[kc], kc * SEL_CHUNK, None)

    def group(i, carry):
        for stream in range(SEL_STREAMS):
            past_chunk(stream, SEL_STREAMS * i + stream)
        return carry

    def single(i, carry):
        past_chunk(0, n_need - 1 - i)
        return carry

    n_full = qs // SEL_CHUNK
    n_need = _list_needed_chunks(unsel, n_full, words_ref, list_ref)
    lax.fori_loop(0, n_need // SEL_STREAMS, group, 0)
    lax.fori_loop(0, n_need % SEL_STREAMS, single, 0)

    key_j = lax.broadcasted_iota(jnp.int32, (tq, SEL_CHUNK), 1)
    causal = jnp.where(tpos >= n_full * SEL_CHUNK + key_j, 0.0, NEG_INF)
    sel_chunk(SEL_STREAMS - 1, kvs_ref[0, 0, n_full], esel_ref[n_full], n_full * SEL_CHUNK, causal)
    o_sel = _merge_streams(m_ref, l_ref, lambda st: acc_ref[st])

    hi, mid, lo = _split3(jax.nn.sigmoid(misc_ref[...]))
    gexp = gexp_ref[0]
    gates = _dot(hi, gexp) + _dot(mid, gexp) + _dot(lo, gexp)
    comb = []
    for r in range(Q_PER_KV):
        rs = slice(r * tq, (r + 1) * tq)
        gate = [gates[:, (3 * r + c) * LANES:(3 * r + c + 1) * LANES] for c in range(3)]
        comb.append(gate[0] * o_cmp[rs] + gate[1] * o_sel[rs] + gate[2] * o_win[rs])
    lo_half = _lo_half(tq)
    o_ref[:, 0:LANES] = jnp.where(lo_half, _swap_halves(comb[0]), comb[1])
    o_ref[:, LANES:2 * LANES] = jnp.where(lo_half, _swap_halves(comb[2]), comb[3])


def _attn(tables, qhm, u, kvcmp, kvsel, kvwin, nseq, t):
    slope_tab, esel, gexp = tables
    tq = Q_BLOCK
    nqb = t // tq
    nchunk = kvcmp.shape[2]
    sel_chunks = kvsel.shape[2]
    win_tiles = kvwin.shape[2]
    m_rows = Q_PER_KV * tq
    assert esel.shape[1] == HEAD_DIM
    in_specs = [pl.BlockSpec((Q_PER_KV, tq, LANES), lambda s, g, i: (g, s * nqb + i, 0)),
                pl.BlockSpec((tq, LANES), lambda s, g, i: (s * nqb + i, U_MISC // LANES)),
                pl.BlockSpec((1, 1, nchunk, LANES), lambda s, g, i: (s, g, 0, 0)),
                pl.BlockSpec((1, 1, sel_chunks, 2 * HEAD_DIM, SEL_CHUNK), lambda s, g, i: (s, g, 0, 0, 0)),
                pl.BlockSpec((1, 1, win_tiles, 2 * HEAD_DIM, KEY_TILE), lambda s, g, i: (g, s, 0, 0, 0)),
                pl.BlockSpec((Q_PER_KV, 8, LANES), lambda s, g, i: (g, 0, 0)),
                pl.BlockSpec(esel.shape, lambda s, g, i: (0, 0, 0)),
                pl.BlockSpec((1, LANES, 12 * LANES), lambda s, g, i: (g, 0, 0))]
    args = [qhm, u, kvcmp, kvsel, kvwin, slope_tab, esel, gexp]
    stream_state = pltpu.VMEM((SEL_STREAMS, m_rows, LANES), F32)
    return pl.pallas_call(
        functools.partial(_attn_kernel, tq=tq, nchunk=nchunk),
        grid=(nseq, KV_HEADS, nqb),
        in_specs=in_specs,
        out_specs=pl.BlockSpec((tq, 2 * LANES), lambda s, g, i: (s * nqb + i, g)),
        out_shape=jax.ShapeDtypeStruct((nseq * t, D_ATTN), F32),
        scratch_shapes=[stream_state, stream_state, stream_state,
                        pltpu.SMEM((-(-sel_chunks // CHUNKS_PER_WORD),), jnp.int32),
                        pltpu.SMEM((sel_chunks,), jnp.int32)],
        compiler_params=_cparams("parallel", "parallel", "arbitrary"),
        name="nsa_attn",
    )(*args)


def _attn_sample_kernel(q_ref, misc_ref, kvc_ref, kvs_ref, kvw_ref, tail_ref, slope_ref, esel_ref, gexp_ref,
                        o_ref, m_ref, l_ref, acc_ref, words_ref, list_ref, *, tq, q0, nchunk, nselp, lmain):
    n_rows = ATTN_HEADS * tq
    grp_rows = Q_PER_KV * tq
    wide = KV_HEADS * LANES
    q = q_ref[...].reshape(n_rows, LANES)
    qb = q.astype(BF16)
    tpos = q0 + lax.broadcasted_iota(jnp.int32, (tq, 1), 0)
    slope_col = jnp.concatenate([jnp.broadcast_to(slope_ref[h, 0:1, 0:1], (tq, 1)) for h in range(ATTN_HEADS)],
                                axis=0)
    grp_shift = grp_rows.bit_length() - 1
    own = ((lax.broadcasted_iota(jnp.int32, (n_rows, wide), 1) >> 7)
           == (lax.broadcasted_iota(jnp.int32, (n_rows, wide), 0) >> grp_shift))
    q_diag = jnp.where(own, jnp.concatenate([q.astype(F32)] * KV_HEADS, axis=1), 0.0).astype(BF16)
    row_grp = lax.broadcasted_iota(jnp.int32, (n_rows, LANES), 0) >> grp_shift

    def per_rows(x, copies):
        return jnp.concatenate([x] * copies, axis=0)

    def own_block(x):
        out = x[:, 0:LANES]
        for g in range(1, KV_HEADS):
            out = jnp.where(row_grp == g, x[:, g * LANES:(g + 1) * LANES], out)
        return out

    cmp_end = lax.broadcasted_iota(jnp.int32, (tq, nchunk), 1) * CMP_STRIDE + (2 * CMP_STRIDE - 1)
    cmp_mask = per_rows(tpos >= cmp_end, ATTN_HEADS)
    cmp_pos = (lax.broadcasted_iota(jnp.int32, (1, nchunk), 1) * CMP_STRIDE + (2 * CMP_STRIDE - 1)).astype(F32)
    grp = lambda x, g: x[g * grp_rows:(g + 1) * grp_rows]
    s = jnp.concatenate([lax.dot_general(grp(qb, g), kvc_ref[0, g], NT_DIMS, preferred_element_type=F32)
                         for g in range(KV_HEADS)], axis=0)
    s = jnp.where(cmp_mask, s + slope_col * cmp_pos, NEG_INF)
    p = jnp.exp(s - jnp.max(s, axis=-1, keepdims=True)) * jnp.where(cmp_mask, 1.0, 0.0)
    p = p / jnp.maximum(jnp.sum(p, axis=-1, keepdims=True), TINY)
    o_cmp = jnp.concatenate([_dot(grp(p, g).astype(BF16), kvc_ref[0, g]) for g in range(KV_HEADS)], axis=0)
    head_p = lambda h: p[h * tq:(h + 1) * tq]
    psum = jnp.concatenate(
        [head_p(4 * g) + head_p(4 * g + 1) + head_p(4 * g + 2) + head_p(4 * g + 3) for g in range(KV_HEADS)],
        axis=0)

    key_b = lax.broadcasted_iota(jnp.int32, (1, KEY_TILE), 1)
    tiles = [kvw_ref[0, :, j].reshape(wide, KEY_TILE) for j in range(WIN_TILES)]
    s = []
    for j in range(WIN_TILES):
        pos = q0 - WINDOW + j * KEY_TILE + key_b
        dist = tpos - pos
        bias = per_rows(jnp.where((dist >= 0) & (dist <= WINDOW), 0.0, NEG_INF), ATTN_HEADS)
        s.append(_dot(q_diag, tiles[j]) + slope_col * pos.astype(F32) + bias)
    m = s[0]
    for j in range(1, WIN_TILES):
        m = jnp.maximum(m, s[j])
    m = jnp.max(m, axis=-1, keepdims=True)
    p = [jnp.exp(sj - m) for sj in s]
    part = p[0]
    for j in range(1, WIN_TILES):
        part = part + p[j]
    o_win = lax.dot_general(p[0].astype(BF16), tiles[0], NT_DIMS, preferred_element_type=F32)
    for j in range(1, WIN_TILES):
        o_win = o_win + lax.dot_general(p[j].astype(BF16), tiles[j], NT_DIMS, preferred_element_type=F32)
    o_win = own_block(o_win) / jnp.maximum(jnp.sum(part, axis=-1, keepdims=True), TINY)

    pool = jnp.where((lax.broadcasted_iota(jnp.int32, (nchunk, nselp), 0) >> 2)
                     == lax.broadcasted_iota(jnp.int32, (nchunk, nselp), 1), 1.0, 0.0).astype(BF16)
    hi, mid, lo = _split3(psum)
    imp = _dot(hi, pool) + _dot(mid, pool) + _dot(lo, pool)
    blk = lax.broadcasted_iota(jnp.int32, (grp_rows, nselp), 1)
    cur = per_rows(tpos, KV_HEADS) >> 6
    forced = (blk == 0) | (blk == cur) | (blk == cur - 1)
    valid = blk <= cur
    val = jnp.where(valid, imp + jnp.where(forced, FORCE_BONUS, 0.0), NEG_INF)
    unsel = _top_blocks_unselected(val, valid)
    unsel_rows = jnp.concatenate([unsel[(h // Q_PER_KV) * tq:(h // Q_PER_KV + 1) * tq]
                                  for h in range(ATTN_HEADS)], axis=0).astype(BF16)

    m_ref[...] = jnp.full_like(m_ref, NEG_INF)
    l_ref[...] = jnp.zeros_like(l_ref)
    acc_ref[...] = jnp.zeros_like(acc_ref)

    def sel_chunk(stream, kv_t, e_c, k0, causal_bias):
        n = kv_t.shape[1]
        pos = (k0 + lax.broadcasted_iota(jnp.int32, (1, n), 1)).astype(F32)
        s = _dot(q_diag, kv_t) + _dot(unsel_rows, e_c) + slope_col * pos
        if causal_bias is not None:
            s = s + causal_bias
        m_old = m_ref[stream]
        m_new = jnp.maximum(m_old, jnp.max(s, axis=-1, keepdims=True))
        alpha = jnp.exp(m_old - m_new)
        p = [jnp.exp(s[:, c * LANES:(c + 1) * LANES] - m_new) for c in range(n // LANES)]
        part = p[0]
        for c in range(1, n // LANES):
            part = part + p[c]
        l_ref[stream] = alpha * l_ref[stream] + part
        acc_ref[stream] = jnp.concatenate([alpha] * KV_HEADS, axis=1) * acc_ref[stream] + lax.dot_general(
            jnp.concatenate(p, axis=1).astype(BF16), kv_t, NT_DIMS, preferred_element_type=F32)
        m_ref[stream] = m_new

    def past_chunk(stream, slot):
        kc = list_ref[slot]
        sel_chunk(stream, kvs_ref[0, :, kc].reshape(wide, SEL_CHUNK), esel_ref[kc], kc * SEL_CHUNK, None)

    def group(i, carry):
        for stream in range(SEL_STREAMS):
            past_chunk(stream, SEL_STREAMS * i + stream)
        return carry

    def single(i, carry):
        past_chunk(0, n_need - 1 - i)
        return carry

    n_need = _list_needed_chunks(unsel, lmain // SEL_CHUNK, words_ref, list_ref)
    lax.fori_loop(0, n_need // SEL_STREAMS, group, 0)
    lax.fori_loop(0, n_need % SEL_STREAMS, single, 0)

    key_j = lax.broadcasted_iota(jnp.int32, (tq, SEL_CHUNK), 1)
    causal = per_rows(jnp.where(tpos >= lmain + key_j, 0.0, NEG_INF), ATTN_HEADS)
    sel_chunk(SEL_STREAMS - 1, tail_ref[0].reshape(wide, SEL_CHUNK), esel_ref[lmain // SEL_CHUNK], lmain, causal)
    o_sel = _merge_streams(m_ref, l_ref, lambda st: own_block(acc_ref[st]))

    hi, mid, lo = _split3(jax.nn.sigmoid(misc_ref[...]))
    comb = []
    for g in range(KV_HEADS):
        gates = _dot(hi, gexp_ref[g]) + _dot(mid, gexp_ref[g]) + _dot(lo, gexp_ref[g])
        for r in range(Q_PER_KV):
            hs = slice((g * Q_PER_KV + r) * tq, (g * Q_PER_KV + r + 1) * tq)
            gate = [gates[:, (3 * r + c) * LANES:(3 * r + c + 1) * LANES] for c in range(3)]
            comb.append(gate[0] * o_cmp[hs] + gate[1] * o_sel[hs] + gate[2] * o_win[hs])
    lo_half = _lo_half(tq)
    for i in range(ATTN_HEADS // 2):
        o_ref[:, i * LANES:(i + 1) * LANES] = jnp.where(lo_half, _swap_halves(comb[2 * i]), comb[2 * i + 1])


def _attn_sample(tables, qhm, u, kvcmp, kvsel, kvwin, tail, nseq, t, q0):
    slope_tab, esel, gexp = tables
    nchunk = kvcmp.shape[2]
    sel_chunks = kvsel.shape[2]
    n_rows = ATTN_HEADS * t
    full = lambda a: pl.BlockSpec(a.shape, lambda s: (0,) * a.ndim)
    per_seq = lambda a: pl.BlockSpec((1,) + a.shape[1:], lambda s: (s,) + (0,) * (a.ndim - 1))
    wide_state = pltpu.VMEM((SEL_STREAMS, n_rows, KV_HEADS * LANES), F32)
    lane_state = pltpu.VMEM((SEL_STREAMS, n_rows, LANES), F32)
    return pl.pallas_call(
        functools.partial(_attn_sample_kernel, tq=t, q0=q0, nchunk=nchunk, nselp=esel.shape[1],
                          lmain=sel_chunks * SEL_CHUNK),
        grid=(nseq,),
        in_specs=[pl.BlockSpec((ATTN_HEADS, t, LANES), lambda s: (0, s, 0)),
                  pl.BlockSpec((t, LANES), lambda s: (s, U_MISC // LANES)),
                  per_seq(kvcmp), per_seq(kvsel), per_seq(kvwin), per_seq(tail),
                  full(slope_tab), full(esel), full(gexp)],
        out_specs=pl.BlockSpec((t, D_ATTN), lambda s: (s, 0)),
        out_shape=jax.ShapeDtypeStruct((nseq * t, D_ATTN), F32),
        scratch_shapes=[lane_state, lane_state, wide_state,
                        pltpu.SMEM((-(-sel_chunks // CHUNKS_PER_WORD),), jnp.int32),
                        pltpu.SMEM((sel_chunks,), jnp.int32)],
        compiler_params=_cparams("parallel"),
        name="nsa_attn_sample",
    )(qhm, u, kvcmp, kvsel, kvwin, tail, slope_tab, esel, gexp)


def _layer_params(l, norm_mix, w_in, conv_a_w, conv_b_w, conv_b_bias, dt_bias, a_log, d_skip,
                  q_norm, k_norm, cmp_pos, cmp_w1, cmp_w2, norm_out, w_out, norm_ffn, w_gate, w_up, w_down):
    w = w_in[l]
    w_perm = jnp.concatenate(
        [w[:, 4104:5640], w[:, 0:1536], w[:, 2048:3072], w[:, 3080:4104], w[:, 1536:2048],
         w[:, 3072:3080], w[:, 5640:5688], jnp.zeros((D_MODEL, U_COLS - 5688), F32)], axis=1).astype(BF16)
    pad8 = lambda v: jnp.pad(v, (0, LANES - v.shape[0])).reshape(1, LANES)
    head_id = jnp.arange(256) // HEAD_DIM
    pos = cmp_pos[l].reshape(4, 1024)
    w1 = cmp_w1[l]
    w2 = cmp_w2[l]
    zeros_w2 = jnp.zeros((CMP_HIDDEN, HEAD_DIM), F32)
    return dict(
        g_mix=norm_mix[l].reshape(1, D_MODEL), w_in=w_perm,
        conv_a_w=conv_a_w[l], conv_b_w=conv_b_w[l], conv_b_bias=conv_b_bias[l].reshape(1, SSM_CONV_DIM),
        dt_bias=pad8(dt_bias[l]), a_log=pad8(a_log[l]),
        d_skip=jnp.repeat(d_skip[l], HEAD_DIM).reshape(1, D_SSM),
        mavg=jnp.where(head_id[:, None] == head_id[None, :], 1.0 / HEAD_DIM, 0.0).astype(BF16),
        q_gain=jnp.tile(q_norm[l], 4).reshape(1, 256),
        k_gain1=jnp.tile(k_norm[l, 1], 4).reshape(1, 256),
        k_gain2=jnp.tile(k_norm[l, 2], 4).reshape(1, 256),
        k_gain0=pad8(k_norm[l, 0]),
        pos=jnp.pad(pos, ((0, 4), (0, 0))),
        w1k=jnp.concatenate([w1[0, 0:1024], w1[0, 1024:2048]], axis=1).astype(BF16),
        w1v=jnp.concatenate([w1[1, 0:1024], w1[1, 1024:2048]], axis=1).astype(BF16),
        w2k=jnp.concatenate([w2[0], zeros_w2], axis=1).astype(BF16),
        w2v=jnp.concatenate([zeros_w2, w2[1]], axis=1).astype(BF16),
        g_out=norm_out[l].reshape(1, D_MODEL), w_out=w_out[l].astype(BF16),
        g_ffn=norm_ffn[l].reshape(1, D_MODEL),
        w_gate=w_gate[l].astype(BF16), w_up=w_up[l].astype(BF16), w_down=w_down[l].astype(BF16),
    )


def _group_tables(slopes, t, q0):
    lmain = q0 if q0 else t
    n_sel_chunks = lmain // SEL_CHUNK + (1 if q0 else 0)
    n_blocks = n_sel_chunks * SEL_CHUNK // SEL_BLOCK
    if q0:
        n_blocks = -(-n_blocks // LANES) * LANES
    else:
        assert n_blocks <= HEAD_DIM
        n_blocks = HEAD_DIM
    return _attn_tables(slopes, n_sel_chunks, n_blocks)


def _layer(x, nseq, t, q0, p, tables, conv_a_prefix, conv_b_prefix, ssm_h0, past):
    u = _in_proj(x, p["g_mix"], p["w_in"])
    ya, new_conv_a = _mixer_a(u, conv_a_prefix, p["conv_a_w"], nseq, t)
    yb, new_conv_b, new_ssm = _ssd(u, conv_b_prefix, ssm_h0, p["conv_b_w"], p["conv_b_bias"],
                                   p["dt_bias"], p["a_log"], p["d_skip"], nseq, t)
    prep_out = _prep(u, p["mavg"], p["q_gain"], p["k_gain1"], p["k_gain2"],
                     BF16 if past is None else F32, past is None)
    qhm, new_k, new_v, kwin = prep_out[:4]
    vwin = u[:, U_KV + 1280:U_KV + 1536]
    cmp_args = (p["pos"], p["w1k"], p["w1v"], p["w2k"], p["w2v"], p["k_gain0"])
    if past is None:
        npages = t // PAGE_SIZE
        ptab = jnp.arange(nseq * npages, dtype=jnp.int32)
        kvcmp, kvsel = _pagepass(ptab, new_k.reshape(-1, PAGE_SIZE, PAGE_COLS),
                                 new_v.reshape(-1, PAGE_SIZE, PAGE_COLS), *cmp_args, nseq, npages)
        kvwin_arr = prep_out[4].reshape(KV_HEADS, nseq, t // KEY_TILE, 2 * HEAD_DIM, KEY_TILE)
        yc = _attn(tables, qhm, u, kvcmp, kvsel, kvwin_arr, nseq, t)
        new_win_k = kwin.reshape(nseq, t, 256)[:, t - WINDOW:]
        new_win_v = vwin.reshape(nseq, t, 256)[:, t - WINDOW:]
    else:
        ptab, k_pages, v_pages, npages, win_k_t, win_v_t, win_k, win_v = past
        kvcmp, kvsel = _pagepass(ptab, k_pages, v_pages, *cmp_args, nseq, npages)
        kvwin_arr, tail = _winpack(win_k_t, win_v_t, kwin, u, new_k, new_v, nseq, t)
        yc = _attn_sample(tables, qhm, u, kvcmp, kvsel, kvwin_arr, tail, nseq, t, q0)
        new_win_k = jnp.concatenate([win_k[:, t:], kwin.reshape(nseq, t, 256)], axis=1)
        new_win_v = jnp.concatenate([win_v[:, t:], vwin.reshape(nseq, t, 256)], axis=1)
    x = _out_proj(x, ya, yb, yc, p["g_out"], p["w_out"])
    x = _ffn(x, p["g_ffn"], p["w_gate"], p["w_up"], p["w_down"])
    hd = (KV_HEADS, HEAD_DIM)
    state = (new_k.reshape(nseq, t, 2, *hd), new_v.reshape(nseq, t, 2, *hd),
             new_win_k.reshape(nseq, WINDOW, *hd), new_win_v.reshape(nseq, WINDOW, *hd),
             new_conv_a, new_conv_b, new_ssm)
    return x, state


def kernel(x_prompt, x_sample, cache_k, cache_v, cache_win_k, cache_win_v, state_conv_a, state_conv_b,
           state_ssm, page_table, norm_mix, w_in, conv_a_w, conv_b_w, conv_b_bias, dt_bias, a_log, d_skip,
           q_norm, k_norm, cmp_pos, cmp_w1, cmp_w2, norm_out, w_out, norm_ffn, w_gate, w_up, w_down):
    bsz, t_prompt, _ = x_prompt.shape
    dec_b, t_dec, _ = x_sample.shape
    depth, n_pool = cache_k.shape[0], cache_k.shape[1]
    npages = page_table.shape[1]
    past_len = npages * PAGE_SIZE
    assert cache_win_k.shape[2] == WINDOW and t_prompt >= WINDOW + Q_BLOCK and t_prompt % Q_BLOCK == 0
    assert t_dec == 8 and npages % PAGES_PER_STEP == 0 and (t_prompt // PAGE_SIZE) % PAGES_PER_STEP == 0

    slopes = jnp.exp2(-8.0 * jnp.arange(1, ATTN_HEADS + 1, dtype=F32) / ATTN_HEADS)
    prompt_tables = _group_tables(slopes, t_prompt, 0)
    sample_tables = _group_tables(slopes, t_dec, past_len)
    to_pages_t = lambda c: jnp.transpose(c, (0, 1, 3, 4, 5, 2)).reshape(depth * n_pool, PAGE_COLS, PAGE_SIZE)
    to_win_t = lambda w: jnp.transpose(w, (0, 2, 3, 1)).reshape(dec_b, 256, WINDOW)
    k_pages = to_pages_t(cache_k)
    v_pages = to_pages_t(cache_v)
    hp = x_prompt.reshape(bsz * t_prompt, D_MODEL)
    hs = x_sample.reshape(dec_b * t_dec, D_MODEL)
    zeros = lambda *shape: jnp.zeros(shape, F32)
    prompt_states, sample_states = [], []
    for l in range(depth):
        p = _layer_params(l, norm_mix, w_in, conv_a_w, conv_b_w, conv_b_bias, dt_bias, a_log, d_skip,
                          q_norm, k_norm, cmp_pos, cmp_w1, cmp_w2, norm_out, w_out, norm_ffn,
                          w_gate, w_up, w_down)
        hp, st_p = _layer(hp, bsz, t_prompt, 0, p, prompt_tables,
                          zeros(bsz, CONV_A_WIDTH - 1, D_CONV), zeros(bsz, SSM_CONV_WIDTH - 1, SSM_CONV_DIM),
                          zeros(bsz, SSM_HEADS, HEAD_DIM, SSM_STATE), None)
        ptab = (page_table + l * n_pool).reshape(-1).astype(jnp.int32)
        past = (ptab, k_pages, v_pages, npages, to_win_t(cache_win_k[l]), to_win_t(cache_win_v[l]),
                cache_win_k[l].reshape(dec_b, WINDOW, 256), cache_win_v[l].reshape(dec_b, WINDOW, 256))
        hs, st_s = _layer(hs, dec_b, t_dec, past_len, p, sample_tables,
                          state_conv_a[l], state_conv_b[l], state_ssm[l], past)
        prompt_states.append(st_p)
        sample_states.append(st_s)

    stack = lambda states, j: jnp.stack([s[j] for s in states], axis=0)
    return ((hp.reshape(bsz, t_prompt, D_MODEL), hs.reshape(dec_b, t_dec, D_MODEL))
            + tuple(stack(prompt_states, j) for j in range(7))
            + tuple(stack(sample_states, j) for j in range(7)))
```

```python
import functools

import jax
import jax.numpy as jnp
from jax import lax
from jax.experimental import pallas as pl
from jax.experimental.pallas import tpu as pltpu

F32 = jnp.float32
BF16 = jnp.bfloat16

D_MODEL = 2048
HEAD_DIM = 64
D_CONV = 512
D_SSM = 512
D_ATTN = 1024
CONV_A_WIDTH = 3
SSM_HEADS = 8
SSM_STATE = 128
SSM_CONV_WIDTH = 4
SSM_CHUNK = 128
SSM_CONV_DIM = 1024
ATTN_HEADS = 16
KV_HEADS = 4
Q_PER_KV = 4
KV_COLS = 1536
CMP_STRIDE = 16
CMP_HIDDEN = 128
SEL_BLOCK = 64
TOP_BLOCKS = 16
WINDOW = 512
Q_BLOCK = 128
PAGE_SIZE = 128
PAGE_COLS = 512
D_FF = 5632
RMS_EPS = 1e-6
NEG_INF = -1e30
TINY = 1e-30
FORCE_BONUS = 1e3
ATTN_SCALE = HEAD_DIM ** -0.5

U_KV = 0
U_A = 1536
U_XBC = 3072
U_Q = 4096
U_Z = 5120
U_MISC = 5632
U_COLS = 5760
GATE_LANE0 = 8

LANES = 128
KEY_TILE = 128
SEL_CHUNK = 256
VMEM_LIMIT_BYTES = 56 * 2 ** 20

NT_DIMS = (((1,), (1,)), ((), ()))


def _cparams(*sem):
    return pltpu.CompilerParams(dimension_semantics=sem, vmem_limit_bytes=VMEM_LIMIT_BYTES)


def _tile(n, pref):
    t = min(n, pref)
    while n % t:
        t //= 2
    return t


def _dot(a, b):
    return jnp.dot(a, b, preferred_element_type=F32)


def _split3(x):
    hi = x.astype(BF16)
    r1 = x - hi.astype(F32)
    mid = r1.astype(BF16)
    lo = (r1 - mid.astype(F32)).astype(BF16)
    return hi, mid, lo


def _lo_half(rows):
    return lax.broadcasted_iota(jnp.int32, (rows, LANES), 1) < HEAD_DIM


def _swap_halves(x):
    return pltpu.roll(x, HEAD_DIM, 1)


def _store_kv_t(dst, k_t, v_t):
    for g in range(KV_HEADS):
        view = dst(g)
        view[0:HEAD_DIM, :] = k_t[g * HEAD_DIM:(g + 1) * HEAD_DIM].astype(BF16)
        view[HEAD_DIM:2 * HEAD_DIM, :] = v_t[g * HEAD_DIM:(g + 1) * HEAD_DIM].astype(BF16)


def _in_proj_kernel(x_ref, g_ref, w_ref, o_ref, xn_ref):
    @pl.when(pl.program_id(1) == 0)
    def _():
        x = x_ref[...]
        ms = jnp.mean(x * x, axis=-1, keepdims=True)
        xn_ref[...] = (x * lax.rsqrt(ms + RMS_EPS) * g_ref[...]).astype(BF16)

    o_ref[...] = _dot(xn_ref[...], w_ref[...])


def _in_proj(x, gain, w):
    n = x.shape[0]
    tm = _tile(n, 1024)
    tn = 1152
    return pl.pallas_call(
        _in_proj_kernel,
        grid=(n // tm, U_COLS // tn),
        in_specs=[pl.BlockSpec((tm, D_MODEL), lambda i, j: (i, 0)),
                  pl.BlockSpec((1, D_MODEL), lambda i, j: (0, 0)),
                  pl.BlockSpec((D_MODEL, tn), lambda i, j: (0, j))],
        out_specs=pl.BlockSpec((tm, tn), lambda i, j: (i, j)),
        out_shape=jax.ShapeDtypeStruct((n, U_COLS), F32),
        scratch_shapes=[pltpu.VMEM((tm, D_MODEL), BF16)],
        compiler_params=_cparams("parallel", "arbitrary"),
        name="in_proj",
    )(x, gain, w)


def _out_proj_kernel(x_ref, ya_ref, yb_ref, yc_ref, g_ref, w_ref, o_ref, mg_ref):
    @pl.when(pl.program_id(1) == 0)
    def _():
        def nrm(y, g):
            ms = jnp.mean(y * y, axis=-1, keepdims=True)
            return (y * lax.rsqrt(ms + RMS_EPS) * g).astype(BF16)

        mg_ref[:, 0:512] = nrm(ya_ref[...], g_ref[:, 0:512])
        mg_ref[:, 512:1024] = nrm(yb_ref[...], g_ref[:, 512:1024])
        mg_ref[:, 1024:2048] = nrm(yc_ref[...], g_ref[:, 1024:2048])

    o_ref[...] = x_ref[...] + _dot(mg_ref[...], w_ref[...])


def _out_proj(x, ya, yb, yc, gain, w):
    n = x.shape[0]
    tm = _tile(n, 512)
    tn = 1024
    return pl.pallas_call(
        _out_proj_kernel,
        grid=(n // tm, D_MODEL // tn),
        in_specs=[pl.BlockSpec((tm, tn), lambda i, j: (i, j)),
                  pl.BlockSpec((tm, D_CONV), lambda i, j: (i, 0)),
                  pl.BlockSpec((tm, D_SSM), lambda i, j: (i, 0)),
                  pl.BlockSpec((tm, D_ATTN), lambda i, j: (i, 0)),
                  pl.BlockSpec((1, D_MODEL), lambda i, j: (0, 0)),
                  pl.BlockSpec((D_MODEL, tn), lambda i, j: (0, j))],
        out_specs=pl.BlockSpec((tm, tn), lambda i, j: (i, j)),
        out_shape=jax.ShapeDtypeStruct((n, D_MODEL), F32),
        scratch_shapes=[pltpu.VMEM((tm, D_MODEL), BF16)],
        compiler_params=_cparams("parallel", "arbitrary"),
        name="out_proj",
    )(x, ya, yb, yc, gain, w)


FFN_CHAINS = 2


def _ffn_kernel(x_ref, g_ref, wg_ref, wu_ref, wd_ref, o_ref, h_ref, acc_ref):
    f = pl.program_id(1)

    @pl.when(f == 0)
    def _():
        x = x_ref[...]
        ms = jnp.mean(x * x, axis=-1, keepdims=True)
        h_ref[...] = (x * lax.rsqrt(ms + RMS_EPS) * g_ref[...]).astype(BF16)
        acc_ref[...] = jnp.zeros_like(acc_ref)

    h = h_ref[...]
    tf = wg_ref.shape[1]
    part = None
    for c in range(FFN_CHAINS):
        cs = slice(c * tf // FFN_CHAINS, (c + 1) * tf // FFN_CHAINS)
        a = _dot(h, wg_ref[:, cs])
        b = _dot(h, wu_ref[:, cs])
        d = _dot((a * jax.nn.sigmoid(a) * b).astype(BF16), wd_ref[cs, :])
        part = d if part is None else part + d
    acc_ref[...] += part

    @pl.when(f == pl.num_programs(1) - 1)
    def _():
        o_ref[...] = x_ref[...] + acc_ref[...]


def _ffn(x, gain, wg, wu, wd):
    n = x.shape[0]
    tm = _tile(n, 512)
    tf = 512
    return pl.pallas_call(
        _ffn_kernel,
        grid=(n // tm, D_FF // tf),
        in_specs=[pl.BlockSpec((tm, D_MODEL), lambda i, f: (i, 0)),
                  pl.BlockSpec((1, D_MODEL), lambda i, f: (0, 0)),
                  pl.BlockSpec((D_MODEL, tf), lambda i, f: (0, f)),
                  pl.BlockSpec((D_MODEL, tf), lambda i, f: (0, f)),
                  pl.BlockSpec((tf, D_MODEL), lambda i, f: (f, 0))],
        out_specs=pl.BlockSpec((tm, D_MODEL), lambda i, f: (i, 0)),
        out_shape=jax.ShapeDtypeStruct((n, D_MODEL), F32),
        scratch_shapes=[pltpu.VMEM((tm, D_MODEL), BF16), pltpu.VMEM((tm, D_MODEL), F32)],
        compiler_params=_cparams("parallel", "arbitrary"),
        name="ffn",
    )(x, gain, wg, wu, wd)


def _mixer_a_kernel(u_ref, pre_ref, w_ref, y_ref, newpre_ref, ext_ref, *, rows):
    @pl.when(pl.program_id(1) == 0)
    def _():
        ext_ref[6:8, :] = pre_ref[0]

    u = u_ref[...]
    v = u[:, 512:1024] * u[:, 1024:1536]
    ext_ref[8:8 + rows, :] = v
    w = w_ref[...]
    y = ext_ref[6:6 + rows, :] * w[0:1] + ext_ref[7:7 + rows, :] * w[1:2] + v * w[2:3]
    y_ref[...] = u[:, 0:512] * y
    last = ext_ref[8 + rows - 2:8 + rows, :]
    ext_ref[6:8, :] = last
    newpre_ref[0] = last


def _mixer_a(u, prefix, w, nseq, t):
    rows = _tile(t, 512)
    nblk = t // rows
    return pl.pallas_call(
        functools.partial(_mixer_a_kernel, rows=rows),
        grid=(nseq, nblk),
        in_specs=[pl.BlockSpec((rows, 3 * D_CONV), lambda s, j: (s * nblk + j, U_A // (3 * D_CONV))),
                  pl.BlockSpec((1, CONV_A_WIDTH - 1, D_CONV), lambda s, j: (s, 0, 0)),
                  pl.BlockSpec((CONV_A_WIDTH, D_CONV), lambda s, j: (0, 0))],
        out_specs=[pl.BlockSpec((rows, D_CONV), lambda s, j: (s * nblk + j, 0)),
                   pl.BlockSpec((1, CONV_A_WIDTH - 1, D_CONV), lambda s, j: (s, 0, 0))],
        out_shape=[jax.ShapeDtypeStruct((nseq * t, D_CONV), F32),
                   jax.ShapeDtypeStruct((nseq, CONV_A_WIDTH - 1, D_CONV), F32)],
        scratch_shapes=[pltpu.VMEM((8 + rows, D_CONV), F32)],
        compiler_params=_cparams("parallel", "arbitrary"),
        name="mixer_a",
    )(u, prefix, w)


def _ssd_kernel(z_ref, xbc_ref, misc_ref, pre_ref, h0_ref, cw_ref, cb_ref, dtb_ref, alog_ref, dsk_ref,
                y_ref, newpre_ref, hout_ref, ext_ref, st_ref, pad_ref, *, lin):
    L = SSM_CHUNK
    c = pl.program_id(1)

    @pl.when(c == 0)
    def _():
        if lin < L:
            ext_ref[...] = jnp.zeros_like(ext_ref)
            pad_ref[...] = jnp.zeros_like(pad_ref)
        ext_ref[5:8, :] = pre_ref[0]
        st_ref[...] = h0_ref[0].reshape(SSM_HEADS * HEAD_DIM, SSM_STATE)

    ext_ref[8:8 + lin, :] = xbc_ref[...]
    cw = cw_ref[...]
    acc = (ext_ref[5:5 + L, :] * cw[0:1] + ext_ref[6:6 + L, :] * cw[1:2]
           + ext_ref[7:7 + L, :] * cw[2:3] + ext_ref[8:8 + L, :] * cw[3:4])
    acc = acc + cb_ref[...]
    xbc = acc * jax.nn.sigmoid(acc)
    newp = ext_ref[8 + lin - 3:8 + lin, :]
    ext_ref[5:8, :] = newp
    newpre_ref[0] = newp

    if lin < L:
        pad_ref[0:lin, :] = misc_ref[...]
        misc = pad_ref[...]
    else:
        misc = misc_ref[...]
    rows = lax.broadcasted_iota(jnp.int32, (L, L), 0)
    cols = lax.broadcasted_iota(jnp.int32, (L, L), 1)
    x = misc + dtb_ref[...]
    dt = jnp.maximum(x, 0.0) + jnp.log1p(jnp.exp(-jnp.abs(x)))
    if lin < L:
        dt = jnp.where(rows < lin, dt, 0.0)
    la = dt * (-jnp.exp(alog_ref[...]))

    causal = rows >= cols
    tril = jnp.where(causal, 1.0, 0.0).astype(BF16)
    hi, mid, lo = _split3(la)
    cum = _dot(tril, hi) + _dot(tril, mid) + _dot(tril, lo)
    cum_t = cum.T
    last = cum[L - 1:L, :]
    lo_half = cols < HEAD_DIM
    top_rows = rows < HEAD_DIM

    z = z_ref[...]
    dsk = dsk_ref[...]
    for g in range(2):
        bg = xbc[:, 512 + g * 128:512 + (g + 1) * 128].astype(BF16)
        cg = xbc[:, 768 + g * 128:768 + (g + 1) * 128].astype(BF16)
        gram = lax.dot_general(cg, bg, NT_DIMS, preferred_element_type=F32)
        for i in (2 * g, 2 * g + 1):
            a, b = 2 * i, 2 * i + 1
            sl = slice(i * LANES, (i + 1) * LANES)
            xs_p = xbc[:, sl]
            col_a = cum[:, a:a + 1]
            col_b = cum[:, b:b + 1]
            xdt = xs_p * jnp.where(lo_half, dt[:, a:a + 1], dt[:, b:b + 1])
            dec_a = jnp.exp(jnp.where(causal, col_a - cum_t[a:a + 1, :], NEG_INF))
            dec_b = jnp.exp(jnp.where(causal, col_b - cum_t[b:b + 1, :], NEG_INF))
            xa = jnp.where(lo_half, xdt, 0.0).astype(BF16)
            xb = jnp.where(lo_half, 0.0, xdt).astype(BF16)
            y_intra = _dot((gram * dec_a).astype(BF16), xa) + _dot((gram * dec_b).astype(BF16), xb)
            st = st_ref[sl, :]
            y_inter = (lax.dot_general(cg, st.astype(BF16), NT_DIMS, preferred_element_type=F32)
                       * jnp.where(lo_half, jnp.exp(col_a), jnp.exp(col_b)))
            to_end = jnp.where(lo_half, jnp.exp(last[:, a:a + 1] - col_a), jnp.exp(last[:, b:b + 1] - col_b))
            xw_t = (xdt * to_end).T.astype(BF16)
            decay = jnp.where(top_rows, jnp.exp(last[:, a:a + 1]), jnp.exp(last[:, b:b + 1]))
            st_ref[sl, :] = st * decay + _dot(xw_t, bg)
            y = y_intra + y_inter + dsk[:, sl] * xs_p
            zp = z[:, sl]
            y_ref[:, sl] = y[0:lin] * (zp * jax.nn.sigmoid(zp))

    @pl.when(c == pl.num_programs(1) - 1)
    def _():
        hout_ref[0] = st_ref[...].reshape(SSM_HEADS, HEAD_DIM, SSM_STATE)


def _ssd(u, prefix, h0, cw, cb, dtb, alog, dsk, nseq, t):
    lin = min(t, SSM_CHUNK)
    nc = t // lin
    full = lambda shape: pl.BlockSpec(shape, lambda s, c: (0,) * len(shape))
    return pl.pallas_call(
        functools.partial(_ssd_kernel, lin=lin),
        grid=(nseq, nc),
        in_specs=[pl.BlockSpec((lin, D_SSM), lambda s, c: (s * nc + c, U_Z // D_SSM)),
                  pl.BlockSpec((lin, SSM_CONV_DIM), lambda s, c: (s * nc + c, U_XBC // SSM_CONV_DIM)),
                  pl.BlockSpec((lin, LANES), lambda s, c: (s * nc + c, U_MISC // LANES)),
                  pl.BlockSpec((1, SSM_CONV_WIDTH - 1, SSM_CONV_DIM), lambda s, c: (s, 0, 0)),
                  pl.BlockSpec((1, SSM_HEADS, HEAD_DIM, SSM_STATE), lambda s, c: (s, 0, 0, 0)),
                  full((SSM_CONV_WIDTH, SSM_CONV_DIM)), full((1, SSM_CONV_DIM)),
                  full((1, LANES)), full((1, LANES)), full((1, D_SSM))],
        out_specs=[pl.BlockSpec((lin, D_SSM), lambda s, c: (s * nc + c, 0)),
                   pl.BlockSpec((1, SSM_CONV_WIDTH - 1, SSM_CONV_DIM), lambda s, c: (s, 0, 0)),
                   pl.BlockSpec((1, SSM_HEADS, HEAD_DIM, SSM_STATE), lambda s, c: (s, 0, 0, 0))],
        out_shape=[jax.ShapeDtypeStruct((nseq * t, D_SSM), F32),
                   jax.ShapeDtypeStruct((nseq, SSM_CONV_WIDTH - 1, SSM_CONV_DIM), F32),
                   jax.ShapeDtypeStruct((nseq, SSM_HEADS, HEAD_DIM, SSM_STATE), F32)],
        scratch_shapes=[pltpu.VMEM((8 + SSM_CHUNK, SSM_CONV_DIM), F32),
                        pltpu.VMEM((SSM_HEADS * HEAD_DIM, SSM_STATE), F32),
                        pltpu.VMEM((SSM_CHUNK, LANES), F32)],
        compiler_params=_cparams("parallel", "arbitrary"),
        name="ssd",
    )(u, u, u, prefix, h0, cw, cb, dtb, alog, dsk)


def _headnorm(x, mavg, gain):
    x2 = x * x
    hi = x2.astype(BF16)
    lo = (x2 - hi.astype(F32)).astype(BF16)
    ms = _dot(hi, mavg) + _dot(lo, mavg)
    return x * lax.rsqrt(ms + RMS_EPS) * gain


def _prep_kernel(q_ref, kv_ref, mavg_ref, qg_ref, kg1_ref, kg2_ref,
                 qhm_ref, nk_ref, nv_ref, kwin_ref, *maybe_kvwin_ref):
    mavg = mavg_ref[...]
    lo = _lo_half(q_ref.shape[0])
    for c in range(4):
        qn = _headnorm(q_ref[:, c * 256:(c + 1) * 256], mavg, qg_ref[...]) * ATTN_SCALE
        for cc in range(2):
            col = qn[:, cc * LANES:(cc + 1) * LANES]
            h = c * 4 + cc * 2
            qhm_ref[h] = jnp.where(lo, col, 0.0).astype(qhm_ref.dtype)
            qhm_ref[h + 1] = jnp.where(lo, _swap_halves(col), 0.0).astype(qhm_ref.dtype)
    ksel = _headnorm(kv_ref[:, 512:768], mavg, kg1_ref[...])
    kwin = _headnorm(kv_ref[:, 1024:1280], mavg, kg2_ref[...])
    nk_ref[:, 0:256] = kv_ref[:, 0:256]
    nk_ref[:, 256:512] = ksel
    nv_ref[:, 0:256] = kv_ref[:, 256:512]
    nv_ref[:, 256:512] = kv_ref[:, 768:1024]
    kwin_ref[...] = kwin
    if maybe_kvwin_ref:
        kvwin_ref, = maybe_kvwin_ref
        for c in range(q_ref.shape[0] // KEY_TILE):
            rs = slice(c * KEY_TILE, (c + 1) * KEY_TILE)
            _store_kv_t(lambda g: kvwin_ref.at[g, c], kwin[rs].T, kv_ref[rs, 1280:1536].T)


def _prep(u, mavg, qg, kg1, kg2, q_dtype, emit_window_tiles):
    n = u.shape[0]
    tm = _tile(n, 256)
    full = lambda shape: pl.BlockSpec(shape, lambda i: (0,) * len(shape))
    out_specs = [pl.BlockSpec((ATTN_HEADS, tm, LANES), lambda i: (0, i, 0)),
                 pl.BlockSpec((tm, PAGE_COLS), lambda i: (i, 0)),
                 pl.BlockSpec((tm, PAGE_COLS), lambda i: (i, 0)),
                 pl.BlockSpec((tm, 256), lambda i: (i, 0))]
    out_shape = [jax.ShapeDtypeStruct((ATTN_HEADS, n, LANES), q_dtype),
                 jax.ShapeDtypeStruct((n, PAGE_COLS), F32),
                 jax.ShapeDtypeStruct((n, PAGE_COLS), F32),
                 jax.ShapeDtypeStruct((n, 256), F32)]
    if emit_window_tiles:
        out_specs.append(pl.BlockSpec((KV_HEADS, tm // KEY_TILE, 2 * HEAD_DIM, KEY_TILE), lambda i: (0, i, 0, 0)))
        out_shape.append(jax.ShapeDtypeStruct((KV_HEADS, n // KEY_TILE, 2 * HEAD_DIM, KEY_TILE), BF16))
    return pl.pallas_call(
        _prep_kernel,
        grid=(n // tm,),
        in_specs=[pl.BlockSpec((tm, D_ATTN), lambda i: (i, U_Q // D_ATTN)),
                  pl.BlockSpec((tm, KV_COLS), lambda i: (i, 0)),
                  full((256, 256)), full((1, 256)), full((1, 256)), full((1, 256))],
        out_specs=out_specs,
        out_shape=out_shape,
        compiler_params=_cparams("parallel"),
        name="nsa_prep",
    )(u, u, mavg, qg, kg1, kg2)


PAGES_PER_STEP = 16


def _gelu_tanh(x):
    return 0.5 * x * (1.0 + jnp.tanh(0.7978845608028654 * (x + 0.044715 * (x * x * x))))


def _pagepass_kernel(pt_ref, *refs, nchunk, feature_major):
    pp = PAGES_PER_STEP

    pi = lax.broadcasted_iota(jnp.int32, (PAGE_SIZE, PAGE_SIZE), 0)
    pk = lax.broadcasted_iota(jnp.int32, (PAGE_SIZE, PAGE_SIZE), 1)
    perm = jnp.where(pk == CMP_STRIDE * (pi & 7) + (pi >> 3), 1.0, 0.0).astype(BF16)

    def cmp_rows_permuted(page, c):
        if feature_major:
            return lax.dot_general(perm, page[0, c * LANES:(c + 1) * LANES, :].astype(BF16), NT_DIMS,
                                   preferred_element_type=F32)
        return _dot(perm, page[0, :, c * LANES:(c + 1) * LANES].astype(BF16))

    def sel_t(page):
        return page[0, 256:512, :] if feature_major else page[0, :, 256:512].T

    kpages = refs[0:pp]
    vpages = refs[pp:2 * pp]
    pos_ref, w1k_ref, w1v_ref, w2k_ref, w2v_ref, kg_ref = refs[2 * pp:2 * pp + 6]
    kvcmp_ref, kvsel_ref = refs[2 * pp + 6:2 * pp + 8]
    xk_ref, xv_ref, hs_ref = refs[2 * pp + 8:]
    j = pl.program_id(1)
    lo16 = _lo_half(16)

    for i2 in range(pp // 2):
        r0 = pl.multiple_of((j * pp + 2 * i2) * 8, 16)
        for pages, xs_ref in ((kpages, xk_ref), (vpages, xv_ref)):
            for c in range(2):
                pa = cmp_rows_permuted(pages[2 * i2], c)
                pb = cmp_rows_permuted(pages[2 * i2 + 1], c)
                for a in range(8):
                    ev, od = slice(16 * a, 16 * a + 8), slice(16 * a + 8, 16 * a + 16)
                    ec = jnp.concatenate([pa[ev], pb[ev]], axis=0)
                    oc = jnp.concatenate([pa[od], pb[od]], axis=0)
                    xs_ref[2 * c, pl.ds(r0, 16), a * LANES:(a + 1) * LANES] = (
                        jnp.where(lo16, ec, _swap_halves(oc)).astype(BF16))
                    xs_ref[2 * c + 1, pl.ds(r0, 16), a * LANES:(a + 1) * LANES] = (
                        jnp.where(lo16, _swap_halves(ec), oc).astype(BF16))

    per_chunk = SEL_CHUNK // PAGE_SIZE
    for i in range(pp):
        ls = slice((i % per_chunk) * PAGE_SIZE, (i % per_chunk + 1) * PAGE_SIZE)
        _store_kv_t(lambda g: kvsel_ref.at[0, g, i // per_chunk, :, ls], sel_t(kpages[i]), sel_t(vpages[i]))

    @pl.when(j == pl.num_programs(1) - 1)
    def _():
        for slot in range(2 * KV_HEADS):
            hs_ref[slot, nchunk:nchunk + 8, :] = jnp.zeros((8, 256), F32)
        pos = pos_ref[...]
        phi = pos.astype(BF16)
        plo = (pos - phi.astype(F32)).astype(BF16)
        bias_k = _dot(phi, w1k_ref[...]) + _dot(plo, w1k_ref[...])
        bias_v = _dot(phi, w1v_ref[...]) + _dot(plo, w1v_ref[...])
        bias_k = bias_k[0:1, 0:128] + bias_k[1:2, 128:256]
        bias_v = bias_v[2:3, 0:128] + bias_v[3:4, 128:256]

        for g in range(KV_HEADS):
            hs_ref[2 * g, 0:nchunk, :] = _dot(xk_ref[g], w1k_ref[...])
            hs_ref[2 * g + 1, 0:nchunk, :] = _dot(xv_ref[g], w1v_ref[...])

        def summarise(slot, bias, w2_ref):
            pre = hs_ref[slot, 0:nchunk, 0:128] + hs_ref[slot, 1:nchunk + 1, 128:256] + bias
            return _dot(_gelu_tanh(pre).astype(BF16), w2_ref[...])

        for g in range(KV_HEADS):
            ko = summarise(2 * g, bias_k, w2k_ref)
            vo = summarise(2 * g + 1, bias_v, w2v_ref)
            ms = jnp.sum(ko * ko, axis=-1, keepdims=True) * (1.0 / HEAD_DIM)
            kvcmp_ref[0, g] = (ko * lax.rsqrt(ms + RMS_EPS) * kg_ref[...] + vo).astype(BF16)


def _pagepass(ptab, kpages, vpages, pos, w1k, w1v, w2k, w2v, kg, nseq, npages):
    pp = PAGES_PER_STEP
    nsteps = npages // pp
    nchunk = npages * PAGE_SIZE // CMP_STRIDE
    feature_major = kpages.shape[1] == PAGE_COLS
    chunks_per_step = pp * PAGE_SIZE // SEL_CHUNK

    def page_spec(i):
        return pl.BlockSpec((1,) + kpages.shape[1:],
                            lambda s, j, pt: (pt[s * npages + j * pp + i], 0, 0))

    full = lambda shape: pl.BlockSpec(shape, lambda s, j, pt: (0,) * len(shape))
    grid_spec = pltpu.PrefetchScalarGridSpec(
        num_scalar_prefetch=1,
        grid=(nseq, nsteps),
        in_specs=([page_spec(i) for i in range(pp)] + [page_spec(i) for i in range(pp)]
                  + [full((8, 1024)), full((1024, 256)), full((1024, 256)),
                     full((128, 128)), full((128, 128)), full((1, 128))]),
        out_specs=[pl.BlockSpec((1, KV_HEADS, nchunk, LANES), lambda s, j, pt: (s, 0, 0, 0)),
                   pl.BlockSpec((1, KV_HEADS, chunks_per_step, 2 * HEAD_DIM, SEL_CHUNK),
                                lambda s, j, pt: (s, 0, j, 0, 0))],
        scratch_shapes=[pltpu.VMEM((KV_HEADS, nchunk, 1024), BF16),
                        pltpu.VMEM((KV_HEADS, nchunk, 1024), BF16),
                        pltpu.VMEM((2 * KV_HEADS, nchunk + 8, 256), F32)],
    )
    return pl.pallas_call(
        functools.partial(_pagepass_kernel, nchunk=nchunk, feature_major=feature_major),
        grid_spec=grid_spec,
        out_shape=[jax.ShapeDtypeStruct((nseq, KV_HEADS, nchunk, LANES), BF16),
                   jax.ShapeDtypeStruct((nseq, KV_HEADS, npages * PAGE_SIZE // SEL_CHUNK, 2 * HEAD_DIM, SEL_CHUNK),
                                        BF16)],
        compiler_params=_cparams("parallel", "arbitrary"),
        name="pagepass",
    )(ptab, *([kpages] * pp), *([vpages] * pp), pos, w1k, w1v, w2k, w2v, kg)


WIN_TILES = (WINDOW + Q_BLOCK) // KEY_TILE


def _winpack_kernel(wk_ref, wv_ref, kwn_ref, vwn_ref, ksn_ref, vsn_ref, kvw_ref, tail_ref):
    t = kwn_ref.shape[0]
    past_tiles = WINDOW // KEY_TILE
    for c in range(past_tiles):
        ls = slice(c * KEY_TILE, (c + 1) * KEY_TILE)
        _store_kv_t(lambda g: kvw_ref.at[0, g, c], wk_ref[0, :, ls], wv_ref[0, :, ls])
    pad_t = lambda ref, n: jnp.concatenate([ref[...], jnp.zeros((n - t, 256), F32)], axis=0).T
    _store_kv_t(lambda g: kvw_ref.at[0, g, past_tiles], pad_t(kwn_ref, KEY_TILE), pad_t(vwn_ref, KEY_TILE))
    _store_kv_t(lambda g: tail_ref.at[0, g], pad_t(ksn_ref, SEL_CHUNK), pad_t(vsn_ref, SEL_CHUNK))


def _winpack(wk_t, wv_t, kwin, u, new_k, new_v, nseq, t):
    tile = (1, KV_HEADS, 2 * HEAD_DIM, SEL_CHUNK)
    return pl.pallas_call(
        _winpack_kernel,
        grid=(nseq,),
        in_specs=[pl.BlockSpec((1, 256, WINDOW), lambda s: (s, 0, 0)),
                  pl.BlockSpec((1, 256, WINDOW), lambda s: (s, 0, 0)),
                  pl.BlockSpec((t, 256), lambda s: (s, 0)),
                  pl.BlockSpec((t, 256), lambda s: (s, 5)),
                  pl.BlockSpec((t, 256), lambda s: (s, 1)),
                  pl.BlockSpec((t, 256), lambda s: (s, 1))],
        out_specs=[pl.BlockSpec((1, KV_HEADS, WIN_TILES, 2 * HEAD_DIM, KEY_TILE), lambda s: (s, 0, 0, 0, 0)),
                   pl.BlockSpec(tile, lambda s: (s, 0, 0, 0))],
        out_shape=[jax.ShapeDtypeStruct((nseq, KV_HEADS, WIN_TILES, 2 * HEAD_DIM, KEY_TILE), BF16),
                   jax.ShapeDtypeStruct((nseq,) + tile[1:], BF16)],
        compiler_params=_cparams("parallel"),
        name="winpack",
    )(wk_t, wv_t, kwin, u, new_k, new_v)


MASK_BIAS = -2.0 ** 100


def _attn_tables(slopes, n_sel_chunks, n_blocks):
    slope_tab = jnp.broadcast_to(slopes[:, None, None], (ATTN_HEADS, 8, LANES))
    pos = (jnp.arange(n_sel_chunks, dtype=jnp.int32)[:, None, None] * SEL_CHUNK
           + jnp.arange(SEL_CHUNK, dtype=jnp.int32)[None, None, :])
    blk = jnp.arange(n_blocks, dtype=jnp.int32)[None, :, None]
    esel = jnp.where(blk == (pos >> 6), MASK_BIAS, 0.0).astype(BF16)
    lane_src = jnp.arange(LANES)[None, :, None]
    gate_id = jnp.arange(12)[None, None, :]
    g_id = jnp.arange(KV_HEADS)[:, None, None]
    onehot = (lane_src == GATE_LANE0 + g_id * 12 + gate_id).astype(BF16)
    gexp = jnp.broadcast_to(onehot[..., None], (KV_HEADS, LANES, 12, LANES)).reshape(KV_HEADS, LANES, 12 * LANES)
    return slope_tab, esel, gexp


def _top_blocks_unselected(val, valid):
    tq, nselp = val.shape
    if tq < LANES:
        val = jnp.concatenate([val, jnp.zeros((LANES - tq, nselp), F32)], axis=0)
    val_t = jnp.concatenate([val[:, c * LANES:(c + 1) * LANES].T for c in range(nselp // LANES)], axis=0)
    blk_t = lax.broadcasted_iota(jnp.int32, (nselp, LANES), 0).astype(F32)

    def pick_top(_, carry):
        v, sel = carry
        best = jnp.max(v, axis=0, keepdims=True)
        idx = jnp.min(jnp.where(v == best, blk_t, 1e9), axis=0, keepdims=True)
        pick = blk_t == idx
        return jnp.where(pick, -3e38, v), jnp.where(pick, 1.0, sel)

    _, sel_t = lax.fori_loop(0, TOP_BLOCKS, pick_top, (val_t, jnp.zeros((nselp, LANES), F32)), unroll=True)
    sel = jnp.concatenate([sel_t[c * LANES:(c + 1) * LANES].T for c in range(nselp // LANES)], axis=1)
    return jnp.where(valid & (sel[0:tq] > 0.5), 0.0, 1.0)


CHUNKS_PER_WORD = 8
SEL_STREAMS = 2


def _merge_streams(m_ref, l_ref, acc_of):
    m = m_ref[0]
    for st in range(1, SEL_STREAMS):
        m = jnp.maximum(m, m_ref[st])
    l_sum = None
    acc = None
    for st in range(SEL_STREAMS):
        w = jnp.exp(m_ref[st] - m)
        l_sum = w * l_ref[st] if l_sum is None else l_sum + w * l_ref[st]
        acc = w * acc_of(st) if acc is None else acc + w * acc_of(st)
    return acc / jnp.maximum(jnp.sum(l_sum, axis=-1, keepdims=True), TINY)


def _list_needed_chunks(unsel, n_chunks, words_ref, list_ref):
    nselp = unsel.shape[1]
    sel_any = jnp.max(1.0 - unsel, axis=0, keepdims=True)
    lane = lax.broadcasted_iota(jnp.int32, (1, nselp), 1)
    chunk_shift = (SEL_CHUNK // SEL_BLOCK).bit_length() - 1
    word_shift = chunk_shift + CHUNKS_PER_WORD.bit_length() - 1
    digit = (lane >> chunk_shift) & (CHUNKS_PER_WORD - 1)
    weight = lax.shift_left(jnp.ones_like(lane), 3 * digit).astype(F32)
    for w in range(words_ref.shape[0]):
        in_word = (lane >> word_shift) == w
        words_ref[w] = jnp.sum(jnp.where(in_word, sel_any * weight, 0.0)).astype(jnp.int32)

    def build(kc, cnt):
        used = (words_ref[kc // CHUNKS_PER_WORD] >> (3 * (kc % CHUNKS_PER_WORD))) & 7
        list_ref[cnt] = kc
        return cnt + jnp.where(used > 0, 1, 0)

    return lax.fori_loop(0, n_chunks, build, 0)


def _attn_kernel(q_ref, qnext_ref, misc_ref, kvc_ref, kvs_ref, kvw_ref, slope_ref, esel_ref, gexp_ref,
                 o_ref, m_ref, l_ref, acc_ref, lhs_ref, ocmp_ref, unsel_ref, words_ref, list_ref,
                 *, tq, nchunk):
    nselp = LANES
    step = pl.program_id(2)
    qs = step * tq
    m_rows = Q_PER_KV * tq
    qb = q_ref[...].reshape(m_rows, LANES).astype(BF16)
    tpos = qs + lax.broadcasted_iota(jnp.int32, (tq, 1), 0)

    def per_head(x):
        return jnp.concatenate([x] * Q_PER_KV, axis=0)

    def add_by_head(s, fn):
        return jnp.concatenate([s[r * tq:(r + 1) * tq] + fn(r) for r in range(Q_PER_KV)], axis=0)

    def slope(r):
        return slope_ref[r, 0:1, 0:1]

    def choose_blocks(qblk_ref, start):
        q = qblk_ref[...].reshape(m_rows, LANES)
        pos_q = start + lax.broadcasted_iota(jnp.int32, (tq, 1), 0)
        kvc = kvc_ref[0, 0]
        cmp_end = lax.broadcasted_iota(jnp.int32, (tq, nchunk), 1) * CMP_STRIDE + (2 * CMP_STRIDE - 1)
        cmp_mask = per_head(pos_q >= cmp_end)
        cmp_pos = (lax.broadcasted_iota(jnp.int32, (1, nchunk), 1) * CMP_STRIDE
                   + (2 * CMP_STRIDE - 1)).astype(F32)
        s = lax.dot_general(q.astype(BF16), kvc, NT_DIMS, preferred_element_type=F32)
        s = jnp.where(cmp_mask, add_by_head(s, lambda r: slope(r) * cmp_pos), NEG_INF)
        p = jnp.exp(s - jnp.max(s, axis=-1, keepdims=True)) * jnp.where(cmp_mask, 1.0, 0.0)
        p = p / jnp.maximum(jnp.sum(p, axis=-1, keepdims=True), TINY)
        ocmp_ref[...] = _dot(p.astype(BF16), kvc)
        psum = p[0:tq] + p[tq:2 * tq] + p[2 * tq:3 * tq] + p[3 * tq:4 * tq]
        pool = jnp.where((lax.broadcasted_iota(jnp.int32, (nchunk, nselp), 0) >> 2)
                         == lax.broadcasted_iota(jnp.int32, (nchunk, nselp), 1), 1.0, 0.0).astype(BF16)
        hi, mid, lo = _split3(psum)
        imp = _dot(hi, pool) + _dot(mid, pool) + _dot(lo, pool)
        blk = lax.broadcasted_iota(jnp.int32, (tq, nselp), 1)
        cur = pos_q >> 6
        forced = (blk == 0) | (blk == cur) | (blk == cur - 1)
        valid = blk <= cur
        val = jnp.where(valid, imp + jnp.where(forced, FORCE_BONUS, 0.0), NEG_INF)
        unsel = _top_blocks_unselected(val, valid)
        unsel_ref[...] = unsel
        lhs_ref[...] = jnp.where(_lo_half(m_rows), q.astype(F32), per_head(_swap_halves(unsel))).astype(BF16)

    @pl.when(step == 0)
    def _():
        choose_blocks(q_ref, qs)

    m_ref[...] = jnp.full_like(m_ref, NEG_INF)
    l_ref[...] = jnp.zeros_like(l_ref)
    acc_ref[...] = jnp.zeros_like(acc_ref)

    def sel_chunk(stream, kv_t, e_c, k0, causal_bias, lhs=None):
        sel_update(stream, kv_t, sel_scores(kv_t, e_c, k0, causal_bias, lhs))

    def sel_scores(kv_t, e_c, k0, causal_bias=None, lhs=None):
        n = kv_t.shape[1]
        lhs = lhs_ref[...] if lhs is None else lhs
        s = _dot(lhs, jnp.concatenate([kv_t[0:HEAD_DIM], e_c], axis=0))
        pos = (k0 + lax.broadcasted_iota(jnp.int32, (1, n), 1)).astype(F32)
        if causal_bias is None:
            return add_by_head(s, lambda r: slope(r) * pos)
        return add_by_head(s, lambda r: slope(r) * pos + causal_bias)

    def sel_update(stream, kv_t, s):
        m_old = m_ref[stream]
        m_new = jnp.maximum(m_old, jnp.max(s, axis=-1, keepdims=True))
        alpha = jnp.exp(m_old - m_new)
        p = [jnp.exp(s[:, c * LANES:(c + 1) * LANES] - m_new) for c in range(s.shape[1] // LANES)]
        part = p[0]
        for c in range(1, len(p)):
            part = part + p[c]
        l_ref[stream] = alpha * l_ref[stream] + part
        acc_ref[stream] = alpha * acc_ref[stream] + lax.dot_general(
            jnp.concatenate(p, axis=1).astype(BF16), kv_t, NT_DIMS, preferred_element_type=F32)
        m_ref[stream] = m_new

    def past_chunk(stream, slot):
        kc = list_ref[slot]
        sel_chunk(stream, kvs_ref[0, 0, kc], esel_ref[kc], kc * SEL_CHUNK, None)

    def group(i, carry):
        kcs = [list_ref[SEL_STREAMS * i + stream] for stream in range(SEL_STREAMS)]
        scores = [sel_scores(kvs_ref[0, 0, kc], esel_ref[kc], kc * SEL_CHUNK) for kc in kcs]
        for stream in range(SEL_STREAMS):
            sel_update(stream, kvs_ref[0, 0, kcs[stream]], scores[stream])
        return carry

    def single(i, carry):
        past_chunk(0, n_need - 1 - i)
        return carry

    n_full = qs // SEL_CHUNK
    n_need = _list_needed_chunks(unsel_ref[...], n_full, words_ref, list_ref)
    lax.fori_loop(0, n_need // SEL_STREAMS, group, 0)
    lax.fori_loop(0, n_need % SEL_STREAMS, single, 0)

    lhs_now = lhs_ref[...]
    o_cmp = ocmp_ref[...]
    next_step = jnp.minimum(step + 1, pl.num_programs(2) - 1)
    choose_blocks(qnext_ref, next_step * tq)

    key_j = lax.broadcasted_iota(jnp.int32, (tq, SEL_CHUNK), 1)
    causal = jnp.where(tpos >= n_full * SEL_CHUNK + key_j, 0.0, NEG_INF)
    diag_scores = sel_scores(kvs_ref[0, 0, n_full], esel_ref[n_full], n_full * SEL_CHUNK, causal, lhs_now)

    tile0 = jnp.maximum(qs - WINDOW, 0) // KEY_TILE
    tiles = [kvw_ref[0, 0, tile0 + j] for j in range(WIN_TILES)]
    key_b = lax.broadcasted_iota(jnp.int32, (1, KEY_TILE), 1)
    s = []
    for j in range(WIN_TILES):
        pos = (tile0 + j) * KEY_TILE + key_b
        dist = tpos - pos
        bias = jnp.where((dist >= 0) & (dist <= WINDOW), 0.0, NEG_INF)
        s.append(add_by_head(_dot(qb, tiles[j]), lambda r: slope(r) * pos.astype(F32) + bias))
    sel_update(SEL_STREAMS - 1, kvs_ref[0, 0, n_full], diag_scores)
    o_sel = _merge_streams(m_ref, l_ref, lambda st: acc_ref[st])
    m = s[0]
    for j in range(1, WIN_TILES):
        m = jnp.maximum(m, s[j])
    m = jnp.max(m, axis=-1, keepdims=True)
    p = [jnp.exp(sj - m) for sj in s]
    part = p[0]
    for j in range(1, WIN_TILES):
        part = part + p[j]
    o_win = lax.dot_general(p[0].astype(BF16), tiles[0], NT_DIMS, preferred_element_type=F32)
    for j in range(1, WIN_TILES):
        o_win = o_win + lax.dot_general(p[j].astype(BF16), tiles[j], NT_DIMS, preferred_element_type=F32)
    o_win = o_win / jnp.maximum(jnp.sum(part, axis=-1, keepdims=True), TINY)

    hi, mid, lo = _split3(jax.nn.sigmoid(misc_ref[...]))
    gexp = gexp_ref[0]
    gates = _dot(hi, gexp) + _dot(mid, gexp) + _dot(lo, gexp)
    comb = []
    for r in range(Q_PER_KV):
        rs = slice(r * tq, (r + 1) * tq)
        gate = [gates[:, (3 * r + c) * LANES:(3 * r + c + 1) * LANES] for c in range(3)]
        comb.append(gate[0] * o_cmp[rs] + gate[1] * o_sel[rs] + gate[2] * o_win[rs])
    lo_half = _lo_half(tq)
    o_ref[:, 0:LANES] = jnp.where(lo_half, _swap_halves(comb[0]), comb[1])
    o_ref[:, LANES:2 * LANES] = jnp.where(lo_half, _swap_halves(comb[2]), comb[3])


def _attn(tables, qhm, u, kvcmp, kvsel, kvwin, nseq, t):
    slope_tab, esel, gexp = tables
    tq = Q_BLOCK
    nqb = t // tq
    nchunk = kvcmp.shape[2]
    sel_chunks = kvsel.shape[2]
    win_tiles = kvwin.shape[2]
    m_rows = Q_PER_KV * tq
    assert esel.shape[1] == HEAD_DIM
    in_specs = [pl.BlockSpec((Q_PER_KV, tq, LANES), lambda s, g, i: (g, s * nqb + i, 0)),
                pl.BlockSpec((Q_PER_KV, tq, LANES), lambda s, g, i: (g, s * nqb + jnp.minimum(i + 1, nqb - 1), 0)),
                pl.BlockSpec((tq, LANES), lambda s, g, i: (s * nqb + i, U_MISC // LANES)),
                pl.BlockSpec((1, 1, nchunk, LANES), lambda s, g, i: (s, g, 0, 0)),
                pl.BlockSpec((1, 1, sel_chunks, 2 * HEAD_DIM, SEL_CHUNK), lambda s, g, i: (s, g, 0, 0, 0)),
                pl.BlockSpec((1, 1, win_tiles, 2 * HEAD_DIM, KEY_TILE), lambda s, g, i: (g, s, 0, 0, 0)),
                pl.BlockSpec((Q_PER_KV, 8, LANES), lambda s, g, i: (g, 0, 0)),
                pl.BlockSpec(esel.shape, lambda s, g, i: (0, 0, 0)),
                pl.BlockSpec((1, LANES, 12 * LANES), lambda s, g, i: (g, 0, 0))]
    args = [qhm, qhm, u, kvcmp, kvsel, kvwin, slope_tab, esel, gexp]
    stream_state = pltpu.VMEM((SEL_STREAMS, m_rows, LANES), F32)
    return pl.pallas_call(
        functools.partial(_attn_kernel, tq=tq, nchunk=nchunk),
        grid=(nseq, KV_HEADS, nqb),
        in_specs=in_specs,
        out_specs=pl.BlockSpec((tq, 2 * LANES), lambda s, g, i: (s * nqb + i, g)),
        out_shape=jax.ShapeDtypeStruct((nseq * t, D_ATTN), F32),
        scratch_shapes=[stream_state, stream_state, stream_state,
                        pltpu.VMEM((m_rows, LANES), BF16), pltpu.VMEM((m_rows, LANES), F32),
                        pltpu.VMEM((tq, LANES), F32),
                        pltpu.SMEM((-(-sel_chunks // CHUNKS_PER_WORD),), jnp.int32),
                        pltpu.SMEM((sel_chunks,), jnp.int32)],
        compiler_params=_cparams("parallel", "parallel", "arbitrary"),
        name="nsa_attn",
    )(*args)


def _attn_sample_kernel(q_ref, misc_ref, kvc_ref, kvs_ref, kvw_ref, tail_ref, slope_ref, esel_ref, gexp_ref,
                        o_ref, m_ref, l_ref, acc_ref, words_ref, list_ref, *, tq, q0, nchunk, nselp, lmain):
    n_rows = ATTN_HEADS * tq
    grp_rows = Q_PER_KV * tq
    wide = KV_HEADS * LANES
    q = q_ref[...].reshape(n_rows, LANES)
    qb = q.astype(BF16)
    tpos = q0 + lax.broadcasted_iota(jnp.int32, (tq, 1), 0)
    slope_col = jnp.concatenate([jnp.broadcast_to(slope_ref[h, 0:1, 0:1], (tq, 1)) for h in range(ATTN_HEADS)],
                                axis=0)
    grp_shift = grp_rows.bit_length() - 1
    own = ((lax.broadcasted_iota(jnp.int32, (n_rows, wide), 1) >> 7)
           == (lax.broadcasted_iota(jnp.int32, (n_rows, wide), 0) >> grp_shift))
    q_diag = jnp.where(own, jnp.concatenate([q.astype(F32)] * KV_HEADS, axis=1), 0.0).astype(BF16)
    row_grp = lax.broadcasted_iota(jnp.int32, (n_rows, LANES), 0) >> grp_shift

    def per_rows(x, copies):
        return jnp.concatenate([x] * copies, axis=0)

    def own_block(x):
        out = x[:, 0:LANES]
        for g in range(1, KV_HEADS):
            out = jnp.where(row_grp == g, x[:, g * LANES:(g + 1) * LANES], out)
        return out

    cmp_end = lax.broadcasted_iota(jnp.int32, (tq, nchunk), 1) * CMP_STRIDE + (2 * CMP_STRIDE - 1)
    cmp_mask = per_rows(tpos >= cmp_end, ATTN_HEADS)
    cmp_pos = (lax.broadcasted_iota(jnp.int32, (1, nchunk), 1) * CMP_STRIDE + (2 * CMP_STRIDE - 1)).astype(F32)
    grp = lambda x, g: x[g * grp_rows:(g + 1) * grp_rows]
    s = jnp.concatenate([lax.dot_general(grp(qb, g), kvc_ref[0, g], NT_DIMS, preferred_element_type=F32)
                         for g in range(KV_HEADS)], axis=0)
    s = jnp.where(cmp_mask, s + slope_col * cmp_pos, NEG_INF)
    p = jnp.exp(s - jnp.max(s, axis=-1, keepdims=True)) * jnp.where(cmp_mask, 1.0, 0.0)
    p = p / jnp.maximum(jnp.sum(p, axis=-1, keepdims=True), TINY)
    o_cmp = jnp.concatenate([_dot(grp(p, g).astype(BF16), kvc_ref[0, g]) for g in range(KV_HEADS)], axis=0)
    head_p = lambda h: p[h * tq:(h + 1) * tq]
    psum = jnp.concatenate(
        [head_p(4 * g) + head_p(4 * g + 1) + head_p(4 * g + 2) + head_p(4 * g + 3) for g in range(KV_HEADS)],
        axis=0)

    key_b = lax.broadcasted_iota(jnp.int32, (1, KEY_TILE), 1)
    tiles = [kvw_ref[0, :, j].reshape(wide, KEY_TILE) for j in range(WIN_TILES)]
    s = []
    for j in range(WIN_TILES):
        pos = q0 - WINDOW + j * KEY_TILE + key_b
        dist = tpos - pos
        bias = per_rows(jnp.where((dist >= 0) & (dist <= WINDOW), 0.0, NEG_INF), ATTN_HEADS)
        s.append(_dot(q_diag, tiles[j]) + slope_col * pos.astype(F32) + bias)
    m = s[0]
    for j in range(1, WIN_TILES):
        m = jnp.maximum(m, s[j])
    m = jnp.max(m, axis=-1, keepdims=True)
    p = [jnp.exp(sj - m) for sj in s]
    part = p[0]
    for j in range(1, WIN_TILES):
        part = part + p[j]
    o_win = lax.dot_general(p[0].astype(BF16), tiles[0], NT_DIMS, preferred_element_type=F32)
    for j in range(1, WIN_TILES):
        o_win = o_win + lax.dot_general(p[j].astype(BF16), tiles[j], NT_DIMS, preferred_element_type=F32)
    o_win = own_block(o_win) / jnp.maximum(jnp.sum(part, axis=-1, keepdims=True), TINY)

    pool = jnp.where((lax.broadcasted_iota(jnp.int32, (nchunk, nselp), 0) >> 2)
                     == lax.broadcasted_iota(jnp.int32, (nchunk, nselp), 1), 1.0, 0.0).astype(BF16)
    hi, mid, lo = _split3(psum)
    imp = _dot(hi, pool) + _dot(mid, pool) + _dot(lo, pool)
    blk = lax.broadcasted_iota(jnp.int32, (grp_rows, nselp), 1)
    cur = per_rows(tpos, KV_HEADS) >> 6
    forced = (blk == 0) | (blk == cur) | (blk == cur - 1)
    valid = blk <= cur
    val = jnp.where(valid, imp + jnp.where(forced, FORCE_BONUS, 0.0), NEG_INF)
    unsel = _top_blocks_unselected(val, valid)
    unsel_rows = jnp.concatenate([unsel[(h // Q_PER_KV) * tq:(h // Q_PER_KV + 1) * tq]
                                  for h in range(ATTN_HEADS)], axis=0).astype(BF16)

    m_ref[...] = jnp.full_like(m_ref, NEG_INF)
    l_ref[...] = jnp.zeros_like(l_ref)
    acc_ref[...] = jnp.zeros_like(acc_ref)

    def sel_chunk(stream, kv_t, e_c, k0, causal_bias):
        sel_update(stream, kv_t, sel_scores(kv_t, e_c, k0, causal_bias))

    def sel_scores(kv_t, e_c, k0, causal_bias=None):
        pos = (k0 + lax.broadcasted_iota(jnp.int32, (1, kv_t.shape[1]), 1)).astype(F32)
        s = _dot(q_diag, kv_t) + _dot(unsel_rows, e_c) + slope_col * pos
        return s if causal_bias is None else s + causal_bias

    def sel_update(stream, kv_t, s):
        m_old = m_ref[stream]
        m_new = jnp.maximum(m_old, jnp.max(s, axis=-1, keepdims=True))
        alpha = jnp.exp(m_old - m_new)
        p = [jnp.exp(s[:, c * LANES:(c + 1) * LANES] - m_new) for c in range(s.shape[1] // LANES)]
        part = p[0]
        for c in range(1, len(p)):
            part = part + p[c]
        l_ref[stream] = alpha * l_ref[stream] + part
        acc_ref[stream] = jnp.concatenate([alpha] * KV_HEADS, axis=1) * acc_ref[stream] + lax.dot_general(
            jnp.concatenate(p, axis=1).astype(BF16), kv_t, NT_DIMS, preferred_element_type=F32)
        m_ref[stream] = m_new

    def past_chunk(stream, slot):
        kc = list_ref[slot]
        sel_chunk(stream, kvs_ref[0, :, kc].reshape(wide, SEL_CHUNK), esel_ref[kc], kc * SEL_CHUNK, None)

    def group(i, carry):
        kcs = [list_ref[SEL_STREAMS * i + stream] for stream in range(SEL_STREAMS)]
        tiles_t = [kvs_ref[0, :, kc].reshape(wide, SEL_CHUNK) for kc in kcs]
        scores = [sel_scores(kv_t, esel_ref[kc], kc * SEL_CHUNK) for kv_t, kc in zip(tiles_t, kcs)]
        for stream in range(SEL_STREAMS):
            sel_update(stream, tiles_t[stream], scores[stream])
        return carry

    def single(i, carry):
        past_chunk(0, n_need - 1 - i)
        return carry

    n_need = _list_needed_chunks(unsel, lmain // SEL_CHUNK, words_ref, list_ref)
    lax.fori_loop(0, n_need // SEL_STREAMS, group, 0)
    lax.fori_loop(0, n_need % SEL_STREAMS, single, 0)

    key_j = lax.broadcasted_iota(jnp.int32, (tq, SEL_CHUNK), 1)
    causal = per_rows(jnp.where(tpos >= lmain + key_j, 0.0, NEG_INF), ATTN_HEADS)
    sel_chunk(SEL_STREAMS - 1, tail_ref[0].reshape(wide, SEL_CHUNK), esel_ref[lmain // SEL_CHUNK], lmain, causal)
    o_sel = _merge_streams(m_ref, l_ref, lambda st: own_block(acc_ref[st]))

    hi, mid, lo = _split3(jax.nn.sigmoid(misc_ref[...]))
    comb = []
    for g in range(KV_HEADS):
        gates = _dot(hi, gexp_ref[g]) + _dot(mid, gexp_ref[g]) + _dot(lo, gexp_ref[g])
        for r in range(Q_PER_KV):
            hs = slice((g * Q_PER_KV + r) * tq, (g * Q_PER_KV + r + 1) * tq)
            gate = [gates[:, (3 * r + c) * LANES:(3 * r + c + 1) * LANES] for c in range(3)]
            comb.append(gate[0] * o_cmp[hs] + gate[1] * o_sel[hs] + gate[2] * o_win[hs])
    lo_half = _lo_half(tq)
    for i in range(ATTN_HEADS // 2):
        o_ref[:, i * LANES:(i + 1) * LANES] = jnp.where(lo_half, _swap_halves(comb[2 * i]), comb[2 * i + 1])


def _attn_sample(tables, qhm, u, kvcmp, kvsel, kvwin, tail, nseq, t, q0):
    slope_tab, esel, gexp = tables
    nchunk = kvcmp.shape[2]
    sel_chunks = kvsel.shape[2]
    n_rows = ATTN_HEADS * t
    full = lambda a: pl.BlockSpec(a.shape, lambda s: (0,) * a.ndim)
    per_seq = lambda a: pl.BlockSpec((1,) + a.shape[1:], lambda s: (s,) + (0,) * (a.ndim - 1))
    wide_state = pltpu.VMEM((SEL_STREAMS, n_rows, KV_HEADS * LANES), F32)
    lane_state = pltpu.VMEM((SEL_STREAMS, n_rows, LANES), F32)
    return pl.pallas_call(
        functools.partial(_attn_sample_kernel, tq=t, q0=q0, nchunk=nchunk, nselp=esel.shape[1],
                          lmain=sel_chunks * SEL_CHUNK),
        grid=(nseq,),
        in_specs=[pl.BlockSpec((ATTN_HEADS, t, LANES), lambda s: (0, s, 0)),
                  pl.BlockSpec((t, LANES), lambda s: (s, U_MISC // LANES)),
                  per_seq(kvcmp), per_seq(kvsel), per_seq(kvwin), per_seq(tail),
                  full(slope_tab), full(esel), full(gexp)],
        out_specs=pl.BlockSpec((t, D_ATTN), lambda s: (s, 0)),
        out_shape=jax.ShapeDtypeStruct((nseq * t, D_ATTN), F32),
        scratch_shapes=[lane_state, lane_state, wide_state,
                        pltpu.SMEM((-(-sel_chunks // CHUNKS_PER_WORD),), jnp.int32),
                        pltpu.SMEM((sel_chunks,), jnp.int32)],
        compiler_params=_cparams("parallel"),
        name="nsa_attn_sample",
    )(qhm, u, kvcmp, kvsel, kvwin, tail, slope_tab, esel, gexp)


def _layer_params(l, norm_mix, w_in, conv_a_w, conv_b_w, conv_b_bias, dt_bias, a_log, d_skip,
                  q_norm, k_norm, cmp_pos, cmp_w1, cmp_w2, norm_out, w_out, norm_ffn, w_gate, w_up, w_down):
    w = w_in[l]
    w_perm = jnp.concatenate(
        [w[:, 4104:5640], w[:, 0:1536], w[:, 2048:3072], w[:, 3080:4104], w[:, 1536:2048],
         w[:, 3072:3080], w[:, 5640:5688], jnp.zeros((D_MODEL, U_COLS - 5688), F32)], axis=1).astype(BF16)
    pad8 = lambda v: jnp.pad(v, (0, LANES - v.shape[0])).reshape(1, LANES)
    head_id = jnp.arange(256) // HEAD_DIM
    pos = cmp_pos[l].reshape(4, 1024)
    w1 = cmp_w1[l]
    w2 = cmp_w2[l]
    zeros_w2 = jnp.zeros((CMP_HIDDEN, HEAD_DIM), F32)
    return dict(
        g_mix=norm_mix[l].reshape(1, D_MODEL), w_in=w_perm,
        conv_a_w=conv_a_w[l], conv_b_w=conv_b_w[l], conv_b_bias=conv_b_bias[l].reshape(1, SSM_CONV_DIM),
        dt_bias=pad8(dt_bias[l]), a_log=pad8(a_log[l]),
        d_skip=jnp.repeat(d_skip[l], HEAD_DIM).reshape(1, D_SSM),
        mavg=jnp.where(head_id[:, None] == head_id[None, :], 1.0 / HEAD_DIM, 0.0).astype(BF16),
        q_gain=jnp.tile(q_norm[l], 4).reshape(1, 256),
        k_gain1=jnp.tile(k_norm[l, 1], 4).reshape(1, 256),
        k_gain2=jnp.tile(k_norm[l, 2], 4).reshape(1, 256),
        k_gain0=pad8(k_norm[l, 0]),
        pos=jnp.pad(pos, ((0, 4), (0, 0))),
        w1k=jnp.concatenate([w1[0, 0:1024], w1[0, 1024:2048]], axis=1).astype(BF16),
        w1v=jnp.concatenate([w1[1, 0:1024], w1[1, 1024:2048]], axis=1).astype(BF16),
        w2k=jnp.concatenate([w2[0], zeros_w2], axis=1).astype(BF16),
        w2v=jnp.concatenate([zeros_w2, w2[1]], axis=1).astype(BF16),
        g_out=norm_out[l].reshape(1, D_MODEL), w_out=w_out[l].astype(BF16),
        g_ffn=norm_ffn[l].reshape(1, D_MODEL),
        w_gate=w_gate[l].astype(BF16), w_up=w_up[l].astype(BF16), w_down=w_down[l].astype(BF16),
    )


def _group_tables(slopes, t, q0):
    lmain = q0 if q0 else t
    n_sel_chunks = lmain // SEL_CHUNK + (1 if q0 else 0)
    n_blocks = n_sel_chunks * SEL_CHUNK // SEL_BLOCK
    if q0:
        n_blocks = -(-n_blocks // LANES) * LANES
    else:
        assert n_blocks <= HEAD_DIM
        n_blocks = HEAD_DIM
    return _attn_tables(slopes, n_sel_chunks, n_blocks)


def _layer(x, nseq, t, q0, p, tables, conv_a_prefix, conv_b_prefix, ssm_h0, past):
    u = _in_proj(x, p["g_mix"], p["w_in"])
    ya, new_conv_a = _mixer_a(u, conv_a_prefix, p["conv_a_w"], nseq, t)
    yb, new_conv_b, new_ssm = _ssd(u, conv_b_prefix, ssm_h0, p["conv_b_w"], p["conv_b_bias"],
                                   p["dt_bias"], p["a_log"], p["d_skip"], nseq, t)
    prep_out = _prep(u, p["mavg"], p["q_gain"], p["k_gain1"], p["k_gain2"],
                     BF16 if past is None else F32, past is None)
    qhm, new_k, new_v, kwin = prep_out[:4]
    vwin = u[:, U_KV + 1280:U_KV + 1536]
    cmp_args = (p["pos"], p["w1k"], p["w1v"], p["w2k"], p["w2v"], p["k_gain0"])
    if past is None:
        npages = t // PAGE_SIZE
        ptab = jnp.arange(nseq * npages, dtype=jnp.int32)
        kvcmp, kvsel = _pagepass(ptab, new_k.reshape(-1, PAGE_SIZE, PAGE_COLS),
                                 new_v.reshape(-1, PAGE_SIZE, PAGE_COLS), *cmp_args, nseq, npages)
        kvwin_arr = prep_out[4].reshape(KV_HEADS, nseq, t // KEY_TILE, 2 * HEAD_DIM, KEY_TILE)
        yc = _attn(tables, qhm, u, kvcmp, kvsel, kvwin_arr, nseq, t)
        new_win_k = kwin.reshape(nseq, t, 256)[:, t - WINDOW:]
        new_win_v = vwin.reshape(nseq, t, 256)[:, t - WINDOW:]
    else:
        ptab, k_pages, v_pages, npages, win_k_t, win_v_t, win_k, win_v = past
        kvcmp, kvsel = _pagepass(ptab, k_pages, v_pages, *cmp_args, nseq, npages)
        kvwin_arr, tail = _winpack(win_k_t, win_v_t, kwin, u, new_k, new_v, nseq, t)
        yc = _attn_sample(tables, qhm, u, kvcmp, kvsel, kvwin_arr, tail, nseq, t, q0)
        new_win_k = jnp.concatenate([win_k[:, t:], kwin.reshape(nseq, t, 256)], axis=1)
        new_win_v = jnp.concatenate([win_v[:, t:], vwin.reshape(nseq, t, 256)], axis=1)
    x = _out_proj(x, ya, yb, yc, p["g_out"], p["w_out"])
    x = _ffn(x, p["g_ffn"], p["w_gate"], p["w_up"], p["w_down"])
    hd = (KV_HEADS, HEAD_DIM)
    state = (new_k.reshape(nseq, t, 2, *hd), new_v.reshape(nseq, t, 2, *hd),
             new_win_k.reshape(nseq, WINDOW, *hd), new_win_v.reshape(nseq, WINDOW, *hd),
             new_conv_a, new_conv_b, new_ssm)
    return x, state


def kernel(x_prompt, x_sample, cache_k, cache_v, cache_win_k, cache_win_v, state_conv_a, state_conv_b,
           state_ssm, page_table, norm_mix, w_in, conv_a_w, conv_b_w, conv_b_bias, dt_bias, a_log, d_skip,
           q_norm, k_norm, cmp_pos, cmp_w1, cmp_w2, norm_out, w_out, norm_ffn, w_gate, w_up, w_down):
    bsz, t_prompt, _ = x_prompt.shape
    dec_b, t_dec, _ = x_sample.shape
    depth, n_pool = cache_k.shape[0], cache_k.shape[1]
    npages = page_table.shape[1]
    past_len = npages * PAGE_SIZE
    assert cache_win_k.shape[2] == WINDOW and t_prompt >= WINDOW + Q_BLOCK and t_prompt % Q_BLOCK == 0
    assert t_dec == 8 and npages % PAGES_PER_STEP == 0 and (t_prompt // PAGE_SIZE) % PAGES_PER_STEP == 0

    slopes = jnp.exp2(-8.0 * jnp.arange(1, ATTN_HEADS + 1, dtype=F32) / ATTN_HEADS)
    prompt_tables = _group_tables(slopes, t_prompt, 0)
    sample_tables = _group_tables(slopes, t_dec, past_len)
    to_pages_t = lambda c: jnp.transpose(c, (0, 1, 3, 4, 5, 2)).reshape(depth * n_pool, PAGE_COLS, PAGE_SIZE)
    to_win_t = lambda w: jnp.transpose(w, (0, 2, 3, 1)).reshape(dec_b, 256, WINDOW)
    k_pages = to_pages_t(cache_k)
    v_pages = to_pages_t(cache_v)
    hp = x_prompt.reshape(bsz * t_prompt, D_MODEL)
    hs = x_sample.reshape(dec_b * t_dec, D_MODEL)
    zeros = lambda *shape: jnp.zeros(shape, F32)
    prompt_states, sample_states = [], []
    for l in range(depth):
        p = _layer_params(l, norm_mix, w_in, conv_a_w, conv_b_w, conv_b_bias, dt_bias, a_log, d_skip,
                          q_norm, k_norm, cmp_pos, cmp_w1, cmp_w2, norm_out, w_out, norm_ffn,
                          w_gate, w_up, w_down)
        hp, st_p = _layer(hp, bsz, t_prompt, 0, p, prompt_tables,
                          zeros(bsz, CONV_A_WIDTH - 1, D_CONV), zeros(bsz, SSM_CONV_WIDTH - 1, SSM_CONV_DIM),
                          zeros(bsz, SSM_HEADS, HEAD_DIM, SSM_STATE), None)
        ptab = (page_table + l * n_pool).reshape(-1).astype(jnp.int32)
        past = (ptab, k_pages, v_pages, npages, to_win_t(cache_win_k[l]), to_win_t(cache_win_v[l]),
                cache_win_k[l].reshape(dec_b, WINDOW, 256), cache_win_v[l].reshape(dec_b, WINDOW, 256))
        hs, st_s = _layer(hs, dec_b, t_dec, past_len, p, sample_tables,
                          state_conv_a[l], state_conv_b[l], state_ssm[l], past)
        prompt_states.append(st_p)
        sample_states.append(st_s)

    stack = lambda states, j: jnp.stack([s[j] for s in states], axis=0)
    return ((hp.reshape(bsz, t_prompt, D_MODEL), hs.reshape(dec_b, t_dec, D_MODEL))
            + tuple(stack(prompt_states, j) for j in range(7))
            + tuple(stack(sample_states, j) for j in range(7)))
```

```python
import functools

import jax
import jax.numpy as jnp
from jax import lax
from jax.experimental import pallas as pl
from jax.experimental.pallas import tpu as pltpu

F32 = jnp.float32
BF16 = jnp.bfloat16

D_MODEL = 2048
HEAD_DIM = 64
D_CONV = 512
D_SSM = 512
D_ATTN = 1024
CONV_A_WIDTH = 3
SSM_HEADS = 8
SSM_STATE = 128
SSM_CONV_WIDTH = 4
SSM_CHUNK = 128
SSM_CONV_DIM = 1024
ATTN_HEADS = 16
KV_HEADS = 4
Q_PER_KV = 4
KV_COLS = 1536
CMP_STRIDE = 16
CMP_HIDDEN = 128
SEL_BLOCK = 64
TOP_BLOCKS = 16
WINDOW = 512
Q_BLOCK = 128
PAGE_SIZE = 128
PAGE_COLS = 512
D_FF = 5632
RMS_EPS = 1e-6
NEG_INF = -1e30
TINY = 1e-30
FORCE_BONUS = 1e3
ATTN_SCALE = HEAD_DIM ** -0.5

U_KV = 0
U_A = 1536
U_XBC = 3072
U_Q = 4096
U_Z = 5120
U_MISC = 5632
U_COLS = 5760
GATE_LANE0 = 8

LANES = 128
KEY_TILE = 128
SEL_CHUNK = 256
VMEM_LIMIT_BYTES = 56 * 2 ** 20

NT_DIMS = (((1,), (1,)), ((), ()))


def _cparams(*sem):
    return pltpu.CompilerParams(dimension_semantics=sem, vmem_limit_bytes=VMEM_LIMIT_BYTES)


def _tile(n, pref):
    t = min(n, pref)
    while n % t:
        t //= 2
    return t


def _dot(a, b):
    return jnp.dot(a, b, preferred_element_type=F32)


def _split3(x):
    hi = x.astype(BF16)
    r1 = x - hi.astype(F32)
    mid = r1.astype(BF16)
    lo = (r1 - mid.astype(F32)).astype(BF16)
    return hi, mid, lo


def _lo_half(rows):
    return lax.broadcasted_iota(jnp.int32, (rows, LANES), 1) < HEAD_DIM


def _swap_halves(x):
    return pltpu.roll(x, HEAD_DIM, 1)


def _store_kv_t(dst, k_t, v_t):
    for g in range(KV_HEADS):
        view = dst(g)
        view[0:HEAD_DIM, :] = k_t[g * HEAD_DIM:(g + 1) * HEAD_DIM].astype(BF16)
        view[HEAD_DIM:2 * HEAD_DIM, :] = v_t[g * HEAD_DIM:(g + 1) * HEAD_DIM].astype(BF16)


def _in_proj_kernel(x_ref, g_ref, w_ref, o_ref, xn_ref):
    @pl.when(pl.program_id(1) == 0)
    def _():
        x = x_ref[...]
        ms = jnp.mean(x * x, axis=-1, keepdims=True)
        xn_ref[...] = (x * lax.rsqrt(ms + RMS_EPS) * g_ref[...]).astype(BF16)

    o_ref[...] = _dot(xn_ref[...], w_ref[...])


def _in_proj(x, gain, w):
    n = x.shape[0]
    tm = _tile(n, 1024)
    tn = 1152
    return pl.pallas_call(
        _in_proj_kernel,
        grid=(n // tm, U_COLS // tn),
        in_specs=[pl.BlockSpec((tm, D_MODEL), lambda i, j: (i, 0)),
                  pl.BlockSpec((1, D_MODEL), lambda i, j: (0, 0)),
                  pl.BlockSpec((D_MODEL, tn), lambda i, j: (0, j))],
        out_specs=pl.BlockSpec((tm, tn), lambda i, j: (i, j)),
        out_shape=jax.ShapeDtypeStruct((n, U_COLS), F32),
        scratch_shapes=[pltpu.VMEM((tm, D_MODEL), BF16)],
        compiler_params=_cparams("parallel", "arbitrary"),
        name="in_proj",
    )(x, gain, w)


def _out_proj_kernel(x_ref, ya_ref, yb_ref, yc_ref, g_ref, w_ref, o_ref):
    def nrm(y, g):
        ms = jnp.mean(y * y, axis=-1, keepdims=True)
        return (y * lax.rsqrt(ms + RMS_EPS) * g).astype(BF16)

    o_ref[...] = (x_ref[...]
                  + _dot(nrm(ya_ref[...], g_ref[:, 0:512]), w_ref[0:512, :])
                  + _dot(nrm(yb_ref[...], g_ref[:, 512:1024]), w_ref[512:1024, :])
                  + _dot(nrm(yc_ref[...], g_ref[:, 1024:2048]), w_ref[1024:2048, :]))


def _out_proj(x, ya, yb, yc, gain, w):
    n = x.shape[0]
    tm = _tile(n, 512)
    return pl.pallas_call(
        _out_proj_kernel,
        grid=(n // tm,),
        in_specs=[pl.BlockSpec((tm, D_MODEL), lambda i: (i, 0)),
                  pl.BlockSpec((tm, D_CONV), lambda i: (i, 0)),
                  pl.BlockSpec((tm, D_SSM), lambda i: (i, 0)),
                  pl.BlockSpec((tm, D_ATTN), lambda i: (i, 0)),
                  pl.BlockSpec((1, D_MODEL), lambda i: (0, 0)),
                  pl.BlockSpec((D_MODEL, D_MODEL), lambda i: (0, 0))],
        out_specs=pl.BlockSpec((tm, D_MODEL), lambda i: (i, 0)),
        out_shape=jax.ShapeDtypeStruct((n, D_MODEL), F32),
        compiler_params=_cparams("parallel"),
        name="out_proj",
    )(x, ya, yb, yc, gain, w)


FFN_CHAINS = 2


def _ffn_kernel(x_ref, g_ref, wg_ref, wu_ref, wd_ref, o_ref, h_ref, acc_ref):
    f = pl.program_id(1)

    @pl.when(f == 0)
    def _():
        x = x_ref[...]
        ms = jnp.mean(x * x, axis=-1, keepdims=True)
        h_ref[...] = (x * lax.rsqrt(ms + RMS_EPS) * g_ref[...]).astype(BF16)
        acc_ref[...] = jnp.zeros_like(acc_ref)

    h = h_ref[...]
    tf = wg_ref.shape[1]
    part = None
    for c in range(FFN_CHAINS):
        cs = slice(c * tf // FFN_CHAINS, (c + 1) * tf // FFN_CHAINS)
        a = _dot(h, wg_ref[:, cs])
        b = _dot(h, wu_ref[:, cs])
        d = _dot((a * jax.nn.sigmoid(a) * b).astype(BF16), wd_ref[cs, :])
        part = d if part is None else part + d
    acc_ref[...] += part

    @pl.when(f == pl.num_programs(1) - 1)
    def _():
        o_ref[...] = x_ref[...] + acc_ref[...]


def _ffn(x, gain, wg, wu, wd):
    n = x.shape[0]
    tm = _tile(n, 512)
    tf = 512
    return pl.pallas_call(
        _ffn_kernel,
        grid=(n // tm, D_FF // tf),
        in_specs=[pl.BlockSpec((tm, D_MODEL), lambda i, f: (i, 0)),
                  pl.BlockSpec((1, D_MODEL), lambda i, f: (0, 0)),
                  pl.BlockSpec((D_MODEL, tf), lambda i, f: (0, f)),
                  pl.BlockSpec((D_MODEL, tf), lambda i, f: (0, f)),
                  pl.BlockSpec((tf, D_MODEL), lambda i, f: (f, 0))],
        out_specs=pl.BlockSpec((tm, D_MODEL), lambda i, f: (i, 0)),
        out_shape=jax.ShapeDtypeStruct((n, D_MODEL), F32),
        scratch_shapes=[pltpu.VMEM((tm, D_MODEL), BF16), pltpu.VMEM((tm, D_MODEL), F32)],
        compiler_params=_cparams("parallel", "arbitrary"),
        name="ffn",
    )(x, gain, wg, wu, wd)


def _mixer_a_kernel(u_ref, pre_ref, w_ref, y_ref, newpre_ref, ext_ref, *, rows):
    @pl.when(pl.program_id(1) == 0)
    def _():
        ext_ref[6:8, :] = pre_ref[0]

    u = u_ref[...]
    v = u[:, 512:1024] * u[:, 1024:1536]
    ext_ref[8:8 + rows, :] = v
    w = w_ref[...]
    y = ext_ref[6:6 + rows, :] * w[0:1] + ext_ref[7:7 + rows, :] * w[1:2] + v * w[2:3]
    y_ref[...] = u[:, 0:512] * y
    last = ext_ref[8 + rows - 2:8 + rows, :]
    ext_ref[6:8, :] = last
    newpre_ref[0] = last


def _mixer_a(u, prefix, w, nseq, t):
    rows = _tile(t, 512)
    nblk = t // rows
    return pl.pallas_call(
        functools.partial(_mixer_a_kernel, rows=rows),
        grid=(nseq, nblk),
        in_specs=[pl.BlockSpec((rows, 3 * D_CONV), lambda s, j: (s * nblk + j, U_A // (3 * D_CONV))),
                  pl.BlockSpec((1, CONV_A_WIDTH - 1, D_CONV), lambda s, j: (s, 0, 0)),
                  pl.BlockSpec((CONV_A_WIDTH, D_CONV), lambda s, j: (0, 0))],
        out_specs=[pl.BlockSpec((rows, D_CONV), lambda s, j: (s * nblk + j, 0)),
                   pl.BlockSpec((1, CONV_A_WIDTH - 1, D_CONV), lambda s, j: (s, 0, 0))],
        out_shape=[jax.ShapeDtypeStruct((nseq * t, D_CONV), F32),
                   jax.ShapeDtypeStruct((nseq, CONV_A_WIDTH - 1, D_CONV), F32)],
        scratch_shapes=[pltpu.VMEM((8 + rows, D_CONV), F32)],
        compiler_params=_cparams("parallel", "arbitrary"),
        name="mixer_a",
    )(u, prefix, w)


def _ssd_kernel(z_ref, xbc_ref, misc_ref, pre_ref, h0_ref, cw_ref, cb_ref, dtb_ref, alog_ref, dsk_ref,
                y_ref, newpre_ref, hout_ref, ext_ref, st_ref, pad_ref, *, lin):
    L = SSM_CHUNK
    c = pl.program_id(1)

    @pl.when(c == 0)
    def _():
        if lin < L:
            ext_ref[...] = jnp.zeros_like(ext_ref)
            pad_ref[...] = jnp.zeros_like(pad_ref)
        ext_ref[5:8, :] = pre_ref[0]
        st_ref[...] = h0_ref[0].reshape(SSM_HEADS * HEAD_DIM, SSM_STATE)

    ext_ref[8:8 + lin, :] = xbc_ref[...]
    cw = cw_ref[...]
    acc = (ext_ref[5:5 + L, :] * cw[0:1] + ext_ref[6:6 + L, :] * cw[1:2]
           + ext_ref[7:7 + L, :] * cw[2:3] + ext_ref[8:8 + L, :] * cw[3:4])
    acc = acc + cb_ref[...]
    xbc = acc * jax.nn.sigmoid(acc)
    newp = ext_ref[8 + lin - 3:8 + lin, :]
    ext_ref[5:8, :] = newp
    newpre_ref[0] = newp

    if lin < L:
        pad_ref[0:lin, :] = misc_ref[...]
        misc = pad_ref[...]
    else:
        misc = misc_ref[...]
    rows = lax.broadcasted_iota(jnp.int32, (L, L), 0)
    cols = lax.broadcasted_iota(jnp.int32, (L, L), 1)
    x = misc + dtb_ref[...]
    dt = jnp.maximum(x, 0.0) + jnp.log1p(jnp.exp(-jnp.abs(x)))
    if lin < L:
        dt = jnp.where(rows < lin, dt, 0.0)
    la = dt * (-jnp.exp(alog_ref[...]))

    causal = rows >= cols
    tril = jnp.where(causal, 1.0, 0.0).astype(BF16)
    hi, mid, lo = _split3(la)
    cum = _dot(tril, hi) + _dot(tril, mid) + _dot(tril, lo)
    cum_t = cum.T
    last = cum[L - 1:L, :]
    lo_half = cols < HEAD_DIM
    top_rows = rows < HEAD_DIM

    z = z_ref[...]
    dsk = dsk_ref[...]
    for g in range(2):
        bg = xbc[:, 512 + g * 128:512 + (g + 1) * 128].astype(BF16)
        cg = xbc[:, 768 + g * 128:768 + (g + 1) * 128].astype(BF16)
        gram = lax.dot_general(cg, bg, NT_DIMS, preferred_element_type=F32)
        for i in (2 * g, 2 * g + 1):
            a, b = 2 * i, 2 * i + 1
            sl = slice(i * LANES, (i + 1) * LANES)
            xs_p = xbc[:, sl]
            col_a = cum[:, a:a + 1]
            col_b = cum[:, b:b + 1]
            xdt = xs_p * jnp.where(lo_half, dt[:, a:a + 1], dt[:, b:b + 1])
            dec_a = jnp.exp(jnp.where(causal, col_a - cum_t[a:a + 1, :], NEG_INF))
            dec_b = jnp.exp(jnp.where(causal, col_b - cum_t[b:b + 1, :], NEG_INF))
            xa = jnp.where(lo_half, xdt, 0.0).astype(BF16)
            xb = jnp.where(lo_half, 0.0, xdt).astype(BF16)
            y_intra = _dot((gram * dec_a).astype(BF16), xa) + _dot((gram * dec_b).astype(BF16), xb)
            st = st_ref[sl, :]
            y_inter = (lax.dot_general(cg, st.astype(BF16), NT_DIMS, preferred_element_type=F32)
                       * jnp.where(lo_half, jnp.exp(col_a), jnp.exp(col_b)))
            to_end = jnp.where(lo_half, jnp.exp(last[:, a:a + 1] - col_a), jnp.exp(last[:, b:b + 1] - col_b))
            xw_t = (xdt * to_end).T.astype(BF16)
            decay = jnp.where(top_rows, jnp.exp(last[:, a:a + 1]), jnp.exp(last[:, b:b + 1]))
            st_ref[sl, :] = st * decay + _dot(xw_t, bg)
            y = y_intra + y_inter + dsk[:, sl] * xs_p
            zp = z[:, sl]
            y_ref[:, sl] = y[0:lin] * (zp * jax.nn.sigmoid(zp))

    @pl.when(c == pl.num_programs(1) - 1)
    def _():
        hout_ref[0] = st_ref[...].reshape(SSM_HEADS, HEAD_DIM, SSM_STATE)


def _ssd(u, prefix, h0, cw, cb, dtb, alog, dsk, nseq, t):
    lin = min(t, SSM_CHUNK)
    nc = t // lin
    full = lambda shape: pl.BlockSpec(shape, lambda s, c: (0,) * len(shape))
    return pl.pallas_call(
        functools.partial(_ssd_kernel, lin=lin),
        grid=(nseq, nc),
        in_specs=[pl.BlockSpec((lin, D_SSM), lambda s, c: (s * nc + c, U_Z // D_SSM)),
                  pl.BlockSpec((lin, SSM_CONV_DIM), lambda s, c: (s * nc + c, U_XBC // SSM_CONV_DIM)),
                  pl.BlockSpec((lin, LANES), lambda s, c: (s * nc + c, U_MISC // LANES)),
                  pl.BlockSpec((1, SSM_CONV_WIDTH - 1, SSM_CONV_DIM), lambda s, c: (s, 0, 0)),
                  pl.BlockSpec((1, SSM_HEADS, HEAD_DIM, SSM_STATE), lambda s, c: (s, 0, 0, 0)),
                  full((SSM_CONV_WIDTH, SSM_CONV_DIM)), full((1, SSM_CONV_DIM)),
                  full((1, LANES)), full((1, LANES)), full((1, D_SSM))],
        out_specs=[pl.BlockSpec((lin, D_SSM), lambda s, c: (s * nc + c, 0)),
                   pl.BlockSpec((1, SSM_CONV_WIDTH - 1, SSM_CONV_DIM), lambda s, c: (s, 0, 0)),
                   pl.BlockSpec((1, SSM_HEADS, HEAD_DIM, SSM_STATE), lambda s, c: (s, 0, 0, 0))],
        out_shape=[jax.ShapeDtypeStruct((nseq * t, D_SSM), F32),
                   jax.ShapeDtypeStruct((nseq, SSM_CONV_WIDTH - 1, SSM_CONV_DIM), F32),
                   jax.ShapeDtypeStruct((nseq, SSM_HEADS, HEAD_DIM, SSM_STATE), F32)],
        scratch_shapes=[pltpu.VMEM((8 + SSM_CHUNK, SSM_CONV_DIM), F32),
                        pltpu.VMEM((SSM_HEADS * HEAD_DIM, SSM_STATE), F32),
                        pltpu.VMEM((SSM_CHUNK, LANES), F32)],
        compiler_params=_cparams("parallel", "arbitrary"),
        name="ssd",
    )(u, u, u, prefix, h0, cw, cb, dtb, alog, dsk)


def _headnorm(x, mavg, gain):
    x2 = x * x
    hi = x2.astype(BF16)
    lo = (x2 - hi.astype(F32)).astype(BF16)
    ms = _dot(hi, mavg) + _dot(lo, mavg)
    return x * lax.rsqrt(ms + RMS_EPS) * gain


def _prep_kernel(q_ref, kv_ref, mavg_ref, qg_ref, kg1_ref, kg2_ref,
                 qhm_ref, nk_ref, nv_ref, kwin_ref, *maybe_kvwin_ref):
    mavg = mavg_ref[...]
    lo = _lo_half(q_ref.shape[0])
    for c in range(4):
        qn = _headnorm(q_ref[:, c * 256:(c + 1) * 256], mavg, qg_ref[...]) * ATTN_SCALE
        for cc in range(2):
            col = qn[:, cc * LANES:(cc + 1) * LANES]
            h = c * 4 + cc * 2
            qhm_ref[h] = jnp.where(lo, col, 0.0).astype(qhm_ref.dtype)
            qhm_ref[h + 1] = jnp.where(lo, _swap_halves(col), 0.0).astype(qhm_ref.dtype)
    ksel = _headnorm(kv_ref[:, 512:768], mavg, kg1_ref[...])
    kwin = _headnorm(kv_ref[:, 1024:1280], mavg, kg2_ref[...])
    nk_ref[:, 0:256] = kv_ref[:, 0:256]
    nk_ref[:, 256:512] = ksel
    nv_ref[:, 0:256] = kv_ref[:, 256:512]
    nv_ref[:, 256:512] = kv_ref[:, 768:1024]
    kwin_ref[...] = kwin
    if maybe_kvwin_ref:
        kvwin_ref, = maybe_kvwin_ref
        for c in range(q_ref.shape[0] // KEY_TILE):
            rs = slice(c * KEY_TILE, (c + 1) * KEY_TILE)
            _store_kv_t(lambda g: kvwin_ref.at[g, c], kwin[rs].T, kv_ref[rs, 1280:1536].T)


def _prep(u, mavg, qg, kg1, kg2, q_dtype, emit_window_tiles):
    n = u.shape[0]
    tm = _tile(n, 256)
    full = lambda shape: pl.BlockSpec(shape, lambda i: (0,) * len(shape))
    out_specs = [pl.BlockSpec((ATTN_HEADS, tm, LANES), lambda i: (0, i, 0)),
                 pl.BlockSpec((tm, PAGE_COLS), lambda i: (i, 0)),
                 pl.BlockSpec((tm, PAGE_COLS), lambda i: (i, 0)),
                 pl.BlockSpec((tm, 256), lambda i: (i, 0))]
    out_shape = [jax.ShapeDtypeStruct((ATTN_HEADS, n, LANES), q_dtype),
                 jax.ShapeDtypeStruct((n, PAGE_COLS), F32),
                 jax.ShapeDtypeStruct((n, PAGE_COLS), F32),
                 jax.ShapeDtypeStruct((n, 256), F32)]
    if emit_window_tiles:
        out_specs.append(pl.BlockSpec((KV_HEADS, tm // KEY_TILE, 2 * HEAD_DIM, KEY_TILE), lambda i: (0, i, 0, 0)))
        out_shape.append(jax.ShapeDtypeStruct((KV_HEADS, n // KEY_TILE, 2 * HEAD_DIM, KEY_TILE), BF16))
    return pl.pallas_call(
        _prep_kernel,
        grid=(n // tm,),
        in_specs=[pl.BlockSpec((tm, D_ATTN), lambda i: (i, U_Q // D_ATTN)),
                  pl.BlockSpec((tm, KV_COLS), lambda i: (i, 0)),
                  full((256, 256)), full((1, 256)), full((1, 256)), full((1, 256))],
        out_specs=out_specs,
        out_shape=out_shape,
        compiler_params=_cparams("parallel"),
        name="nsa_prep",
    )(u, u, mavg, qg, kg1, kg2)


PAGES_PER_STEP = 16


def _gelu_tanh(x):
    return 0.5 * x * (1.0 + jnp.tanh(0.7978845608028654 * (x + 0.044715 * (x * x * x))))


def _pagepass_kernel(pt_ref, *refs, nchunk, feature_major):
    pp = PAGES_PER_STEP

    pi = lax.broadcasted_iota(jnp.int32, (PAGE_SIZE, PAGE_SIZE), 0)
    pk = lax.broadcasted_iota(jnp.int32, (PAGE_SIZE, PAGE_SIZE), 1)
    perm = jnp.where(pk == CMP_STRIDE * (pi & 7) + (pi >> 3), 1.0, 0.0).astype(BF16)

    def cmp_rows_permuted(page, c):
        if feature_major:
            return lax.dot_general(perm, page[0, c * LANES:(c + 1) * LANES, :].astype(BF16), NT_DIMS,
                                   preferred_element_type=F32)
        return _dot(perm, page[0, :, c * LANES:(c + 1) * LANES].astype(BF16))

    def sel_t(page):
        return page[0, 256:512, :] if feature_major else page[0, :, 256:512].T

    kpages = refs[0:pp]
    vpages = refs[pp:2 * pp]
    pos_ref, w1k_ref, w1v_ref, w2k_ref, w2v_ref, kg_ref = refs[2 * pp:2 * pp + 6]
    kvcmp_ref, kvsel_ref = refs[2 * pp + 6:2 * pp + 8]
    xk_ref, xv_ref, hs_ref = refs[2 * pp + 8:]
    j = pl.program_id(1)
    lo16 = _lo_half(16)

    for i2 in range(pp // 2):
        r0 = pl.multiple_of((j * pp + 2 * i2) * 8, 16)
        for pages, xs_ref in ((kpages, xk_ref), (vpages, xv_ref)):
            for c in range(2):
                pa = cmp_rows_permuted(pages[2 * i2], c)
                pb = cmp_rows_permuted(pages[2 * i2 + 1], c)
                for a in range(8):
                    ev, od = slice(16 * a, 16 * a + 8), slice(16 * a + 8, 16 * a + 16)
                    ec = jnp.concatenate([pa[ev], pb[ev]], axis=0)
                    oc = jnp.concatenate([pa[od], pb[od]], axis=0)
                    xs_ref[2 * c, pl.ds(r0, 16), a * LANES:(a + 1) * LANES] = (
                        jnp.where(lo16, ec, _swap_halves(oc)).astype(BF16))
                    xs_ref[2 * c + 1, pl.ds(r0, 16), a * LANES:(a + 1) * LANES] = (
                        jnp.where(lo16, _swap_halves(ec), oc).astype(BF16))

    per_chunk = SEL_CHUNK // PAGE_SIZE
    for i in range(pp):
        ls = slice((i % per_chunk) * PAGE_SIZE, (i % per_chunk + 1) * PAGE_SIZE)
        _store_kv_t(lambda g: kvsel_ref.at[0, g, i // per_chunk, :, ls], sel_t(kpages[i]), sel_t(vpages[i]))

    @pl.when(j == pl.num_programs(1) - 1)
    def _():
        for slot in range(2 * KV_HEADS):
            hs_ref[slot, nchunk:nchunk + 8, :] = jnp.zeros((8, 256), F32)
        pos = pos_ref[...]
        phi = pos.astype(BF16)
        plo = (pos - phi.astype(F32)).astype(BF16)
        bias_k = _dot(phi, w1k_ref[...]) + _dot(plo, w1k_ref[...])
        bias_v = _dot(phi, w1v_ref[...]) + _dot(plo, w1v_ref[...])
        bias_k = bias_k[0:1, 0:128] + bias_k[1:2, 128:256]
        bias_v = bias_v[2:3, 0:128] + bias_v[3:4, 128:256]

        for g in range(KV_HEADS):
            hs_ref[2 * g, 0:nchunk, :] = _dot(xk_ref[g], w1k_ref[...])
            hs_ref[2 * g + 1, 0:nchunk, :] = _dot(xv_ref[g], w1v_ref[...])

        def summarise(slot, bias, w2_ref):
            pre = hs_ref[slot, 0:nchunk, 0:128] + hs_ref[slot, 1:nchunk + 1, 128:256] + bias
            return _dot(_gelu_tanh(pre).astype(BF16), w2_ref[...])

        for g in range(KV_HEADS):
            ko = summarise(2 * g, bias_k, w2k_ref)
            vo = summarise(2 * g + 1, bias_v, w2v_ref)
            ms = jnp.sum(ko * ko, axis=-1, keepdims=True) * (1.0 / HEAD_DIM)
            kvcmp_ref[0, g] = (ko * lax.rsqrt(ms + RMS_EPS) * kg_ref[...] + vo).astype(BF16)


def _pagepass(ptab, kpages, vpages, pos, w1k, w1v, w2k, w2v, kg, nseq, npages):
    pp = PAGES_PER_STEP
    nsteps = npages // pp
    nchunk = npages * PAGE_SIZE // CMP_STRIDE
    feature_major = kpages.shape[1] == PAGE_COLS
    chunks_per_step = pp * PAGE_SIZE // SEL_CHUNK

    def page_spec(i):
        return pl.BlockSpec((1,) + kpages.shape[1:],
                            lambda s, j, pt: (pt[s * npages + j * pp + i], 0, 0))

    full = lambda shape: pl.BlockSpec(shape, lambda s, j, pt: (0,) * len(shape))
    grid_spec = pltpu.PrefetchScalarGridSpec(
        num_scalar_prefetch=1,
        grid=(nseq, nsteps),
        in_specs=([page_spec(i) for i in range(pp)] + [page_spec(i) for i in range(pp)]
                  + [full((8, 1024)), full((1024, 256)), full((1024, 256)),
                     full((128, 128)), full((128, 128)), full((1, 128))]),
        out_specs=[pl.BlockSpec((1, KV_HEADS, nchunk, LANES), lambda s, j, pt: (s, 0, 0, 0)),
                   pl.BlockSpec((1, KV_HEADS, chunks_per_step, 2 * HEAD_DIM, SEL_CHUNK),
                                lambda s, j, pt: (s, 0, j, 0, 0))],
        scratch_shapes=[pltpu.VMEM((KV_HEADS, nchunk, 1024), BF16),
                        pltpu.VMEM((KV_HEADS, nchunk, 1024), BF16),
                        pltpu.VMEM((2 * KV_HEADS, nchunk + 8, 256), F32)],
    )
    return pl.pallas_call(
        functools.partial(_pagepass_kernel, nchunk=nchunk, feature_major=feature_major),
        grid_spec=grid_spec,
        out_shape=[jax.ShapeDtypeStruct((nseq, KV_HEADS, nchunk, LANES), BF16),
                   jax.ShapeDtypeStruct((nseq, KV_HEADS, npages * PAGE_SIZE // SEL_CHUNK, 2 * HEAD_DIM, SEL_CHUNK),
                                        BF16)],
        compiler_params=_cparams("parallel", "arbitrary"),
        name="pagepass",
    )(ptab, *([kpages] * pp), *([vpages] * pp), pos, w1k, w1v, w2k, w2v, kg)


WIN_TILES = (WINDOW + Q_BLOCK) // KEY_TILE


def _winpack_kernel(wk_ref, wv_ref, kwn_ref, vwn_ref, ksn_ref, vsn_ref, kvw_ref, tail_ref):
    t = kwn_ref.shape[0]
    past_tiles = WINDOW // KEY_TILE
    for c in range(past_tiles):
        ls = slice(c * KEY_TILE, (c + 1) * KEY_TILE)
        _store_kv_t(lambda g: kvw_ref.at[0, g, c], wk_ref[0, :, ls], wv_ref[0, :, ls])
    pad_t = lambda ref, n: jnp.concatenate([ref[...], jnp.zeros((n - t, 256), F32)], axis=0).T
    _store_kv_t(lambda g: kvw_ref.at[0, g, past_tiles], pad_t(kwn_ref, KEY_TILE), pad_t(vwn_ref, KEY_TILE))
    _store_kv_t(lambda g: tail_ref.at[0, g], pad_t(ksn_ref, SEL_CHUNK), pad_t(vsn_ref, SEL_CHUNK))


def _winpack(wk_t, wv_t, kwin, u, new_k, new_v, nseq, t):
    tile = (1, KV_HEADS, 2 * HEAD_DIM, SEL_CHUNK)
    return pl.pallas_call(
        _winpack_kernel,
        grid=(nseq,),
        in_specs=[pl.BlockSpec((1, 256, WINDOW), lambda s: (s, 0, 0)),
                  pl.BlockSpec((1, 256, WINDOW), lambda s: (s, 0, 0)),
                  pl.BlockSpec((t, 256), lambda s: (s, 0)),
                  pl.BlockSpec((t, 256), lambda s: (s, 5)),
                  pl.BlockSpec((t, 256), lambda s: (s, 1)),
                  pl.BlockSpec((t, 256), lambda s: (s, 1))],
        out_specs=[pl.BlockSpec((1, KV_HEADS, WIN_TILES, 2 * HEAD_DIM, KEY_TILE), lambda s: (s, 0, 0, 0, 0)),
                   pl.BlockSpec(tile, lambda s: (s, 0, 0, 0))],
        out_shape=[jax.ShapeDtypeStruct((nseq, KV_HEADS, WIN_TILES, 2 * HEAD_DIM, KEY_TILE), BF16),
                   jax.ShapeDtypeStruct((nseq,) + tile[1:], BF16)],
        compiler_params=_cparams("parallel"),
        name="winpack",
    )(wk_t, wv_t, kwin, u, new_k, new_v)


MASK_BIAS = -2.0 ** 100


def _attn_tables(slopes, n_sel_chunks, n_blocks):
    slope_tab = jnp.broadcast_to(slopes[:, None, None], (ATTN_HEADS, 8, LANES))
    pos = (jnp.arange(n_sel_chunks, dtype=jnp.int32)[:, None, None] * SEL_CHUNK
           + jnp.arange(SEL_CHUNK, dtype=jnp.int32)[None, None, :])
    blk = jnp.arange(n_blocks, dtype=jnp.int32)[None, :, None]
    esel = jnp.where(blk == (pos >> 6), MASK_BIAS, 0.0).astype(BF16)
    lane_src = jnp.arange(LANES)[None, :, None]
    gate_id = jnp.arange(12)[None, None, :]
    g_id = jnp.arange(KV_HEADS)[:, None, None]
    onehot = (lane_src == GATE_LANE0 + g_id * 12 + gate_id).astype(BF16)
    gexp = jnp.broadcast_to(onehot[..., None], (KV_HEADS, LANES, 12, LANES)).reshape(KV_HEADS, LANES, 12 * LANES)
    return slope_tab, esel, gexp


def _top_blocks_unselected(val, valid):
    tq, nselp = val.shape
    if tq < LANES:
        val = jnp.concatenate([val, jnp.zeros((LANES - tq, nselp), F32)], axis=0)
    val_t = jnp.concatenate([val[:, c * LANES:(c + 1) * LANES].T for c in range(nselp // LANES)], axis=0)
    blk_t = lax.broadcasted_iota(jnp.int32, (nselp, LANES), 0).astype(F32)

    def pick_top(_, carry):
        v, sel = carry
        best = jnp.max(v, axis=0, keepdims=True)
        idx = jnp.min(jnp.where(v == best, blk_t, 1e9), axis=0, keepdims=True)
        pick = blk_t == idx
        return jnp.where(pick, -3e38, v), jnp.where(pick, 1.0, sel)

    _, sel_t = lax.fori_loop(0, TOP_BLOCKS, pick_top, (val_t, jnp.zeros((nselp, LANES), F32)), unroll=True)
    sel = jnp.concatenate([sel_t[c * LANES:(c + 1) * LANES].T for c in range(nselp // LANES)], axis=1)
    return jnp.where(valid & (sel[0:tq] > 0.5), 0.0, 1.0)


CHUNKS_PER_WORD = 8
SEL_STREAMS = 2


def _merge_streams(m_ref, l_ref, acc_of):
    m = m_ref[0]
    for st in range(1, SEL_STREAMS):
        m = jnp.maximum(m, m_ref[st])
    l_sum = None
    acc = None
    for st in range(SEL_STREAMS):
        w = jnp.exp(m_ref[st] - m)
        l_sum = w * l_ref[st] if l_sum is None else l_sum + w * l_ref[st]
        acc = w * acc_of(st) if acc is None else acc + w * acc_of(st)
    return acc / jnp.maximum(jnp.sum(l_sum, axis=-1, keepdims=True), TINY)


def _list_needed_chunks(unsel, n_chunks, words_ref, list_ref):
    nselp = unsel.shape[1]
    sel_any = jnp.max(1.0 - unsel, axis=0, keepdims=True)
    lane = lax.broadcasted_iota(jnp.int32, (1, nselp), 1)
    chunk_shift = (SEL_CHUNK // SEL_BLOCK).bit_length() - 1
    word_shift = chunk_shift + CHUNKS_PER_WORD.bit_length() - 1
    digit = (lane >> chunk_shift) & (CHUNKS_PER_WORD - 1)
    weight = lax.shift_left(jnp.ones_like(lane), 3 * digit).astype(F32)
    for w in range(words_ref.shape[0]):
        in_word = (lane >> word_shift) == w
        words_ref[w] = jnp.sum(jnp.where(in_word, sel_any * weight, 0.0)).astype(jnp.int32)

    def build(kc, cnt):
        used = (words_ref[kc // CHUNKS_PER_WORD] >> (3 * (kc % CHUNKS_PER_WORD))) & 7
        list_ref[cnt] = kc
        return cnt + jnp.where(used > 0, 1, 0)

    return lax.fori_loop(0, n_chunks, build, 0)


def _attn_kernel(q_ref, qnext_ref, misc_ref, kvc_ref, kvs_ref, kvw_ref, slope_ref, esel_ref, gexp_ref,
                 o_ref, m_ref, l_ref, acc_ref, lhs_ref, ocmp_ref, unsel_ref, words_ref, list_ref,
                 *, tq, nchunk):
    nselp = LANES
    step = pl.program_id(2)
    qs = step * tq
    m_rows = Q_PER_KV * tq
    qb = q_ref[...].reshape(m_rows, LANES).astype(BF16)
    tpos = qs + lax.broadcasted_iota(jnp.int32, (tq, 1), 0)

    def per_head(x):
        return jnp.concatenate([x] * Q_PER_KV, axis=0)

    def add_by_head(s, fn):
        return jnp.concatenate([s[r * tq:(r + 1) * tq] + fn(r) for r in range(Q_PER_KV)], axis=0)

    def slope(r):
        return slope_ref[r, 0:1, 0:1]

    def choose_blocks(qblk_ref, start):
        q = qblk_ref[...].reshape(m_rows, LANES)
        pos_q = start + lax.broadcasted_iota(jnp.int32, (tq, 1), 0)
        kvc = kvc_ref[0, 0]
        cmp_end = lax.broadcasted_iota(jnp.int32, (tq, nchunk), 1) * CMP_STRIDE + (2 * CMP_STRIDE - 1)
        cmp_mask = per_head(pos_q >= cmp_end)
        cmp_pos = (lax.broadcasted_iota(jnp.int32, (1, nchunk), 1) * CMP_STRIDE
                   + (2 * CMP_STRIDE - 1)).astype(F32)
        s = lax.dot_general(q.astype(BF16), kvc, NT_DIMS, preferred_element_type=F32)
        s = jnp.where(cmp_mask, add_by_head(s, lambda r: slope(r) * cmp_pos), NEG_INF)
        p = jnp.exp(s - jnp.max(s, axis=-1, keepdims=True)) * jnp.where(cmp_mask, 1.0, 0.0)
        p = p / jnp.maximum(jnp.sum(p, axis=-1, keepdims=True), TINY)
        ocmp_ref[...] = _dot(p.astype(BF16), kvc)
        psum = p[0:tq] + p[tq:2 * tq] + p[2 * tq:3 * tq] + p[3 * tq:4 * tq]
        pool = jnp.where((lax.broadcasted_iota(jnp.int32, (nchunk, nselp), 0) >> 2)
                         == lax.broadcasted_iota(jnp.int32, (nchunk, nselp), 1), 1.0, 0.0).astype(BF16)
        hi, mid, lo = _split3(psum)
        imp = _dot(hi, pool) + _dot(mid, pool) + _dot(lo, pool)
        blk = lax.broadcasted_iota(jnp.int32, (tq, nselp), 1)
        cur = pos_q >> 6
        forced = (blk == 0) | (blk == cur) | (blk == cur - 1)
        valid = blk <= cur
        val = jnp.where(valid, imp + jnp.where(forced, FORCE_BONUS, 0.0), NEG_INF)
        unsel = _top_blocks_unselected(val, valid)
        unsel_ref[...] = unsel
        lhs_ref[...] = jnp.where(_lo_half(m_rows), q.astype(F32), per_head(_swap_halves(unsel))).astype(BF16)

    @pl.when(step == 0)
    def _():
        choose_blocks(q_ref, qs)

    m_ref[...] = jnp.full_like(m_ref, NEG_INF)
    l_ref[...] = jnp.zeros_like(l_ref)
    acc_ref[...] = jnp.zeros_like(acc_ref)

    def sel_chunk(stream, kv_t, e_c, k0, causal_bias, lhs=None):
        sel_update(stream, kv_t, sel_scores(kv_t, e_c, k0, causal_bias, lhs))

    def sel_scores(kv_t, e_c, k0, causal_bias=None, lhs=None):
        n = kv_t.shape[1]
        lhs = lhs_ref[...] if lhs is None else lhs
        s = _dot(lhs, jnp.concatenate([kv_t[0:HEAD_DIM], e_c], axis=0))
        pos = (k0 + lax.broadcasted_iota(jnp.int32, (1, n), 1)).astype(F32)
        if causal_bias is None:
            return add_by_head(s, lambda r: slope(r) * pos)
        return add_by_head(s, lambda r: slope(r) * pos + causal_bias)

    def sel_update(stream, kv_t, s):
        m_old = m_ref[stream]
        m_new = jnp.maximum(m_old, jnp.max(s, axis=-1, keepdims=True))
        alpha = jnp.exp(m_old - m_new)
        p = [jnp.exp(s[:, c * LANES:(c + 1) * LANES] - m_new) for c in range(s.shape[1] // LANES)]
        part = p[0]
        for c in range(1, len(p)):
            part = part + p[c]
        l_ref[stream] = alpha * l_ref[stream] + part
        acc_ref[stream] = alpha * acc_ref[stream] + lax.dot_general(
            jnp.concatenate(p, axis=1).astype(BF16), kv_t, NT_DIMS, preferred_element_type=F32)
        m_ref[stream] = m_new

    def past_chunk(stream, slot):
        kc = list_ref[slot]
        sel_chunk(stream, kvs_ref[0, 0, kc], esel_ref[kc], kc * SEL_CHUNK, None)

    def group(i, carry):
        kcs = [list_ref[SEL_STREAMS * i + stream] for stream in range(SEL_STREAMS)]
        scores = [sel_scores(kvs_ref[0, 0, kc], esel_ref[kc], kc * SEL_CHUNK) for kc in kcs]
        for stream in range(SEL_STREAMS):
            sel_update(stream, kvs_ref[0, 0, kcs[stream]], scores[stream])
        return carry

    def single(i, carry):
        past_chunk(0, n_need - 1 - i)
        return carry

    n_full = qs // SEL_CHUNK
    n_need = _list_needed_chunks(unsel_ref[...], n_full, words_ref, list_ref)
    lax.fori_loop(0, n_need // SEL_STREAMS, group, 0)
    lax.fori_loop(0, n_need % SEL_STREAMS, single, 0)

    lhs_now = lhs_ref[...]
    o_cmp = ocmp_ref[...]
    next_step = jnp.minimum(step + 1, pl.num_programs(2) - 1)
    choose_blocks(qnext_ref, next_step * tq)

    key_j = lax.broadcasted_iota(jnp.int32, (tq, SEL_CHUNK), 1)
    causal = jnp.where(tpos >= n_full * SEL_CHUNK + key_j, 0.0, NEG_INF)
    diag_scores = sel_scores(kvs_ref[0, 0, n_full], esel_ref[n_full], n_full * SEL_CHUNK, causal, lhs_now)

    tile0 = jnp.maximum(qs - WINDOW, 0) // KEY_TILE
    tiles = [kvw_ref[0, 0, tile0 + j] for j in range(WIN_TILES)]
    key_b = lax.broadcasted_iota(jnp.int32, (1, KEY_TILE), 1)
    s = []
    for j in range(WIN_TILES):
        pos = (tile0 + j) * KEY_TILE + key_b
        dist = tpos - pos
        bias = jnp.where((dist >= 0) & (dist <= WINDOW), 0.0, NEG_INF)
        s.append(add_by_head(_dot(qb, tiles[j]), lambda r: slope(r) * pos.astype(F32) + bias))
    sel_update(SEL_STREAMS - 1, kvs_ref[0, 0, n_full], diag_scores)
    o_sel = _merge_streams(m_ref, l_ref, lambda st: acc_ref[st])
    m = s[0]
    for j in range(1, WIN_TILES):
        m = jnp.maximum(m, s[j])
    m = jnp.max(m, axis=-1, keepdims=True)
    p = [jnp.exp(sj - m) for sj in s]
    part = p[0]
    for j in range(1, WIN_TILES):
        part = part + p[j]
    o_win = lax.dot_general(p[0].astype(BF16), tiles[0], NT_DIMS, preferred_element_type=F32)
    for j in range(1, WIN_TILES):
        o_win = o_win + lax.dot_general(p[j].astype(BF16), tiles[j], NT_DIMS, preferred_element_type=F32)
    o_win = o_win / jnp.maximum(jnp.sum(part, axis=-1, keepdims=True), TINY)

    hi, mid, lo = _split3(jax.nn.sigmoid(misc_ref[...]))
    gexp = gexp_ref[0]
    gates = _dot(hi, gexp) + _dot(mid, gexp) + _dot(lo, gexp)
    comb = []
    for r in range(Q_PER_KV):
        rs = slice(r * tq, (r + 1) * tq)
        gate = [gates[:, (3 * r + c) * LANES:(3 * r + c + 1) * LANES] for c in range(3)]
        comb.append(gate[0] * o_cmp[rs] + gate[1] * o_sel[rs] + gate[2] * o_win[rs])
    lo_half = _lo_half(tq)
    o_ref[:, 0:LANES] = jnp.where(lo_half, _swap_halves(comb[0]), comb[1])
    o_ref[:, LANES:2 * LANES] = jnp.where(lo_half, _swap_halves(comb[2]), comb[3])


def _attn(tables, qhm, u, kvcmp, kvsel, kvwin, nseq, t):
    slope_tab, esel, gexp = tables
    tq = Q_BLOCK
    nqb = t // tq
    nchunk = kvcmp.shape[2]
    sel_chunks = kvsel.shape[2]
    win_tiles = kvwin.shape[2]
    m_rows = Q_PER_KV * tq
    assert esel.shape[1] == HEAD_DIM
    in_specs = [pl.BlockSpec((Q_PER_KV, tq, LANES), lambda s, g, i: (g, s * nqb + i, 0)),
                pl.BlockSpec((Q_PER_KV, tq, LANES), lambda s, g, i: (g, s * nqb + jnp.minimum(i + 1, nqb - 1), 0)),
                pl.BlockSpec((tq, LANES), lambda s, g, i: (s * nqb + i, U_MISC // LANES)),
                pl.BlockSpec((1, 1, nchunk, LANES), lambda s, g, i: (s, g, 0, 0)),
                pl.BlockSpec((1, 1, sel_chunks, 2 * HEAD_DIM, SEL_CHUNK), lambda s, g, i: (s, g, 0, 0, 0)),
                pl.BlockSpec((1, 1, win_tiles, 2 * HEAD_DIM, KEY_TILE), lambda s, g, i: (g, s, 0, 0, 0)),
                pl.BlockSpec((Q_PER_KV, 8, LANES), lambda s, g, i: (g, 0, 0)),
                pl.BlockSpec(esel.shape, lambda s, g, i: (0, 0, 0)),
                pl.BlockSpec((1, LANES, 12 * LANES), lambda s, g, i: (g, 0, 0))]
    args = [qhm, qhm, u, kvcmp, kvsel, kvwin, slope_tab, esel, gexp]
    stream_state = pltpu.VMEM((SEL_STREAMS, m_rows, LANES), F32)
    return pl.pallas_call(
        functools.partial(_attn_kernel, tq=tq, nchunk=nchunk),
        grid=(nseq, KV_HEADS, nqb),
        in_specs=in_specs,
        out_specs=pl.BlockSpec((tq, 2 * LANES), lambda s, g, i: (s * nqb + i, g)),
        out_shape=jax.ShapeDtypeStruct((nseq * t, D_ATTN), F32),
        scratch_shapes=[stream_state, stream_state, stream_state,
                        pltpu.VMEM((m_rows, LANES), BF16), pltpu.VMEM((m_rows, LANES), F32),
                        pltpu.VMEM((tq, LANES), F32),
                        pltpu.SMEM((-(-sel_chunks // CHUNKS_PER_WORD),), jnp.int32),
                        pltpu.SMEM((sel_chunks,), jnp.int32)],
        compiler_params=_cparams("parallel", "parallel", "arbitrary"),
        name="nsa_attn",
    )(*args)


def _attn_sample_kernel(q_ref, misc_ref, kvc_ref, kvs_ref, kvw_ref, tail_ref, slope_ref, esel_ref, gexp_ref,
                        o_ref, m_ref, l_ref, acc_ref, words_ref, list_ref, *, tq, q0, nchunk, nselp, lmain):
    n_rows = ATTN_HEADS * tq
    grp_rows = Q_PER_KV * tq
    wide = KV_HEADS * LANES
    q = q_ref[...].reshape(n_rows, LANES)
    qb = q.astype(BF16)
    tpos = q0 + lax.broadcasted_iota(jnp.int32, (tq, 1), 0)
    slope_col = jnp.concatenate([jnp.broadcast_to(slope_ref[h, 0:1, 0:1], (tq, 1)) for h in range(ATTN_HEADS)],
                                axis=0)
    grp_shift = grp_rows.bit_length() - 1
    own = ((lax.broadcasted_iota(jnp.int32, (n_rows, wide), 1) >> 7)
           == (lax.broadcasted_iota(jnp.int32, (n_rows, wide), 0) >> grp_shift))
    q_diag = jnp.where(own, jnp.concatenate([q.astype(F32)] * KV_HEADS, axis=1), 0.0).astype(BF16)
    row_grp = lax.broadcasted_iota(jnp.int32, (n_rows, LANES), 0) >> grp_shift

    def per_rows(x, copies):
        return jnp.concatenate([x] * copies, axis=0)

    def own_block(x):
        out = x[:, 0:LANES]
        for g in range(1, KV_HEADS):
            out = jnp.where(row_grp == g, x[:, g * LANES:(g + 1) * LANES], out)
        return out

    cmp_end = lax.broadcasted_iota(jnp.int32, (tq, nchunk), 1) * CMP_STRIDE + (2 * CMP_STRIDE - 1)
    cmp_mask = per_rows(tpos >= cmp_end, ATTN_HEADS)
    cmp_pos = (lax.broadcasted_iota(jnp.int32, (1, nchunk), 1) * CMP_STRIDE + (2 * CMP_STRIDE - 1)).astype(F32)
    grp = lambda x, g: x[g * grp_rows:(g + 1) * grp_rows]
    s = jnp.concatenate([lax.dot_general(grp(qb, g), kvc_ref[0, g], NT_DIMS, preferred_element_type=F32)
                         for g in range(KV_HEADS)], axis=0)
    s = jnp.where(cmp_mask, s + slope_col * cmp_pos, NEG_INF)
    p = jnp.exp(s - jnp.max(s, axis=-1, keepdims=True)) * jnp.where(cmp_mask, 1.0, 0.0)
    p = p / jnp.maximum(jnp.sum(p, axis=-1, keepdims=True), TINY)
    o_cmp = jnp.concatenate([_dot(grp(p, g).astype(BF16), kvc_ref[0, g]) for g in range(KV_HEADS)], axis=0)
    head_p = lambda h: p[h * tq:(h + 1) * tq]
    psum = jnp.concatenate(
        [head_p(4 * g) + head_p(4 * g + 1) + head_p(4 * g + 2) + head_p(4 * g + 3) for g in range(KV_HEADS)],
        axis=0)

    pool = jnp.where((lax.broadcasted_iota(jnp.int32, (nchunk, nselp), 0) >> 2)
                     == lax.broadcasted_iota(jnp.int32, (nchunk, nselp), 1), 1.0, 0.0).astype(BF16)
    hi, mid, lo = _split3(psum)
    imp = _dot(hi, pool) + _dot(mid, pool) + _dot(lo, pool)
    blk = lax.broadcasted_iota(jnp.int32, (grp_rows, nselp), 1)
    cur = per_rows(tpos, KV_HEADS) >> 6
    forced = (blk == 0) | (blk == cur) | (blk == cur - 1)
    valid = blk <= cur
    val = jnp.where(valid, imp + jnp.where(forced, FORCE_BONUS, 0.0), NEG_INF)
    unsel = _top_blocks_unselected(val, valid)
    unsel_rows = jnp.concatenate([unsel[(h // Q_PER_KV) * tq:(h // Q_PER_KV + 1) * tq]
                                  for h in range(ATTN_HEADS)], axis=0).astype(BF16)

    key_b = lax.broadcasted_iota(jnp.int32, (1, KEY_TILE), 1)
    tiles = [kvw_ref[0, :, j].reshape(wide, KEY_TILE) for j in range(WIN_TILES)]
    s = []
    for j in range(WIN_TILES):
        pos = q0 - WINDOW + j * KEY_TILE + key_b
        dist = tpos - pos
        bias = per_rows(jnp.where((dist >= 0) & (dist <= WINDOW), 0.0, NEG_INF), ATTN_HEADS)
        s.append(_dot(q_diag, tiles[j]) + slope_col * pos.astype(F32) + bias)
    m = s[0]
    for j in range(1, WIN_TILES):
        m = jnp.maximum(m, s[j])
    m = jnp.max(m, axis=-1, keepdims=True)
    p = [jnp.exp(sj - m) for sj in s]
    part = p[0]
    for j in range(1, WIN_TILES):
        part = part + p[j]
    o_win = lax.dot_general(p[0].astype(BF16), tiles[0], NT_DIMS, preferred_element_type=F32)
    for j in range(1, WIN_TILES):
        o_win = o_win + lax.dot_general(p[j].astype(BF16), tiles[j], NT_DIMS, preferred_element_type=F32)
    o_win = own_block(o_win) / jnp.maximum(jnp.sum(part, axis=-1, keepdims=True), TINY)

    m_ref[...] = jnp.full_like(m_ref, NEG_INF)
    l_ref[...] = jnp.zeros_like(l_ref)
    acc_ref[...] = jnp.zeros_like(acc_ref)

    def sel_chunk(stream, kv_t, e_c, k0, causal_bias):
        sel_update(stream, kv_t, sel_scores(kv_t, e_c, k0, causal_bias))

    def sel_scores(kv_t, e_c, k0, causal_bias=None):
        pos = (k0 + lax.broadcasted_iota(jnp.int32, (1, kv_t.shape[1]), 1)).astype(F32)
        s = _dot(q_diag, kv_t) + _dot(unsel_rows, e_c) + slope_col * pos
        return s if causal_bias is None else s + causal_bias

    def sel_update(stream, kv_t, s):
        m_old = m_ref[stream]
        m_new = jnp.maximum(m_old, jnp.max(s, axis=-1, keepdims=True))
        alpha = jnp.exp(m_old - m_new)
        p = [jnp.exp(s[:, c * LANES:(c + 1) * LANES] - m_new) for c in range(s.shape[1] // LANES)]
        part = p[0]
        for c in range(1, len(p)):
            part = part + p[c]
        l_ref[stream] = alpha * l_ref[stream] + part
        acc_ref[stream] = jnp.concatenate([alpha] * KV_HEADS, axis=1) * acc_ref[stream] + lax.dot_general(
            jnp.concatenate(p, axis=1).astype(BF16), kv_t, NT_DIMS, preferred_element_type=F32)
        m_ref[stream] = m_new

    def past_chunk(stream, slot):
        kc = list_ref[slot]
        sel_chunk(stream, kvs_ref[0, :, kc].reshape(wide, SEL_CHUNK), esel_ref[kc], kc * SEL_CHUNK, None)

    def group(i, carry):
        kcs = [list_ref[SEL_STREAMS * i + stream] for stream in range(SEL_STREAMS)]
        tiles_t = [kvs_ref[0, :, kc].reshape(wide, SEL_CHUNK) for kc in kcs]
        scores = [sel_scores(kv_t, esel_ref[kc], kc * SEL_CHUNK) for kv_t, kc in zip(tiles_t, kcs)]
        for stream in range(SEL_STREAMS):
            sel_update(stream, tiles_t[stream], scores[stream])
        return carry

    def single(i, carry):
        past_chunk(0, n_need - 1 - i)
        return carry

    n_need = _list_needed_chunks(unsel, lmain // SEL_CHUNK, words_ref, list_ref)
    lax.fori_loop(0, n_need // SEL_STREAMS, group, 0)
    lax.fori_loop(0, n_need % SEL_STREAMS, single, 0)

    key_j = lax.broadcasted_iota(jnp.int32, (tq, SEL_CHUNK), 1)
    causal = per_rows(jnp.where(tpos >= lmain + key_j, 0.0, NEG_INF), ATTN_HEADS)
    sel_chunk(SEL_STREAMS - 1, tail_ref[0].reshape(wide, SEL_CHUNK), esel_ref[lmain // SEL_CHUNK], lmain, causal)
    o_sel = _merge_streams(m_ref, l_ref, lambda st: own_block(acc_ref[st]))

    hi, mid, lo = _split3(jax.nn.sigmoid(misc_ref[...]))
    comb = []
    for g in range(KV_HEADS):
        gates = _dot(hi, gexp_ref[g]) + _dot(mid, gexp_ref[g]) + _dot(lo, gexp_ref[g])
        for r in range(Q_PER_KV):
            hs = slice((g * Q_PER_KV + r) * tq, (g * Q_PER_KV + r + 1) * tq)
            gate = [gates[:, (3 * r + c) * LANES:(3 * r + c + 1) * LANES] for c in range(3)]
            comb.append(gate[0] * o_cmp[hs] + gate[1] * o_sel[hs] + gate[2] * o_win[hs])
    lo_half = _lo_half(tq)
    for i in range(ATTN_HEADS // 2):
        o_ref[:, i * LANES:(i + 1) * LANES] = jnp.where(lo_half, _swap_halves(comb[2 * i]), comb[2 * i + 1])


def _attn_sample(tables, qhm, u, kvcmp, kvsel, kvwin, tail, nseq, t, q0):
    slope_tab, esel, gexp = tables
    nchunk = kvcmp.shape[2]
    sel_chunks = kvsel.shape[2]
    n_rows = ATTN_HEADS * t
    full = lambda a: pl.BlockSpec(a.shape, lambda s: (0,) * a.ndim)
    per_seq = lambda a: pl.BlockSpec((1,) + a.shape[1:], lambda s: (s,) + (0,) * (a.ndim - 1))
    wide_state = pltpu.VMEM((SEL_STREAMS, n_rows, KV_HEADS * LANES), F32)
    lane_state = pltpu.VMEM((SEL_STREAMS, n_rows, LANES), F32)
    return pl.pallas_call(
        functools.partial(_attn_sample_kernel, tq=t, q0=q0, nchunk=nchunk, nselp=esel.shape[1],
                          lmain=sel_chunks * SEL_CHUNK),
        grid=(nseq,),
        in_specs=[pl.BlockSpec((ATTN_HEADS, t, LANES), lambda s: (0, s, 0)),
                  pl.BlockSpec((t, LANES), lambda s: (s, U_MISC // LANES)),
                  per_seq(kvcmp), per_seq(kvsel), per_seq(kvwin), per_seq(tail),
                  full(slope_tab), full(esel), full(gexp)],
        out_specs=pl.BlockSpec((t, D_ATTN), lambda s: (s, 0)),
        out_shape=jax.ShapeDtypeStruct((nseq * t, D_ATTN), F32),
        scratch_shapes=[lane_state, lane_state, wide_state,
                        pltpu.SMEM((-(-sel_chunks // CHUNKS_PER_WORD),), jnp.int32),
                        pltpu.SMEM((sel_chunks,), jnp.int32)],
        compiler_params=_cparams("parallel"),
        name="nsa_attn_sample",
    )(qhm, u, kvcmp, kvsel, kvwin, tail, slope_tab, esel, gexp)


def _layer_params(l, norm_mix, w_in, conv_a_w, conv_b_w, conv_b_bias, dt_bias, a_log, d_skip,
                  q_norm, k_norm, cmp_pos, cmp_w1, cmp_w2, norm_out, w_out, norm_ffn, w_gate, w_up, w_down):
    w = w_in[l]
    w_perm = jnp.concatenate(
        [w[:, 4104:5640], w[:, 0:1536], w[:, 2048:3072], w[:, 3080:4104], w[:, 1536:2048],
         w[:, 3072:3080], w[:, 5640:5688], jnp.zeros((D_MODEL, U_COLS - 5688), F32)], axis=1).astype(BF16)
    pad8 = lambda v: jnp.pad(v, (0, LANES - v.shape[0])).reshape(1, LANES)
    head_id = jnp.arange(256) // HEAD_DIM
    pos = cmp_pos[l].reshape(4, 1024)
    w1 = cmp_w1[l]
    w2 = cmp_w2[l]
    zeros_w2 = jnp.zeros((CMP_HIDDEN, HEAD_DIM), F32)
    return dict(
        g_mix=norm_mix[l].reshape(1, D_MODEL), w_in=w_perm,
        conv_a_w=conv_a_w[l], conv_b_w=conv_b_w[l], conv_b_bias=conv_b_bias[l].reshape(1, SSM_CONV_DIM),
        dt_bias=pad8(dt_bias[l]), a_log=pad8(a_log[l]),
        d_skip=jnp.repeat(d_skip[l], HEAD_DIM).reshape(1, D_SSM),
        mavg=jnp.where(head_id[:, None] == head_id[None, :], 1.0 / HEAD_DIM, 0.0).astype(BF16),
        q_gain=jnp.tile(q_norm[l], 4).reshape(1, 256),
        k_gain1=jnp.tile(k_norm[l, 1], 4).reshape(1, 256),
        k_gain2=jnp.tile(k_norm[l, 2], 4).reshape(1, 256),
        k_gain0=pad8(k_norm[l, 0]),
        pos=jnp.pad(pos, ((0, 4), (0, 0))),
        w1k=jnp.concatenate([w1[0, 0:1024], w1[0, 1024:2048]], axis=1).astype(BF16),
        w1v=jnp.concatenate([w1[1, 0:1024], w1[1, 1024:2048]], axis=1).astype(BF16),
        w2k=jnp.concatenate([w2[0], zeros_w2], axis=1).astype(BF16),
        w2v=jnp.concatenate([zeros_w2, w2[1]], axis=1).astype(BF16),
        g_out=norm_out[l].reshape(1, D_MODEL), w_out=w_out[l].astype(BF16),
        g_ffn=norm_ffn[l].reshape(1, D_MODEL),
        w_gate=w_gate[l].astype(BF16), w_up=w_up[l].astype(BF16), w_down=w_down[l].astype(BF16),
    )


def _group_tables(slopes, t, q0):
    lmain = q0 if q0 else t
    n_sel_chunks = lmain // SEL_CHUNK + (1 if q0 else 0)
    n_blocks = n_sel_chunks * SEL_CHUNK // SEL_BLOCK
    if q0:
        n_blocks = -(-n_blocks // LANES) * LANES
    else:
        assert n_blocks <= HEAD_DIM
        n_blocks = HEAD_DIM
    return _attn_tables(slopes, n_sel_chunks, n_blocks)


def _layer(x, nseq, t, q0, p, tables, conv_a_prefix, conv_b_prefix, ssm_h0, past):
    u = _in_proj(x, p["g_mix"], p["w_in"])
    ya, new_conv_a = _mixer_a(u, conv_a_prefix, p["conv_a_w"], nseq, t)
    yb, new_conv_b, new_ssm = _ssd(u, conv_b_prefix, ssm_h0, p["conv_b_w"], p["conv_b_bias"],
                                   p["dt_bias"], p["a_log"], p["d_skip"], nseq, t)
    prep_out = _prep(u, p["mavg"], p["q_gain"], p["k_gain1"], p["k_gain2"],
                     BF16 if past is None else F32, past is None)
    qhm, new_k, new_v, kwin = prep_out[:4]
    vwin = u[:, U_KV + 1280:U_KV + 1536]
    cmp_args = (p["pos"], p["w1k"], p["w1v"], p["w2k"], p["w2v"], p["k_gain0"])
    if past is None:
        npages = t // PAGE_SIZE
        ptab = jnp.arange(nseq * npages, dtype=jnp.int32)
        kvcmp, kvsel = _pagepass(ptab, new_k.reshape(-1, PAGE_SIZE, PAGE_COLS),
                                 new_v.reshape(-1, PAGE_SIZE, PAGE_COLS), *cmp_args, nseq, npages)
        kvwin_arr = prep_out[4].reshape(KV_HEADS, nseq, t // KEY_TILE, 2 * HEAD_DIM, KEY_TILE)
        yc = _attn(tables, qhm, u, kvcmp, kvsel, kvwin_arr, nseq, t)
        new_win_k = kwin.reshape(nseq, t, 256)[:, t - WINDOW:]
        new_win_v = vwin.reshape(nseq, t, 256)[:, t - WINDOW:]
    else:
        ptab, k_pages, v_pages, npages, win_k_t, win_v_t, win_k, win_v = past
        kvcmp, kvsel = _pagepass(ptab, k_pages, v_pages, *cmp_args, nseq, npages)
        kvwin_arr, tail = _winpack(win_k_t, win_v_t, kwin, u, new_k, new_v, nseq, t)
        yc = _attn_sample(tables, qhm, u, kvcmp, kvsel, kvwin_arr, tail, nseq, t, q0)
        new_win_k = jnp.concatenate([win_k[:, t:], kwin.reshape(nseq, t, 256)], axis=1)
        new_win_v = jnp.concatenate([win_v[:, t:], vwin.reshape(nseq, t, 256)], axis=1)
    x = _out_proj(x, ya, yb, yc, p["g_out"], p["w_out"])
    x = _ffn(x, p["g_ffn"], p["w_gate"], p["w_up"], p["w_down"])
    hd = (KV_HEADS, HEAD_DIM)
    state = (new_k.reshape(nseq, t, 2, *hd), new_v.reshape(nseq, t, 2, *hd),
             new_win_k.reshape(nseq, WINDOW, *hd), new_win_v.reshape(nseq, WINDOW, *hd),
             new_conv_a, new_conv_b, new_ssm)
    return x, state


def kernel(x_prompt, x_sample, cache_k, cache_v, cache_win_k, cache_win_v, state_conv_a, state_conv_b,
           state_ssm, page_table, norm_mix, w_in, conv_a_w, conv_b_w, conv_b_bias, dt_bias, a_log, d_skip,
           q_norm, k_norm, cmp_pos, cmp_w1, cmp_w2, norm_out, w_out, norm_ffn, w_gate, w_up, w_down):
    bsz, t_prompt, _ = x_prompt.shape
    dec_b, t_dec, _ = x_sample.shape
    depth, n_pool = cache_k.shape[0], cache_k.shape[1]
    npages = page_table.shape[1]
    past_len = npages * PAGE_SIZE
    assert cache_win_k.shape[2] == WINDOW and t_prompt >= WINDOW + Q_BLOCK and t_prompt % Q_BLOCK == 0
    assert t_dec == 8 and npages % PAGES_PER_STEP == 0 and (t_prompt // PAGE_SIZE) % PAGES_PER_STEP == 0

    slopes = jnp.exp2(-8.0 * jnp.arange(1, ATTN_HEADS + 1, dtype=F32) / ATTN_HEADS)
    prompt_tables = _group_tables(slopes, t_prompt, 0)
    sample_tables = _group_tables(slopes, t_dec, past_len)
    to_pages_t = lambda c: jnp.transpose(c, (0, 1, 3, 4, 5, 2)).reshape(depth * n_pool, PAGE_COLS, PAGE_SIZE)
    to_win_t = lambda w: jnp.transpose(w, (0, 2, 3, 1)).reshape(dec_b, 256, WINDOW)
    k_pages = to_pages_t(cache_k)
    v_pages = to_pages_t(cache_v)
    hp = x_prompt.reshape(bsz * t_prompt, D_MODEL)
    hs = x_sample.reshape(dec_b * t_dec, D_MODEL)
    zeros = lambda *shape: jnp.zeros(shape, F32)
    prompt_states, sample_states = [], []
    for l in range(depth):
        p = _layer_params(l, norm_mix, w_in, conv_a_w, conv_b_w, conv_b_bias, dt_bias, a_log, d_skip,
                          q_norm, k_norm, cmp_pos, cmp_w1, cmp_w2, norm_out, w_out, norm_ffn,
                          w_gate, w_up, w_down)
        hp, st_p = _layer(hp, bsz, t_prompt, 0, p, prompt_tables,
                          zeros(bsz, CONV_A_WIDTH - 1, D_CONV), zeros(bsz, SSM_CONV_WIDTH - 1, SSM_CONV_DIM),
                          zeros(bsz, SSM_HEADS, HEAD_DIM, SSM_STATE), None)
        ptab = (page_table + l * n_pool).reshape(-1).astype(jnp.int32)
        past = (ptab, k_pages, v_pages, npages, to_win_t(cache_win_k[l]), to_win_t(cache_win_v[l]),
                cache_win_k[l].reshape(dec_b, WINDOW, 256), cache_win_v[l].reshape(dec_b, WINDOW, 256))
        hs, st_s = _layer(hs, dec_b, t_dec, past_len, p, sample_tables,
                          state_conv_a[l], state_conv_b[l], state_ssm[l], past)
        prompt_states.append(st_p)
        sample_states.append(st_s)

    stack = lambda states, j: jnp.stack([s[j] for s in states], axis=0)
    return ((hp.reshape(bsz, t_prompt, D_MODEL), hs.reshape(dec_b, t_dec, D_MODEL))
            + tuple(stack(prompt_states, j) for j in range(7))
            + tuple(stack(sample_states, j) for j in range(7)))
```

```python
import functools

import jax
import jax.numpy as jnp
from jax import lax
from jax.experimental import pallas as pl
from jax.experimental.pallas import tpu as pltpu

F32 = jnp.float32
BF16 = jnp.bfloat16

D_MODEL = 2048
HEAD_DIM = 64
D_CONV = 512
D_SSM = 512
D_ATTN = 1024
CONV_A_WIDTH = 3
SSM_HEADS = 8
SSM_STATE = 128
SSM_CONV_WIDTH = 4
SSM_CHUNK = 128
SSM_CONV_DIM = 1024
ATTN_HEADS = 16
KV_HEADS = 4
Q_PER_KV = 4
KV_COLS = 1536
CMP_STRIDE = 16
CMP_HIDDEN = 128
SEL_BLOCK = 64
TOP_BLOCKS = 16
WINDOW = 512
Q_BLOCK = 128
PAGE_SIZE = 128
PAGE_COLS = 512
D_FF = 5632
RMS_EPS = 1e-6
NEG_INF = -1e30
TINY = 1e-30
FORCE_BONUS = 1e3
ATTN_SCALE = HEAD_DIM ** -0.5

U_KV = 0
U_A = 1536
U_XBC = 3072
U_Q = 4096
U_Z = 5120
U_MISC = 5632
U_COLS = 5760
GATE_LANE0 = 8

LANES = 128
KEY_TILE = 128
SEL_CHUNK = 256
VMEM_LIMIT_BYTES = 56 * 2 ** 20

NT_DIMS = (((1,), (1,)), ((), ()))


def _cparams(*sem):
    return pltpu.CompilerParams(dimension_semantics=sem, vmem_limit_bytes=VMEM_LIMIT_BYTES)


def _tile(n, pref):
    t = min(n, pref)
    while n % t:
        t //= 2
    return t


def _dot(a, b):
    return jnp.dot(a, b, preferred_element_type=F32)


def _split3(x):
    hi = x.astype(BF16)
    r1 = x - hi.astype(F32)
    mid = r1.astype(BF16)
    lo = (r1 - mid.astype(F32)).astype(BF16)
    return hi, mid, lo


def _lo_half(rows):
    return lax.broadcasted_iota(jnp.int32, (rows, LANES), 1) < HEAD_DIM


def _swap_halves(x):
    return pltpu.roll(x, HEAD_DIM, 1)


def _store_kv_t(dst, k_t, v_t):
    for g in range(KV_HEADS):
        view = dst(g)
        view[0:HEAD_DIM, :] = k_t[g * HEAD_DIM:(g + 1) * HEAD_DIM].astype(BF16)
        view[HEAD_DIM:2 * HEAD_DIM, :] = v_t[g * HEAD_DIM:(g + 1) * HEAD_DIM].astype(BF16)


def _in_proj_kernel(x_ref, g_ref, w_ref, o_ref, xn_ref):
    @pl.when(pl.program_id(1) == 0)
    def _():
        x = x_ref[...]
        ms = jnp.mean(x * x, axis=-1, keepdims=True)
        xn_ref[...] = (x * lax.rsqrt(ms + RMS_EPS) * g_ref[...]).astype(BF16)

    o_ref[...] = _dot(xn_ref[...], w_ref[...])


def _in_proj(x, gain, w):
    n = x.shape[0]
    tm = _tile(n, 1024)
    tn = 1152
    return pl.pallas_call(
        _in_proj_kernel,
        grid=(n // tm, U_COLS // tn),
        in_specs=[pl.BlockSpec((tm, D_MODEL), lambda i, j: (i, 0)),
                  pl.BlockSpec((1, D_MODEL), lambda i, j: (0, 0)),
                  pl.BlockSpec((D_MODEL, tn), lambda i, j: (0, j))],
        out_specs=pl.BlockSpec((tm, tn), lambda i, j: (i, j)),
        out_shape=jax.ShapeDtypeStruct((n, U_COLS), F32),
        scratch_shapes=[pltpu.VMEM((tm, D_MODEL), BF16)],
        compiler_params=_cparams("parallel", "arbitrary"),
        name="in_proj",
    )(x, gain, w)


def _out_proj_kernel(x_ref, ya_ref, yb_ref, yc_ref, g_ref, w_ref, o_ref):
    def nrm(y, g):
        ms = jnp.mean(y * y, axis=-1, keepdims=True)
        return (y * lax.rsqrt(ms + RMS_EPS) * g).astype(BF16)

    o_ref[...] = (x_ref[...]
                  + _dot(nrm(ya_ref[...], g_ref[:, 0:512]), w_ref[0:512, :])
                  + _dot(nrm(yb_ref[...], g_ref[:, 512:1024]), w_ref[512:1024, :])
                  + _dot(nrm(yc_ref[...], g_ref[:, 1024:2048]), w_ref[1024:2048, :]))


def _out_proj(x, ya, yb, yc, gain, w):
    n = x.shape[0]
    tm = _tile(n, 512)
    return pl.pallas_call(
        _out_proj_kernel,
        grid=(n // tm,),
        in_specs=[pl.BlockSpec((tm, D_MODEL), lambda i: (i, 0)),
                  pl.BlockSpec((tm, D_CONV), lambda i: (i, 0)),
                  pl.BlockSpec((tm, D_SSM), lambda i: (i, 0)),
                  pl.BlockSpec((tm, D_ATTN), lambda i: (i, 0)),
                  pl.BlockSpec((1, D_MODEL), lambda i: (0, 0)),
                  pl.BlockSpec((D_MODEL, D_MODEL), lambda i: (0, 0))],
        out_specs=pl.BlockSpec((tm, D_MODEL), lambda i: (i, 0)),
        out_shape=jax.ShapeDtypeStruct((n, D_MODEL), F32),
        compiler_params=_cparams("parallel"),
        name="out_proj",
    )(x, ya, yb, yc, gain, w)


FFN_CHAINS = 2


def _ffn_kernel(x_ref, g_ref, wg_ref, wu_ref, wd_ref, o_ref, h_ref, acc_ref):
    f = pl.program_id(1)

    @pl.when(f == 0)
    def _():
        x = x_ref[...]
        ms = jnp.mean(x * x, axis=-1, keepdims=True)
        h_ref[...] = (x * lax.rsqrt(ms + RMS_EPS) * g_ref[...]).astype(BF16)
        acc_ref[...] = jnp.zeros_like(acc_ref)

    h = h_ref[...]
    tf = wg_ref.shape[1]
    part = None
    for c in range(FFN_CHAINS):
        cs = slice(c * tf // FFN_CHAINS, (c + 1) * tf // FFN_CHAINS)
        a = _dot(h, wg_ref[:, cs])
        b = _dot(h, wu_ref[:, cs])
        d = _dot((a * jax.nn.sigmoid(a) * b).astype(BF16), wd_ref[cs, :])
        part = d if part is None else part + d
    acc_ref[...] += part

    @pl.when(f == pl.num_programs(1) - 1)
    def _():
        o_ref[...] = x_ref[...] + acc_ref[...]


def _ffn(x, gain, wg, wu, wd):
    n = x.shape[0]
    tm = _tile(n, 512)
    tf = 512
    return pl.pallas_call(
        _ffn_kernel,
        grid=(n // tm, D_FF // tf),
        in_specs=[pl.BlockSpec((tm, D_MODEL), lambda i, f: (i, 0)),
                  pl.BlockSpec((1, D_MODEL), lambda i, f: (0, 0)),
                  pl.BlockSpec((D_MODEL, tf), lambda i, f: (0, f)),
                  pl.BlockSpec((D_MODEL, tf), lambda i, f: (0, f)),
                  pl.BlockSpec((tf, D_MODEL), lambda i, f: (f, 0))],
        out_specs=pl.BlockSpec((tm, D_MODEL), lambda i, f: (i, 0)),
        out_shape=jax.ShapeDtypeStruct((n, D_MODEL), F32),
        scratch_shapes=[pltpu.VMEM((tm, D_MODEL), BF16), pltpu.VMEM((tm, D_MODEL), F32)],
        compiler_params=_cparams("parallel", "arbitrary"),
        name="ffn",
    )(x, gain, wg, wu, wd)


def _mixer_a_kernel(u_ref, pre_ref, w_ref, y_ref, newpre_ref, ext_ref, *, rows):
    @pl.when(pl.program_id(1) == 0)
    def _():
        ext_ref[6:8, :] = pre_ref[0]

    u = u_ref[...]
    v = u[:, 512:1024] * u[:, 1024:1536]
    ext_ref[8:8 + rows, :] = v
    w = w_ref[...]
    y = ext_ref[6:6 + rows, :] * w[0:1] + ext_ref[7:7 + rows, :] * w[1:2] + v * w[2:3]
    y_ref[...] = u[:, 0:512] * y
    last = ext_ref[8 + rows - 2:8 + rows, :]
    ext_ref[6:8, :] = last
    newpre_ref[0] = last


def _mixer_a(u, prefix, w, nseq, t):
    rows = _tile(t, 512)
    nblk = t // rows
    return pl.pallas_call(
        functools.partial(_mixer_a_kernel, rows=rows),
        grid=(nseq, nblk),
        in_specs=[pl.BlockSpec((rows, 3 * D_CONV), lambda s, j: (s * nblk + j, U_A // (3 * D_CONV))),
                  pl.BlockSpec((1, CONV_A_WIDTH - 1, D_CONV), lambda s, j: (s, 0, 0)),
                  pl.BlockSpec((CONV_A_WIDTH, D_CONV), lambda s, j: (0, 0))],
        out_specs=[pl.BlockSpec((rows, D_CONV), lambda s, j: (s * nblk + j, 0)),
                   pl.BlockSpec((1, CONV_A_WIDTH - 1, D_CONV), lambda s, j: (s, 0, 0))],
        out_shape=[jax.ShapeDtypeStruct((nseq * t, D_CONV), F32),
                   jax.ShapeDtypeStruct((nseq, CONV_A_WIDTH - 1, D_CONV), F32)],
        scratch_shapes=[pltpu.VMEM((8 + rows, D_CONV), F32)],
        compiler_params=_cparams("parallel", "arbitrary"),
        name="mixer_a",
    )(u, prefix, w)


def _ssd_kernel(z_ref, xbc_ref, misc_ref, pre_ref, h0_ref, cw_ref, cb_ref, dtb_ref, alog_ref, dsk_ref,
                y_ref, newpre_ref, hout_ref, ext_ref, st_ref, pad_ref, *, lin):
    L = SSM_CHUNK
    c = pl.program_id(1)

    @pl.when(c == 0)
    def _():
        if lin < L:
            ext_ref[...] = jnp.zeros_like(ext_ref)
            pad_ref[...] = jnp.zeros_like(pad_ref)
        ext_ref[5:8, :] = pre_ref[0]
        st_ref[...] = h0_ref[0].reshape(SSM_HEADS * HEAD_DIM, SSM_STATE)

    ext_ref[8:8 + lin, :] = xbc_ref[...]
    cw = cw_ref[...]
    acc = (ext_ref[5:5 + L, :] * cw[0:1] + ext_ref[6:6 + L, :] * cw[1:2]
           + ext_ref[7:7 + L, :] * cw[2:3] + ext_ref[8:8 + L, :] * cw[3:4])
    acc = acc + cb_ref[...]
    xbc = acc * jax.nn.sigmoid(acc)
    newp = ext_ref[8 + lin - 3:8 + lin, :]
    ext_ref[5:8, :] = newp
    newpre_ref[0] = newp

    if lin < L:
        pad_ref[0:lin, :] = misc_ref[...]
        misc = pad_ref[...]
    else:
        misc = misc_ref[...]
    rows = lax.broadcasted_iota(jnp.int32, (L, L), 0)
    cols = lax.broadcasted_iota(jnp.int32, (L, L), 1)
    x = misc + dtb_ref[...]
    dt = jnp.maximum(x, 0.0) + jnp.log1p(jnp.exp(-jnp.abs(x)))
    if lin < L:
        dt = jnp.where(rows < lin, dt, 0.0)
    la = dt * (-jnp.exp(alog_ref[...]))

    causal = rows >= cols
    tril = jnp.where(causal, 1.0, 0.0).astype(BF16)
    hi, mid, lo = _split3(la)
    cum = _dot(tril, hi) + _dot(tril, mid) + _dot(tril, lo)
    cum_t = cum.T
    last = cum[L - 1:L, :]
    lo_half = cols < HEAD_DIM
    top_rows = rows < HEAD_DIM

    z = z_ref[...]
    dsk = dsk_ref[...]
    for g in range(2):
        bg = xbc[:, 512 + g * 128:512 + (g + 1) * 128].astype(BF16)
        cg = xbc[:, 768 + g * 128:768 + (g + 1) * 128].astype(BF16)
        gram = lax.dot_general(cg, bg, NT_DIMS, preferred_element_type=F32)
        for i in (2 * g, 2 * g + 1):
            a, b = 2 * i, 2 * i + 1
            sl = slice(i * LANES, (i + 1) * LANES)
            xs_p = xbc[:, sl]
            col_a = cum[:, a:a + 1]
            col_b = cum[:, b:b + 1]
            xdt = xs_p * jnp.where(lo_half, dt[:, a:a + 1], dt[:, b:b + 1])
            dec_a = jnp.exp(jnp.where(causal, col_a - cum_t[a:a + 1, :], NEG_INF))
            dec_b = jnp.exp(jnp.where(causal, col_b - cum_t[b:b + 1, :], NEG_INF))
            xa = jnp.where(lo_half, xdt, 0.0).astype(BF16)
            xb = jnp.where(lo_half, 0.0, xdt).astype(BF16)
            y_intra = _dot((gram * dec_a).astype(BF16), xa) + _dot((gram * dec_b).astype(BF16), xb)
            st = st_ref[sl, :]
            y_inter = (lax.dot_general(cg, st.astype(BF16), NT_DIMS, preferred_element_type=F32)
                       * jnp.where(lo_half, jnp.exp(col_a), jnp.exp(col_b)))
            to_end = jnp.where(lo_half, jnp.exp(last[:, a:a + 1] - col_a), jnp.exp(last[:, b:b + 1] - col_b))
            xw_t = (xdt * to_end).T.astype(BF16)
            decay = jnp.where(top_rows, jnp.exp(last[:, a:a + 1]), jnp.exp(last[:, b:b + 1]))
            st_ref[sl, :] = st * decay + _dot(xw_t, bg)
            y = y_intra + y_inter + dsk[:, sl] * xs_p
            zp = z[:, sl]
            y_ref[:, sl] = y[0:lin] * (zp * jax.nn.sigmoid(zp))

    @pl.when(c == pl.num_programs(1) - 1)
    def _():
        hout_ref[0] = st_ref[...].reshape(SSM_HEADS, HEAD_DIM, SSM_STATE)


def _ssd(u, prefix, h0, cw, cb, dtb, alog, dsk, nseq, t):
    lin = min(t, SSM_CHUNK)
    nc = t // lin
    full = lambda shape: pl.BlockSpec(shape, lambda s, c: (0,) * len(shape))
    return pl.pallas_call(
        functools.partial(_ssd_kernel, lin=lin),
        grid=(nseq, nc),
        in_specs=[pl.BlockSpec((lin, D_SSM), lambda s, c: (s * nc + c, U_Z // D_SSM)),
                  pl.BlockSpec((lin, SSM_CONV_DIM), lambda s, c: (s * nc + c, U_XBC // SSM_CONV_DIM)),
                  pl.BlockSpec((lin, LANES), lambda s, c: (s * nc + c, U_MISC // LANES)),
                  pl.BlockSpec((1, SSM_CONV_WIDTH - 1, SSM_CONV_DIM), lambda s, c: (s, 0, 0)),
                  pl.BlockSpec((1, SSM_HEADS, HEAD_DIM, SSM_STATE), lambda s, c: (s, 0, 0, 0)),
                  full((SSM_CONV_WIDTH, SSM_CONV_DIM)), full((1, SSM_CONV_DIM)),
                  full((1, LANES)), full((1, LANES)), full((1, D_SSM))],
        out_specs=[pl.BlockSpec((lin, D_SSM), lambda s, c: (s * nc + c, 0)),
                   pl.BlockSpec((1, SSM_CONV_WIDTH - 1, SSM_CONV_DIM), lambda s, c: (s, 0, 0)),
                   pl.BlockSpec((1, SSM_HEADS, HEAD_DIM, SSM_STATE), lambda s, c: (s, 0, 0, 0))],
        out_shape=[jax.ShapeDtypeStruct((nseq * t, D_SSM), F32),
                   jax.ShapeDtypeStruct((nseq, SSM_CONV_WIDTH - 1, SSM_CONV_DIM), F32),
                   jax.ShapeDtypeStruct((nseq, SSM_HEADS, HEAD_DIM, SSM_STATE), F32)],
        scratch_shapes=[pltpu.VMEM((8 + SSM_CHUNK, SSM_CONV_DIM), F32),
                        pltpu.VMEM((SSM_HEADS * HEAD_DIM, SSM_STATE), F32),
                        pltpu.VMEM((SSM_CHUNK, LANES), F32)],
        compiler_params=_cparams("parallel", "arbitrary"),
        name="ssd",
    )(u, u, u, prefix, h0, cw, cb, dtb, alog, dsk)


def _headnorm(x, mavg, gain):
    x2 = x * x
    hi = x2.astype(BF16)
    lo = (x2 - hi.astype(F32)).astype(BF16)
    ms = _dot(hi, mavg) + _dot(lo, mavg)
    return x * lax.rsqrt(ms + RMS_EPS) * gain


def _prep_kernel(q_ref, kv_ref, mavg_ref, qg_ref, kg1_ref, kg2_ref,
                 qhm_ref, nk_ref, nv_ref, kwin_ref, vwin_ref, *maybe_kvwin_ref):
    mavg = mavg_ref[...]
    lo = _lo_half(q_ref.shape[0])
    for c in range(4):
        qn = _headnorm(q_ref[:, c * 256:(c + 1) * 256], mavg, qg_ref[...]) * ATTN_SCALE
        for cc in range(2):
            col = qn[:, cc * LANES:(cc + 1) * LANES]
            h = c * 4 + cc * 2
            qhm_ref[h] = jnp.where(lo, col, 0.0).astype(qhm_ref.dtype)
            qhm_ref[h + 1] = jnp.where(lo, _swap_halves(col), 0.0).astype(qhm_ref.dtype)
    ksel = _headnorm(kv_ref[:, 512:768], mavg, kg1_ref[...])
    kwin = _headnorm(kv_ref[:, 1024:1280], mavg, kg2_ref[...])
    nk_ref[:, 0:256] = kv_ref[:, 0:256]
    nk_ref[:, 256:512] = ksel
    nv_ref[:, 0:256] = kv_ref[:, 256:512]
    nv_ref[:, 256:512] = kv_ref[:, 768:1024]
    kwin_ref[...] = kwin
    vwin_ref[...] = kv_ref[:, 1280:1536]
    if maybe_kvwin_ref:
        kvwin_ref, = maybe_kvwin_ref
        for c in range(q_ref.shape[0] // KEY_TILE):
            rs = slice(c * KEY_TILE, (c + 1) * KEY_TILE)
            _store_kv_t(lambda g: kvwin_ref.at[g, c], kwin[rs].T, kv_ref[rs, 1280:1536].T)


def _prep(u, mavg, qg, kg1, kg2, q_dtype, emit_window_tiles):
    n = u.shape[0]
    tm = _tile(n, 256)
    full = lambda shape: pl.BlockSpec(shape, lambda i: (0,) * len(shape))
    out_specs = [pl.BlockSpec((ATTN_HEADS, tm, LANES), lambda i: (0, i, 0)),
                 pl.BlockSpec((tm, PAGE_COLS), lambda i: (i, 0)),
                 pl.BlockSpec((tm, PAGE_COLS), lambda i: (i, 0)),
                 pl.BlockSpec((tm, 256), lambda i: (i, 0)),
                 pl.BlockSpec((tm, 256), lambda i: (i, 0))]
    out_shape = [jax.ShapeDtypeStruct((ATTN_HEADS, n, LANES), q_dtype),
                 jax.ShapeDtypeStruct((n, PAGE_COLS), F32),
                 jax.ShapeDtypeStruct((n, PAGE_COLS), F32),
                 jax.ShapeDtypeStruct((n, 256), F32),
                 jax.ShapeDtypeStruct((n, 256), F32)]
    if emit_window_tiles:
        out_specs.append(pl.BlockSpec((KV_HEADS, tm // KEY_TILE, 2 * HEAD_DIM, KEY_TILE), lambda i: (0, i, 0, 0)))
        out_shape.append(jax.ShapeDtypeStruct((KV_HEADS, n // KEY_TILE, 2 * HEAD_DIM, KEY_TILE), BF16))
    return pl.pallas_call(
        _prep_kernel,
        grid=(n // tm,),
        in_specs=[pl.BlockSpec((tm, D_ATTN), lambda i: (i, U_Q // D_ATTN)),
                  pl.BlockSpec((tm, KV_COLS), lambda i: (i, 0)),
                  full((256, 256)), full((1, 256)), full((1, 256)), full((1, 256))],
        out_specs=out_specs,
        out_shape=out_shape,
        compiler_params=_cparams("parallel"),
        name="nsa_prep",
    )(u, u, mavg, qg, kg1, kg2)


PAGES_PER_STEP = 16


def _gelu_tanh(x):
    return 0.5 * x * (1.0 + jnp.tanh(0.7978845608028654 * (x + 0.044715 * (x * x * x))))


def _pagepass_kernel(pt_ref, *refs, nchunk, feature_major):
    pp = PAGES_PER_STEP

    pi = lax.broadcasted_iota(jnp.int32, (PAGE_SIZE, PAGE_SIZE), 0)
    pk = lax.broadcasted_iota(jnp.int32, (PAGE_SIZE, PAGE_SIZE), 1)
    perm = jnp.where(pk == CMP_STRIDE * (pi & 7) + (pi >> 3), 1.0, 0.0).astype(BF16)

    def cmp_rows_permuted(page, c):
        if feature_major:
            return lax.dot_general(perm, page[0, c * LANES:(c + 1) * LANES, :].astype(BF16), NT_DIMS,
                                   preferred_element_type=F32)
        return _dot(perm, page[0, :, c * LANES:(c + 1) * LANES].astype(BF16))

    def sel_t(page):
        return page[0, 256:512, :] if feature_major else page[0, :, 256:512].T

    kpages = refs[0:pp]
    vpages = refs[pp:2 * pp]
    pos_ref, w1k_ref, w1v_ref, w2k_ref, w2v_ref, kg_ref = refs[2 * pp:2 * pp + 6]
    kvcmp_ref, kvsel_ref = refs[2 * pp + 6:2 * pp + 8]
    xk_ref, xv_ref, hs_ref = refs[2 * pp + 8:]
    j = pl.program_id(1)
    lo16 = _lo_half(16)

    for i2 in range(pp // 2):
        r0 = pl.multiple_of((j * pp + 2 * i2) * 8, 16)
        for pages, xs_ref in ((kpages, xk_ref), (vpages, xv_ref)):
            for c in range(2):
                pa = cmp_rows_permuted(pages[2 * i2], c)
                pb = cmp_rows_permuted(pages[2 * i2 + 1], c)
                for a in range(8):
                    ev, od = slice(16 * a, 16 * a + 8), slice(16 * a + 8, 16 * a + 16)
                    ec = jnp.concatenate([pa[ev], pb[ev]], axis=0)
                    oc = jnp.concatenate([pa[od], pb[od]], axis=0)
                    xs_ref[2 * c, pl.ds(r0, 16), a * LANES:(a + 1) * LANES] = (
                        jnp.where(lo16, ec, _swap_halves(oc)).astype(BF16))
                    xs_ref[2 * c + 1, pl.ds(r0, 16), a * LANES:(a + 1) * LANES] = (
                        jnp.where(lo16, _swap_halves(ec), oc).astype(BF16))

    per_chunk = SEL_CHUNK // PAGE_SIZE
    for i in range(pp):
        ls = slice((i % per_chunk) * PAGE_SIZE, (i % per_chunk + 1) * PAGE_SIZE)
        _store_kv_t(lambda g: kvsel_ref.at[0, g, i // per_chunk, :, ls], sel_t(kpages[i]), sel_t(vpages[i]))

    @pl.when(j == pl.num_programs(1) - 1)
    def _():
        for slot in range(2 * KV_HEADS):
            hs_ref[slot, nchunk:nchunk + 8, :] = jnp.zeros((8, 256), F32)
        pos = pos_ref[...]
        phi = pos.astype(BF16)
        plo = (pos - phi.astype(F32)).astype(BF16)
        bias_k = _dot(phi, w1k_ref[...]) + _dot(plo, w1k_ref[...])
        bias_v = _dot(phi, w1v_ref[...]) + _dot(plo, w1v_ref[...])
        bias_k = bias_k[0:1, 0:128] + bias_k[1:2, 128:256]
        bias_v = bias_v[2:3, 0:128] + bias_v[3:4, 128:256]

        for g in range(KV_HEADS):
            hs_ref[2 * g, 0:nchunk, :] = _dot(xk_ref[g], w1k_ref[...])
            hs_ref[2 * g + 1, 0:nchunk, :] = _dot(xv_ref[g], w1v_ref[...])

        def summarise(slot, bias, w2_ref):
            pre = hs_ref[slot, 0:nchunk, 0:128] + hs_ref[slot, 1:nchunk + 1, 128:256] + bias
            return _dot(_gelu_tanh(pre).astype(BF16), w2_ref[...])

        for g in range(KV_HEADS):
            ko = summarise(2 * g, bias_k, w2k_ref)
            vo = summarise(2 * g + 1, bias_v, w2v_ref)
            ms = jnp.sum(ko * ko, axis=-1, keepdims=True) * (1.0 / HEAD_DIM)
            kvcmp_ref[0, g] = (ko * lax.rsqrt(ms + RMS_EPS) * kg_ref[...] + vo).astype(BF16)


def _pagepass(ptab, kpages, vpages, pos, w1k, w1v, w2k, w2v, kg, nseq, npages):
    pp = PAGES_PER_STEP
    nsteps = npages // pp
    nchunk = npages * PAGE_SIZE // CMP_STRIDE
    feature_major = kpages.shape[1] == PAGE_COLS
    chunks_per_step = pp * PAGE_SIZE // SEL_CHUNK

    def page_spec(i):
        return pl.BlockSpec((1,) + kpages.shape[1:],
                            lambda s, j, pt: (pt[s * npages + j * pp + i], 0, 0))

    full = lambda shape: pl.BlockSpec(shape, lambda s, j, pt: (0,) * len(shape))
    grid_spec = pltpu.PrefetchScalarGridSpec(
        num_scalar_prefetch=1,
        grid=(nseq, nsteps),
        in_specs=([page_spec(i) for i in range(pp)] + [page_spec(i) for i in range(pp)]
                  + [full((8, 1024)), full((1024, 256)), full((1024, 256)),
                     full((128, 128)), full((128, 128)), full((1, 128))]),
        out_specs=[pl.BlockSpec((1, KV_HEADS, nchunk, LANES), lambda s, j, pt: (s, 0, 0, 0)),
                   pl.BlockSpec((1, KV_HEADS, chunks_per_step, 2 * HEAD_DIM, SEL_CHUNK),
                                lambda s, j, pt: (s, 0, j, 0, 0))],
        scratch_shapes=[pltpu.VMEM((KV_HEADS, nchunk, 1024), BF16),
                        pltpu.VMEM((KV_HEADS, nchunk, 1024), BF16),
                        pltpu.VMEM((2 * KV_HEADS, nchunk + 8, 256), F32)],
    )
    return pl.pallas_call(
        functools.partial(_pagepass_kernel, nchunk=nchunk, feature_major=feature_major),
        grid_spec=grid_spec,
        out_shape=[jax.ShapeDtypeStruct((nseq, KV_HEADS, nchunk, LANES), BF16),
                   jax.ShapeDtypeStruct((nseq, KV_HEADS, npages * PAGE_SIZE // SEL_CHUNK, 2 * HEAD_DIM, SEL_CHUNK),
                                        BF16)],
        compiler_params=_cparams("parallel", "arbitrary"),
        name="pagepass",
    )(ptab, *([kpages] * pp), *([vpages] * pp), pos, w1k, w1v, w2k, w2v, kg)


WIN_TILES = (WINDOW + Q_BLOCK) // KEY_TILE


def _winpack_kernel(wk_ref, wv_ref, kwn_ref, vwn_ref, ksn_ref, vsn_ref, kvw_ref, tail_ref):
    t = kwn_ref.shape[0]
    past_tiles = WINDOW // KEY_TILE
    for c in range(past_tiles):
        ls = slice(c * KEY_TILE, (c + 1) * KEY_TILE)
        _store_kv_t(lambda g: kvw_ref.at[0, g, c], wk_ref[0, :, ls], wv_ref[0, :, ls])
    pad_t = lambda ref, n: jnp.concatenate([ref[...], jnp.zeros((n - t, 256), F32)], axis=0).T
    _store_kv_t(lambda g: kvw_ref.at[0, g, past_tiles], pad_t(kwn_ref, KEY_TILE), pad_t(vwn_ref, KEY_TILE))
    _store_kv_t(lambda g: tail_ref.at[0, g], pad_t(ksn_ref, SEL_CHUNK), pad_t(vsn_ref, SEL_CHUNK))


def _winpack(wk_t, wv_t, kwin, u, new_k, new_v, nseq, t):
    tile = (1, KV_HEADS, 2 * HEAD_DIM, SEL_CHUNK)
    return pl.pallas_call(
        _winpack_kernel,
        grid=(nseq,),
        in_specs=[pl.BlockSpec((1, 256, WINDOW), lambda s: (s, 0, 0)),
                  pl.BlockSpec((1, 256, WINDOW), lambda s: (s, 0, 0)),
                  pl.BlockSpec((t, 256), lambda s: (s, 0)),
                  pl.BlockSpec((t, 256), lambda s: (s, 5)),
                  pl.BlockSpec((t, 256), lambda s: (s, 1)),
                  pl.BlockSpec((t, 256), lambda s: (s, 1))],
        out_specs=[pl.BlockSpec((1, KV_HEADS, WIN_TILES, 2 * HEAD_DIM, KEY_TILE), lambda s: (s, 0, 0, 0, 0)),
                   pl.BlockSpec(tile, lambda s: (s, 0, 0, 0))],
        out_shape=[jax.ShapeDtypeStruct((nseq, KV_HEADS, WIN_TILES, 2 * HEAD_DIM, KEY_TILE), BF16),
                   jax.ShapeDtypeStruct((nseq,) + tile[1:], BF16)],
        compiler_params=_cparams("parallel"),
        name="winpack",
    )(wk_t, wv_t, kwin, u, new_k, new_v)


MASK_BIAS = -2.0 ** 100


def _attn_tables(slopes, n_sel_chunks, n_blocks):
    slope_tab = jnp.broadcast_to(slopes[:, None, None], (ATTN_HEADS, 8, LANES))
    pos = (jnp.arange(n_sel_chunks, dtype=jnp.int32)[:, None, None] * SEL_CHUNK
           + jnp.arange(SEL_CHUNK, dtype=jnp.int32)[None, None, :])
    blk = jnp.arange(n_blocks, dtype=jnp.int32)[None, :, None]
    esel = jnp.where(blk == (pos >> 6), MASK_BIAS, 0.0).astype(BF16)
    lane_src = jnp.arange(LANES)[None, :, None]
    gate_id = jnp.arange(12)[None, None, :]
    g_id = jnp.arange(KV_HEADS)[:, None, None]
    onehot = (lane_src == GATE_LANE0 + g_id * 12 + gate_id).astype(BF16)
    gexp = jnp.broadcast_to(onehot[..., None], (KV_HEADS, LANES, 12, LANES)).reshape(KV_HEADS, LANES, 12 * LANES)
    return slope_tab, esel, gexp


def _top_blocks_unselected(val, valid):
    tq, nselp = val.shape
    if tq < LANES:
        val = jnp.concatenate([val, jnp.zeros((LANES - tq, nselp), F32)], axis=0)
    val_t = jnp.concatenate([val[:, c * LANES:(c + 1) * LANES].T for c in range(nselp // LANES)], axis=0)
    blk_t = lax.broadcasted_iota(jnp.int32, (nselp, LANES), 0).astype(F32)

    def pick_top(_, carry):
        v, sel = carry
        best = jnp.max(v, axis=0, keepdims=True)
        idx = jnp.min(jnp.where(v == best, blk_t, 1e9), axis=0, keepdims=True)
        pick = blk_t == idx
        return jnp.where(pick, -3e38, v), jnp.where(pick, 1.0, sel)

    _, sel_t = lax.fori_loop(0, TOP_BLOCKS, pick_top, (val_t, jnp.zeros((nselp, LANES), F32)), unroll=True)
    sel = jnp.concatenate([sel_t[c * LANES:(c + 1) * LANES].T for c in range(nselp // LANES)], axis=1)
    return jnp.where(valid & (sel[0:tq] > 0.5), 0.0, 1.0)


CHUNKS_PER_WORD = 8
SEL_STREAMS = 2


def _merge_streams(m_ref, l_ref, acc_of):
    m = m_ref[0]
    for st in range(1, SEL_STREAMS):
        m = jnp.maximum(m, m_ref[st])
    l_sum = None
    acc = None
    for st in range(SEL_STREAMS):
        w = jnp.exp(m_ref[st] - m)
        l_sum = w * l_ref[st] if l_sum is None else l_sum + w * l_ref[st]
        acc = w * acc_of(st) if acc is None else acc + w * acc_of(st)
    return acc / jnp.maximum(jnp.sum(l_sum, axis=-1, keepdims=True), TINY)


def _list_needed_chunks(unsel, n_chunks, words_ref, list_ref):
    nselp = unsel.shape[1]
    sel_any = jnp.max(1.0 - unsel, axis=0, keepdims=True)
    lane = lax.broadcasted_iota(jnp.int32, (1, nselp), 1)
    chunk_shift = (SEL_CHUNK // SEL_BLOCK).bit_length() - 1
    word_shift = chunk_shift + CHUNKS_PER_WORD.bit_length() - 1
    digit = (lane >> chunk_shift) & (CHUNKS_PER_WORD - 1)
    weight = lax.shift_left(jnp.ones_like(lane), 3 * digit).astype(F32)
    for w in range(words_ref.shape[0]):
        in_word = (lane >> word_shift) == w
        words_ref[w] = jnp.sum(jnp.where(in_word, sel_any * weight, 0.0)).astype(jnp.int32)

    def build(kc, cnt):
        used = (words_ref[kc // CHUNKS_PER_WORD] >> (3 * (kc % CHUNKS_PER_WORD))) & 7
        list_ref[cnt] = kc
        return cnt + jnp.where(used > 0, 1, 0)

    return lax.fori_loop(0, n_chunks, build, 0)


def _attn_kernel(q_ref, qnext_ref, misc_ref, kvc_ref, kvs_ref, kvw_ref, slope_ref, esel_ref, gexp_ref,
                 o_ref, m_ref, l_ref, acc_ref, lhs_ref, ocmp_ref, unsel_ref, words_ref, list_ref,
                 *, tq, nchunk):
    nselp = LANES
    step = pl.program_id(2)
    qs = step * tq
    m_rows = Q_PER_KV * tq
    qb = q_ref[...].reshape(m_rows, LANES).astype(BF16)
    tpos = qs + lax.broadcasted_iota(jnp.int32, (tq, 1), 0)

    def per_head(x):
        return jnp.concatenate([x] * Q_PER_KV, axis=0)

    def add_by_head(s, fn):
        return jnp.concatenate([s[r * tq:(r + 1) * tq] + fn(r) for r in range(Q_PER_KV)], axis=0)

    def slope(r):
        return slope_ref[r, 0:1, 0:1]

    def choose_blocks(qblk_ref, start):
        q = qblk_ref[...].reshape(m_rows, LANES)
        pos_q = start + lax.broadcasted_iota(jnp.int32, (tq, 1), 0)
        kvc = kvc_ref[0, 0]
        cmp_end = lax.broadcasted_iota(jnp.int32, (tq, nchunk), 1) * CMP_STRIDE + (2 * CMP_STRIDE - 1)
        cmp_mask = per_head(pos_q >= cmp_end)
        cmp_pos = (lax.broadcasted_iota(jnp.int32, (1, nchunk), 1) * CMP_STRIDE
                   + (2 * CMP_STRIDE - 1)).astype(F32)
        s = lax.dot_general(q.astype(BF16), kvc, NT_DIMS, preferred_element_type=F32)
        s = jnp.where(cmp_mask, add_by_head(s, lambda r: slope(r) * cmp_pos), NEG_INF)
        p = jnp.exp(s - jnp.max(s, axis=-1, keepdims=True)) * jnp.where(cmp_mask, 1.0, 0.0)
        p = p / jnp.maximum(jnp.sum(p, axis=-1, keepdims=True), TINY)
        ocmp_ref[...] = _dot(p.astype(BF16), kvc)
        psum = p[0:tq] + p[tq:2 * tq] + p[2 * tq:3 * tq] + p[3 * tq:4 * tq]
        pool = jnp.where((lax.broadcasted_iota(jnp.int32, (nchunk, nselp), 0) >> 2)
                         == lax.broadcasted_iota(jnp.int32, (nchunk, nselp), 1), 1.0, 0.0).astype(BF16)
        hi, mid, lo = _split3(psum)
        imp = _dot(hi, pool) + _dot(mid, pool) + _dot(lo, pool)
        blk = lax.broadcasted_iota(jnp.int32, (tq, nselp), 1)
        cur = pos_q >> 6
        forced = (blk == 0) | (blk == cur) | (blk == cur - 1)
        valid = blk <= cur
        val = jnp.where(valid, imp + jnp.where(forced, FORCE_BONUS, 0.0), NEG_INF)
        unsel = _top_blocks_unselected(val, valid)
        unsel_ref[...] = unsel
        lhs_ref[...] = jnp.where(_lo_half(m_rows), q.astype(F32), per_head(_swap_halves(unsel))).astype(BF16)

    @pl.when(step == 0)
    def _():
        choose_blocks(q_ref, qs)

    m_ref[...] = jnp.full_like(m_ref, NEG_INF)
    l_ref[...] = jnp.zeros_like(l_ref)
    acc_ref[...] = jnp.zeros_like(acc_ref)

    def sel_chunk(stream, kv_t, e_c, k0, causal_bias, lhs=None):
        sel_update(stream, kv_t, sel_scores(kv_t, e_c, k0, causal_bias, lhs))

    def sel_scores(kv_t, e_c, k0, causal_bias=None, lhs=None):
        n = kv_t.shape[1]
        lhs = lhs_ref[...] if lhs is None else lhs
        s = _dot(lhs, jnp.concatenate([kv_t[0:HEAD_DIM], e_c], axis=0))
        pos = (k0 + lax.broadcasted_iota(jnp.int32, (1, n), 1)).astype(F32)
        if causal_bias is None:
            return add_by_head(s, lambda r: slope(r) * pos)
        return add_by_head(s, lambda r: slope(r) * pos + causal_bias)

    def sel_update(stream, kv_t, s):
        m_old = m_ref[stream]
        m_new = jnp.maximum(m_old, jnp.max(s, axis=-1, keepdims=True))
        alpha = jnp.exp(m_old - m_new)
        p = [jnp.exp(s[:, c * LANES:(c + 1) * LANES] - m_new) for c in range(s.shape[1] // LANES)]
        part = p[0]
        for c in range(1, len(p)):
            part = part + p[c]
        l_ref[stream] = alpha * l_ref[stream] + part
        acc_ref[stream] = alpha * acc_ref[stream] + lax.dot_general(
            jnp.concatenate(p, axis=1).astype(BF16), kv_t, NT_DIMS, preferred_element_type=F32)
        m_ref[stream] = m_new

    def past_chunk(stream, slot):
        kc = list_ref[slot]
        sel_chunk(stream, kvs_ref[0, 0, kc], esel_ref[kc], kc * SEL_CHUNK, None)

    def group(i, carry):
        kcs = [list_ref[SEL_STREAMS * i + stream] for stream in range(SEL_STREAMS)]
        scores = [sel_scores(kvs_ref[0, 0, kc], esel_ref[kc], kc * SEL_CHUNK) for kc in kcs]
        for stream in range(SEL_STREAMS):
            sel_update(stream, kvs_ref[0, 0, kcs[stream]], scores[stream])
        return carry

    def single(i, carry):
        past_chunk(0, n_need - 1 - i)
        return carry

    n_full = qs // SEL_CHUNK
    n_need = _list_needed_chunks(unsel_ref[...], n_full, words_ref, list_ref)
    lax.fori_loop(0, n_need // SEL_STREAMS, group, 0)
    lax.fori_loop(0, n_need % SEL_STREAMS, single, 0)

    lhs_now = lhs_ref[...]
    o_cmp = ocmp_ref[...]
    next_step = jnp.minimum(step + 1, pl.num_programs(2) - 1)
    choose_blocks(qnext_ref, next_step * tq)

    key_j = lax.broadcasted_iota(jnp.int32, (tq, SEL_CHUNK), 1)
    causal = jnp.where(tpos >= n_full * SEL_CHUNK + key_j, 0.0, NEG_INF)
    diag_scores = sel_scores(kvs_ref[0, 0, n_full], esel_ref[n_full], n_full * SEL_CHUNK, causal, lhs_now)

    tile0 = jnp.maximum(qs - WINDOW, 0) // KEY_TILE
    tiles = [kvw_ref[0, 0, tile0 + j] for j in range(WIN_TILES)]
    key_b = lax.broadcasted_iota(jnp.int32, (1, KEY_TILE), 1)
    s = []
    for j in range(WIN_TILES):
        pos = (tile0 + j) * KEY_TILE + key_b
        dist = tpos - pos
        bias = jnp.where((dist >= 0) & (dist <= WINDOW), 0.0, NEG_INF)
        s.append(add_by_head(_dot(qb, tiles[j]), lambda r: slope(r) * pos.astype(F32) + bias))
    sel_update(SEL_STREAMS - 1, kvs_ref[0, 0, n_full], diag_scores)
    o_sel = _merge_streams(m_ref, l_ref, lambda st: acc_ref[st])
    m = s[0]
    for j in range(1, WIN_TILES):
        m = jnp.maximum(m, s[j])
    m = jnp.max(m, axis=-1, keepdims=True)
    p = [jnp.exp(sj - m) for sj in s]
    part = p[0]
    for j in range(1, WIN_TILES):
        part = part + p[j]
    o_win = lax.dot_general(p[0].astype(BF16), tiles[0], NT_DIMS, preferred_element_type=F32)
    for j in range(1, WIN_TILES):
        o_win = o_win + lax.dot_general(p[j].astype(BF16), tiles[j], NT_DIMS, preferred_element_type=F32)
    o_win = o_win / jnp.maximum(jnp.sum(part, axis=-1, keepdims=True), TINY)

    hi, mid, lo = _split3(jax.nn.sigmoid(misc_ref[...]))
    gexp = gexp_ref[0]
    gates = _dot(hi, gexp) + _dot(mid, gexp) + _dot(lo, gexp)
    comb = []
    for r in range(Q_PER_KV):
        rs = slice(r * tq, (r + 1) * tq)
        gate = [gates[:, (3 * r + c) * LANES:(3 * r + c + 1) * LANES] for c in range(3)]
        comb.append(gate[0] * o_cmp[rs] + gate[1] * o_sel[rs] + gate[2] * o_win[rs])
    lo_half = _lo_half(tq)
    o_ref[:, 0:LANES] = jnp.where(lo_half, _swap_halves(comb[0]), comb[1])
    o_ref[:, LANES:2 * LANES] = jnp.where(lo_half, _swap_halves(comb[2]), comb[3])


def _attn(tables, qhm, u, kvcmp, kvsel, kvwin, nseq, t):
    slope_tab, esel, gexp = tables
    tq = Q_BLOCK
    nqb = t // tq
    nchunk = kvcmp.shape[2]
    sel_chunks = kvsel.shape[2]
    win_tiles = kvwin.shape[2]
    m_rows = Q_PER_KV * tq
    assert esel.shape[1] == HEAD_DIM
    in_specs = [pl.BlockSpec((Q_PER_KV, tq, LANES), lambda s, g, i: (g, s * nqb + i, 0)),
                pl.BlockSpec((Q_PER_KV, tq, LANES), lambda s, g, i: (g, s * nqb + jnp.minimum(i + 1, nqb - 1), 0)),
                pl.BlockSpec((tq, LANES), lambda s, g, i: (s * nqb + i, U_MISC // LANES)),
                pl.BlockSpec((1, 1, nchunk, LANES), lambda s, g, i: (s, g, 0, 0)),
                pl.BlockSpec((1, 1, sel_chunks, 2 * HEAD_DIM, SEL_CHUNK), lambda s, g, i: (s, g, 0, 0, 0)),
                pl.BlockSpec((1, 1, win_tiles, 2 * HEAD_DIM, KEY_TILE), lambda s, g, i: (g, s, 0, 0, 0)),
                pl.BlockSpec((Q_PER_KV, 8, LANES), lambda s, g, i: (g, 0, 0)),
                pl.BlockSpec(esel.shape, lambda s, g, i: (0, 0, 0)),
                pl.BlockSpec((1, LANES, 12 * LANES), lambda s, g, i: (g, 0, 0))]
    args = [qhm, qhm, u, kvcmp, kvsel, kvwin, slope_tab, esel, gexp]
    stream_state = pltpu.VMEM((SEL_STREAMS, m_rows, LANES), F32)
    return pl.pallas_call(
        functools.partial(_attn_kernel, tq=tq, nchunk=nchunk),
        grid=(nseq, KV_HEADS, nqb),
        in_specs=in_specs,
        out_specs=pl.BlockSpec((tq, 2 * LANES), lambda s, g, i: (s * nqb + i, g)),
        out_shape=jax.ShapeDtypeStruct((nseq * t, D_ATTN), F32),
        scratch_shapes=[stream_state, stream_state, stream_state,
                        pltpu.VMEM((m_rows, LANES), BF16), pltpu.VMEM((m_rows, LANES), F32),
                        pltpu.VMEM((tq, LANES), F32),
                        pltpu.SMEM((-(-sel_chunks // CHUNKS_PER_WORD),), jnp.int32),
                        pltpu.SMEM((sel_chunks,), jnp.int32)],
        compiler_params=_cparams("parallel", "parallel", "arbitrary"),
        name="nsa_attn",
    )(*args)


def _attn_sample_kernel(q_ref, misc_ref, kvc_ref, kvs_ref, kvw_ref, tail_ref, slope_ref, esel_ref, gexp_ref,
                        o_ref, m_ref, l_ref, acc_ref, words_ref, list_ref, *, tq, q0, nchunk, nselp, lmain):
    n_rows = ATTN_HEADS * tq
    grp_rows = Q_PER_KV * tq
    wide = KV_HEADS * LANES
    q = q_ref[...].reshape(n_rows, LANES)
    qb = q.astype(BF16)
    tpos = q0 + lax.broadcasted_iota(jnp.int32, (tq, 1), 0)
    slope_col = jnp.concatenate([jnp.broadcast_to(slope_ref[h, 0:1, 0:1], (tq, 1)) for h in range(ATTN_HEADS)],
                                axis=0)
    grp_shift = grp_rows.bit_length() - 1
    own = ((lax.broadcasted_iota(jnp.int32, (n_rows, wide), 1) >> 7)
           == (lax.broadcasted_iota(jnp.int32, (n_rows, wide), 0) >> grp_shift))
    q_diag = jnp.where(own, jnp.concatenate([q.astype(F32)] * KV_HEADS, axis=1), 0.0).astype(BF16)
    row_grp = lax.broadcasted_iota(jnp.int32, (n_rows, LANES), 0) >> grp_shift

    def per_rows(x, copies):
        return jnp.concatenate([x] * copies, axis=0)

    def own_block(x):
        out = x[:, 0:LANES]
        for g in range(1, KV_HEADS):
            out = jnp.where(row_grp == g, x[:, g * LANES:(g + 1) * LANES], out)
        return out

    cmp_end = lax.broadcasted_iota(jnp.int32, (tq, nchunk), 1) * CMP_STRIDE + (2 * CMP_STRIDE - 1)
    cmp_mask = per_rows(tpos >= cmp_end, ATTN_HEADS)
    cmp_pos = (lax.broadcasted_iota(jnp.int32, (1, nchunk), 1) * CMP_STRIDE + (2 * CMP_STRIDE - 1)).astype(F32)
    grp = lambda x, g: x[g * grp_rows:(g + 1) * grp_rows]
    s = jnp.concatenate([lax.dot_general(grp(qb, g), kvc_ref[0, g], NT_DIMS, preferred_element_type=F32)
                         for g in range(KV_HEADS)], axis=0)
    s = jnp.where(cmp_mask, s + slope_col * cmp_pos, NEG_INF)
    p = jnp.exp(s - jnp.max(s, axis=-1, keepdims=True)) * jnp.where(cmp_mask, 1.0, 0.0)
    p = p / jnp.maximum(jnp.sum(p, axis=-1, keepdims=True), TINY)
    o_cmp = jnp.concatenate([_dot(grp(p, g).astype(BF16), kvc_ref[0, g]) for g in range(KV_HEADS)], axis=0)
    head_p = lambda h: p[h * tq:(h + 1) * tq]
    psum = jnp.concatenate(
        [head_p(4 * g) + head_p(4 * g + 1) + head_p(4 * g + 2) + head_p(4 * g + 3) for g in range(KV_HEADS)],
        axis=0)

    pool = jnp.where((lax.broadcasted_iota(jnp.int32, (nchunk, nselp), 0) >> 2)
                     == lax.broadcasted_iota(jnp.int32, (nchunk, nselp), 1), 1.0, 0.0).astype(BF16)
    hi, mid, lo = _split3(psum)
    imp = _dot(hi, pool) + _dot(mid, pool) + _dot(lo, pool)
    blk = lax.broadcasted_iota(jnp.int32, (grp_rows, nselp), 1)
    cur = per_rows(tpos, KV_HEADS) >> 6
    forced = (blk == 0) | (blk == cur) | (blk == cur - 1)
    valid = blk <= cur
    val = jnp.where(valid, imp + jnp.where(forced, FORCE_BONUS, 0.0), NEG_INF)
    unsel = _top_blocks_unselected(val, valid)
    unsel_rows = jnp.concatenate([unsel[(h // Q_PER_KV) * tq:(h // Q_PER_KV + 1) * tq]
                                  for h in range(ATTN_HEADS)], axis=0).astype(BF16)

    key_b = lax.broadcasted_iota(jnp.int32, (1, KEY_TILE), 1)
    tiles = [kvw_ref[0, :, j].reshape(wide, KEY_TILE) for j in range(WIN_TILES)]
    s = []
    for j in range(WIN_TILES):
        pos = q0 - WINDOW + j * KEY_TILE + key_b
        dist = tpos - pos
        bias = per_rows(jnp.where((dist >= 0) & (dist <= WINDOW), 0.0, NEG_INF), ATTN_HEADS)
        s.append(_dot(q_diag, tiles[j]) + slope_col * pos.astype(F32) + bias)
    m = s[0]
    for j in range(1, WIN_TILES):
        m = jnp.maximum(m, s[j])
    m = jnp.max(m, axis=-1, keepdims=True)
    p = [jnp.exp(sj - m) for sj in s]
    part = p[0]
    for j in range(1, WIN_TILES):
        part = part + p[j]
    o_win = lax.dot_general(p[0].astype(BF16), tiles[0], NT_DIMS, preferred_element_type=F32)
    for j in range(1, WIN_TILES):
        o_win = o_win + lax.dot_general(p[j].astype(BF16), tiles[j], NT_DIMS, preferred_element_type=F32)
    o_win = own_block(o_win) / jnp.maximum(jnp.sum(part, axis=-1, keepdims=True), TINY)

    m_ref[...] = jnp.full_like(m_ref, NEG_INF)
    l_ref[...] = jnp.zeros_like(l_ref)
    acc_ref[...] = jnp.zeros_like(acc_ref)

    def sel_chunk(stream, kv_t, e_c, k0, causal_bias):
        sel_update(stream, kv_t, sel_scores(kv_t, e_c, k0, causal_bias))

    def sel_scores(kv_t, e_c, k0, causal_bias=None):
        pos = (k0 + lax.broadcasted_iota(jnp.int32, (1, kv_t.shape[1]), 1)).astype(F32)
        s = _dot(q_diag, kv_t) + _dot(unsel_rows, e_c) + slope_col * pos
        return s if causal_bias is None else s + causal_bias

    def sel_update(stream, kv_t, s):
        m_old = m_ref[stream]
        m_new = jnp.maximum(m_old, jnp.max(s, axis=-1, keepdims=True))
        alpha = jnp.exp(m_old - m_new)
        p = [jnp.exp(s[:, c * LANES:(c + 1) * LANES] - m_new) for c in range(s.shape[1] // LANES)]
        part = p[0]
        for c in range(1, len(p)):
            part = part + p[c]
        l_ref[stream] = alpha * l_ref[stream] + part
        acc_ref[stream] = jnp.concatenate([alpha] * KV_HEADS, axis=1) * acc_ref[stream] + lax.dot_general(
            jnp.concatenate(p, axis=1).astype(BF16), kv_t, NT_DIMS, preferred_element_type=F32)
        m_ref[stream] = m_new

    def past_chunk(stream, slot):
        kc = list_ref[slot]
        sel_chunk(stream, kvs_ref[0, :, kc].reshape(wide, SEL_CHUNK), esel_ref[kc], kc * SEL_CHUNK, None)

    def group(i, carry):
        kcs = [list_ref[SEL_STREAMS * i + stream] for stream in range(SEL_STREAMS)]
        tiles_t = [kvs_ref[0, :, kc].reshape(wide, SEL_CHUNK) for kc in kcs]
        scores = [sel_scores(kv_t, esel_ref[kc], kc * SEL_CHUNK) for kv_t, kc in zip(tiles_t, kcs)]
        for stream in range(SEL_STREAMS):
            sel_update(stream, tiles_t[stream], scores[stream])
        return carry

    def single(i, carry):
        past_chunk(0, n_need - 1 - i)
        return carry

    n_need = _list_needed_chunks(unsel, lmain // SEL_CHUNK, words_ref, list_ref)
    lax.fori_loop(0, n_need // SEL_STREAMS, group, 0)
    lax.fori_loop(0, n_need % SEL_STREAMS, single, 0)

    key_j = lax.broadcasted_iota(jnp.int32, (tq, SEL_CHUNK), 1)
    causal = per_rows(jnp.where(tpos >= lmain + key_j, 0.0, NEG_INF), ATTN_HEADS)
    sel_chunk(SEL_STREAMS - 1, tail_ref[0].reshape(wide, SEL_CHUNK), esel_ref[lmain // SEL_CHUNK], lmain, causal)
    o_sel = _merge_streams(m_ref, l_ref, lambda st: own_block(acc_ref[st]))

    hi, mid, lo = _split3(jax.nn.sigmoid(misc_ref[...]))
    comb = []
    for g in range(KV_HEADS):
        gates = _dot(hi, gexp_ref[g]) + _dot(mid, gexp_ref[g]) + _dot(lo, gexp_ref[g])
        for r in range(Q_PER_KV):
            hs = slice((g * Q_PER_KV + r) * tq, (g * Q_PER_KV + r + 1) * tq)
            gate = [gates[:, (3 * r + c) * LANES:(3 * r + c + 1) * LANES] for c in range(3)]
            comb.append(gate[0] * o_cmp[hs] + gate[1] * o_sel[hs] + gate[2] * o_win[hs])
    lo_half = _lo_half(tq)
    for i in range(ATTN_HEADS // 2):
        o_ref[:, i * LANES:(i + 1) * LANES] = jnp.where(lo_half, _swap_halves(comb[2 * i]), comb[2 * i + 1])


def _attn_sample(tables, qhm, u, kvcmp, kvsel, kvwin, tail, nseq, t, q0):
    slope_tab, esel, gexp = tables
    nchunk = kvcmp.shape[2]
    sel_chunks = kvsel.shape[2]
    n_rows = ATTN_HEADS * t
    full = lambda a: pl.BlockSpec(a.shape, lambda s: (0,) * a.ndim)
    per_seq = lambda a: pl.BlockSpec((1,) + a.shape[1:], lambda s: (s,) + (0,) * (a.ndim - 1))
    wide_state = pltpu.VMEM((SEL_STREAMS, n_rows, KV_HEADS * LANES), F32)
    lane_state = pltpu.VMEM((SEL_STREAMS, n_rows, LANES), F32)
    return pl.pallas_call(
        functools.partial(_attn_sample_kernel, tq=t, q0=q0, nchunk=nchunk, nselp=esel.shape[1],
                          lmain=sel_chunks * SEL_CHUNK),
        grid=(nseq,),
        in_specs=[pl.BlockSpec((ATTN_HEADS, t, LANES), lambda s: (0, s, 0)),
                  pl.BlockSpec((t, LANES), lambda s: (s, U_MISC // LANES)),
                  per_seq(kvcmp), per_seq(kvsel), per_seq(kvwin), per_seq(tail),
                  full(slope_tab), full(esel), full(gexp)],
        out_specs=pl.BlockSpec((t, D_ATTN), lambda s: (s, 0)),
        out_shape=jax.ShapeDtypeStruct((nseq * t, D_ATTN), F32),
        scratch_shapes=[lane_state, lane_state, wide_state,
                        pltpu.SMEM((-(-sel_chunks // CHUNKS_PER_WORD),), jnp.int32),
                        pltpu.SMEM((sel_chunks,), jnp.int32)],
        compiler_params=_cparams("parallel"),
        name="nsa_attn_sample",
    )(qhm, u, kvcmp, kvsel, kvwin, tail, slope_tab, esel, gexp)


def _layer_params(l, norm_mix, w_in, conv_a_w, conv_b_w, conv_b_bias, dt_bias, a_log, d_skip,
                  q_norm, k_norm, cmp_pos, cmp_w1, cmp_w2, norm_out, w_out, norm_ffn, w_gate, w_up, w_down):
    w = w_in[l].astype(BF16)
    w_perm = jnp.concatenate(
        [w[:, 4104:5640], w[:, 0:1536], w[:, 2048:3072], w[:, 3080:4104], w[:, 1536:2048],
         w[:, 3072:3080], w[:, 5640:5688], jnp.zeros((D_MODEL, U_COLS - 5688), BF16)], axis=1)
    pad8 = lambda v: jnp.pad(v, (0, LANES - v.shape[0])).reshape(1, LANES)
    head_id = jnp.arange(256) // HEAD_DIM
    pos = cmp_pos[l].reshape(4, 1024)
    w1 = cmp_w1[l]
    w2 = cmp_w2[l]
    zeros_w2 = jnp.zeros((CMP_HIDDEN, HEAD_DIM), F32)
    return dict(
        g_mix=norm_mix[l].reshape(1, D_MODEL), w_in=w_perm,
        conv_a_w=conv_a_w[l], conv_b_w=conv_b_w[l], conv_b_bias=conv_b_bias[l].reshape(1, SSM_CONV_DIM),
        dt_bias=pad8(dt_bias[l]), a_log=pad8(a_log[l]),
        d_skip=jnp.repeat(d_skip[l], HEAD_DIM).reshape(1, D_SSM),
        mavg=jnp.where(head_id[:, None] == head_id[None, :], 1.0 / HEAD_DIM, 0.0).astype(BF16),
        q_gain=jnp.tile(q_norm[l], 4).reshape(1, 256),
        k_gain1=jnp.tile(k_norm[l, 1], 4).reshape(1, 256),
        k_gain2=jnp.tile(k_norm[l, 2], 4).reshape(1, 256),
        k_gain0=pad8(k_norm[l, 0]),
        pos=jnp.pad(pos, ((0, 4), (0, 0))),
        w1k=jnp.concatenate([w1[0, 0:1024], w1[0, 1024:2048]], axis=1).astype(BF16),
        w1v=jnp.concatenate([w1[1, 0:1024], w1[1, 1024:2048]], axis=1).astype(BF16),
        w2k=jnp.concatenate([w2[0], zeros_w2], axis=1).astype(BF16),
        w2v=jnp.concatenate([zeros_w2, w2[1]], axis=1).astype(BF16),
        g_out=norm_out[l].reshape(1, D_MODEL), w_out=w_out[l].astype(BF16),
        g_ffn=norm_ffn[l].reshape(1, D_MODEL),
        w_gate=w_gate[l].astype(BF16), w_up=w_up[l].astype(BF16), w_down=w_down[l].astype(BF16),
    )


def _group_tables(slopes, t, q0):
    lmain = q0 if q0 else t
    n_sel_chunks = lmain // SEL_CHUNK + (1 if q0 else 0)
    n_blocks = n_sel_chunks * SEL_CHUNK // SEL_BLOCK
    if q0:
        n_blocks = -(-n_blocks // LANES) * LANES
    else:
        assert n_blocks <= HEAD_DIM
        n_blocks = HEAD_DIM
    return _attn_tables(slopes, n_sel_chunks, n_blocks)


def _layer(x, nseq, t, q0, p, tables, conv_a_prefix, conv_b_prefix, ssm_h0, past):
    u = _in_proj(x, p["g_mix"], p["w_in"])
    ya, new_conv_a = _mixer_a(u, conv_a_prefix, p["conv_a_w"], nseq, t)
    yb, new_conv_b, new_ssm = _ssd(u, conv_b_prefix, ssm_h0, p["conv_b_w"], p["conv_b_bias"],
                                   p["dt_bias"], p["a_log"], p["d_skip"], nseq, t)
    prep_out = _prep(u, p["mavg"], p["q_gain"], p["k_gain1"], p["k_gain2"],
                     BF16 if past is None else F32, past is None)
    qhm, new_k, new_v, kwin, vwin = prep_out[:5]
    cmp_args = (p["pos"], p["w1k"], p["w1v"], p["w2k"], p["w2v"], p["k_gain0"])
    if past is None:
        npages = t // PAGE_SIZE
        ptab = jnp.arange(nseq * npages, dtype=jnp.int32)
        kvcmp, kvsel = _pagepass(ptab, new_k.reshape(-1, PAGE_SIZE, PAGE_COLS),
                                 new_v.reshape(-1, PAGE_SIZE, PAGE_COLS), *cmp_args, nseq, npages)
        kvwin_arr = prep_out[5].reshape(KV_HEADS, nseq, t // KEY_TILE, 2 * HEAD_DIM, KEY_TILE)
        yc = _attn(tables, qhm, u, kvcmp, kvsel, kvwin_arr, nseq, t)
        new_win_k = kwin.reshape(nseq, t, 256)[:, t - WINDOW:]
        new_win_v = vwin.reshape(nseq, t, 256)[:, t - WINDOW:]
    else:
        ptab, k_pages, v_pages, npages, win_k_t, win_v_t, win_k, win_v = past
        kvcmp, kvsel = _pagepass(ptab, k_pages, v_pages, *cmp_args, nseq, npages)
        kvwin_arr, tail = _winpack(win_k_t, win_v_t, kwin, u, new_k, new_v, nseq, t)
        yc = _attn_sample(tables, qhm, u, kvcmp, kvsel, kvwin_arr, tail, nseq, t, q0)
        new_win_k = jnp.concatenate([win_k[:, t:], kwin.reshape(nseq, t, 256)], axis=1)
        new_win_v = jnp.concatenate([win_v[:, t:], vwin.reshape(nseq, t, 256)], axis=1)
    x = _out_proj(x, ya, yb, yc, p["g_out"], p["w_out"])
    x = _ffn(x, p["g_ffn"], p["w_gate"], p["w_up"], p["w_down"])
    hd = (KV_HEADS, HEAD_DIM)
    state = (new_k.reshape(nseq, t, 2, *hd), new_v.reshape(nseq, t, 2, *hd),
             new_win_k.reshape(nseq, WINDOW, *hd), new_win_v.reshape(nseq, WINDOW, *hd),
             new_conv_a, new_conv_b, new_ssm)
    return x, state


def kernel(x_prompt, x_sample, cache_k, cache_v, cache_win_k, cache_win_v, state_conv_a, state_conv_b,
           state_ssm, page_table, norm_mix, w_in, conv_a_w, conv_b_w, conv_b_bias, dt_bias, a_log, d_skip,
           q_norm, k_norm, cmp_pos, cmp_w1, cmp_w2, norm_out, w_out, norm_ffn, w_gate, w_up, w_down):
    bsz, t_prompt, _ = x_prompt.shape
    dec_b, t_dec, _ = x_sample.shape
    depth, n_pool = cache_k.shape[0], cache_k.shape[1]
    npages = page_table.shape[1]
    past_len = npages * PAGE_SIZE
    assert cache_win_k.shape[2] == WINDOW and t_prompt >= WINDOW + Q_BLOCK and t_prompt % Q_BLOCK == 0
    assert t_dec == 8 and npages % PAGES_PER_STEP == 0 and (t_prompt // PAGE_SIZE) % PAGES_PER_STEP == 0

    slopes = jnp.exp2(-8.0 * jnp.arange(1, ATTN_HEADS + 1, dtype=F32) / ATTN_HEADS)
    prompt_tables = _group_tables(slopes, t_prompt, 0)
    sample_tables = _group_tables(slopes, t_dec, past_len)
    to_pages_t = lambda c: jnp.transpose(c, (0, 1, 3, 4, 5, 2)).reshape(depth * n_pool, PAGE_COLS, PAGE_SIZE)
    to_win_t = lambda w: jnp.transpose(w, (0, 2, 3, 1)).reshape(dec_b, 256, WINDOW)
    k_pages = to_pages_t(cache_k)
    v_pages = to_pages_t(cache_v)
    hp = x_prompt.reshape(bsz * t_prompt, D_MODEL)
    hs = x_sample.reshape(dec_b * t_dec, D_MODEL)
    zeros = lambda *shape: jnp.zeros(shape, F32)
    prompt_states, sample_states = [], []
    for l in range(depth):
        p = _layer_params(l, norm_mix, w_in, conv_a_w, conv_b_w, conv_b_bias, dt_bias, a_log, d_skip,
                          q_norm, k_norm, cmp_pos, cmp_w1, cmp_w2, norm_out, w_out, norm_ffn,
                          w_gate, w_up, w_down)
        hp, st_p = _layer(hp, bsz, t_prompt, 0, p, prompt_tables,
                          zeros(bsz, CONV_A_WIDTH - 1, D_CONV), zeros(bsz, SSM_CONV_WIDTH - 1, SSM_CONV_DIM),
                          zeros(bsz, SSM_HEADS, HEAD_DIM, SSM_STATE), None)
        ptab = (page_table + l * n_pool).reshape(-1).astype(jnp.int32)
        past = (ptab, k_pages, v_pages, npages, to_win_t(cache_win_k[l]), to_win_t(cache_win_v[l]),
                cache_win_k[l].reshape(dec_b, WINDOW, 256), cache_win_v[l].reshape(dec_b, WINDOW, 256))
        hs, st_s = _layer(hs, dec_b, t_dec, past_len, p, sample_tables,
                          state_conv_a[l], state_conv_b[l], state_ssm[l], past)
        prompt_states.append(st_p)
        sample_states.append(st_s)

    stack = lambda states, j: jnp.stack([s[j] for s in states], axis=0)
    return ((hp.reshape(bsz, t_prompt, D_MODEL), hs.reshape(dec_b, t_dec, D_MODEL))
            + tuple(stack(prompt_states, j) for j in range(7))
            + tuple(stack(sample_states, j) for j in range(7)))
```
